```python
import math
import jax, jax.numpy as jnp
from jax import lax
import numpy as np

D_MODEL = 1024
BATCH = 2
SEQ = 8192
DEPTH = 4

BRANCH_W = D_MODEL // 4
D_MIX = 4 * BRANCH_W
HEAD_DIM = 64
EPS = 1e-6
RET_HEADS = BRANCH_W // HEAD_DIM
RET_CHUNK = 64
ROPE_BASE = 10000.0
DN_HEADS = 4
DN_DK = 64
DN_DV = BRANCH_W // DN_HEADS
DN_CONV = 4
DN_CHUNK = 64
DN_QKV = 2 * DN_HEADS * DN_DK + DN_HEADS * DN_DV
S5_GROUP = 16
S5_GROUPS = BRANCH_W // S5_GROUP
S5_STATE = 64
S5_DT_MIN = 1e-3
S5_DT_MAX = 1e-1
GLA_HEADS = 4
GLA_DK = 32
GLA_DV = BRANCH_W // GLA_HEADS
GLA_QK = GLA_HEADS * GLA_DK
GLA_GATE_RANK = 16
GLA_GATE_TAU = 16.0
GLA_CHUNK = 16

IN_SPLITS = [
    BRANCH_W, BRANCH_W, BRANCH_W, BRANCH_W,
    DN_QKV, DN_HEADS, DN_HEADS, BRANCH_W,
    BRANCH_W, BRANCH_W,
    GLA_QK, GLA_QK, BRANCH_W, GLA_GATE_RANK, BRANCH_W,
]
IN_COLS = sum(IN_SPLITS)

kernel_name = "hymba_style_ret_deltanet_s5_gla"

F32 = jnp.float32


def _rmsnorm(x, g):
    x32 = x.astype(F32)
    y = x32 * lax.rsqrt(jnp.mean(x32 * x32, axis=-1, keepdims=True) + EPS)
    return (y * g.astype(F32)).astype(x.dtype)


def _head_rmsnorm(o, g):
    return o * lax.rsqrt(jnp.mean(o * o, axis=-1, keepdims=True) + EPS) * g.astype(F32)


def _l2norm(t):
    return t * lax.rsqrt(jnp.sum(t * t, axis=-1, keepdims=True) + EPS)


def _split_cols(t, sizes):
    offs = np.cumsum(sizes)[:-1].tolist()
    return jnp.split(t, offs, axis=-1)


def _heads(t, n_heads):
    b, l, _ = t.shape
    return t.reshape(b, l, n_heads, -1).transpose(0, 2, 1, 3)


def _merge(o):
    b, h, l, d = o.shape
    return o.transpose(0, 2, 1, 3).reshape(b, l, h * d)


def _chunk(t, c):
    return t.reshape(t.shape[0], t.shape[1], t.shape[2] // c, c, *t.shape[3:])


def _rotary(t, pos):
    d = t.shape[-1]
    inv = ROPE_BASE ** (-jnp.arange(0, d, 2, dtype=F32) / d)
    ang = pos[:, None] * inv[None, :]
    cos, sin = jnp.cos(ang), jnp.sin(ang)
    t1, t2 = t[..., 0::2], t[..., 1::2]
    return jnp.stack([t1 * cos - t2 * sin, t1 * sin + t2 * cos], axis=-1).reshape(t.shape)


def _causal_conv(x, w):
    k, ch = w.shape
    return lax.conv_general_dilated(x, w[:, None, :], window_strides=(1,), padding=((k - 1, 0),),
                                    dimension_numbers=('NWC', 'WIO', 'NWC'), feature_group_count=ch)


def _retention(q, k, v):
    b, h, l, dk = q.shape
    dv = v.shape[-1]
    c = RET_CHUNK
    log_gamma = jnp.log(1.0 - 2.0 ** (-5.0 - jnp.arange(h, dtype=F32)))
    pos = jnp.arange(l, dtype=F32)
    q = _rotary(q, pos) * (dk ** -0.5)
    k = _rotary(k, pos)
    qc, kc, vc = _chunk(q, c), _chunk(k, c), _chunk(v, c)
    idx = jnp.arange(c, dtype=F32)
    diff = idx[:, None] - idx[None, :]
    causal = diff >= 0
    dmat = jnp.where(causal, jnp.exp(log_gamma[:, None, None] * jnp.where(causal, diff, 0.0)), 0.0)
    scores = jnp.einsum('bhnid,bhnjd->bhnij', qc, kc) * dmat[None, :, None]
    o_intra = jnp.einsum('bhnij,bhnje->bhnie', scores, vc)
    q_decay = jnp.exp(log_gamma[:, None] * (idx + 1.0))
    k_decay = jnp.exp(log_gamma[:, None] * (c - 1.0 - idx))
    chunk_decay = jnp.exp(log_gamma * c)
    kv = jnp.einsum('bhncd,bhnce->bhnde', kc * k_decay[None, :, None, :, None], vc)

    def step(state, kv_n):
        return state * chunk_decay[None, :, None, None] + kv_n, state

    _, r_prev = lax.scan(step, jnp.zeros((b, h, dk, dv), F32), jnp.moveaxis(kv, 2, 0))
    r_prev = jnp.moveaxis(r_prev, 0, 2)
    o_inter = jnp.einsum('bhncd,bhnde->bhnce', qc, r_prev) * q_decay[None, :, None, :, None]
    return (o_intra + o_inter).reshape(b, h, l, dv)


def _retention_branch(rq, rk, rv, gate, norm_g):
    o = _retention(_heads(rq, RET_HEADS), _heads(rk, RET_HEADS), _heads(rv, RET_HEADS))
    return _merge(_head_rmsnorm(o, norm_g)) * jax.nn.silu(gate)


def _gated_delta_rule(q, k, v, beta, g):
    b, h, l, dk = q.shape
    dv = v.shape[-1]
    c = DN_CHUNK
    q = _l2norm(q) * (dk ** -0.5)
    k = _l2norm(k)
    qc, kc, vc = _chunk(q, c), _chunk(k, c), _chunk(v, c)
    bc, gc = _chunk(beta, c), _chunk(g, c)
    gcum = jnp.cumsum(gc, axis=-1)
    tri = jnp.tril(jnp.ones((c, c), dtype=bool))
    strict = jnp.tril(jnp.ones((c, c), dtype=bool), -1)
    decay = jnp.exp(jnp.where(tri, gcum[..., :, None] - gcum[..., None, :], -jnp.inf))
    kb = kc * bc[..., None]
    a_mat = jnp.where(strict, jnp.einsum('bhnid,bhnjd->bhnij', kb, kc) * decay, 0.0)
    rhs = jnp.concatenate([vc * bc[..., None], kb * jnp.exp(gcum)[..., None]], axis=-1)
    sol = lax.linalg.triangular_solve(jnp.eye(c, dtype=F32) + a_mat, rhs, left_side=True, lower=True)
    u, w = sol[..., :dv], sol[..., dv:]
    attn = jnp.einsum('bhnid,bhnjd->bhnij', qc, kc) * decay
    qg = qc * jnp.exp(gcum)[..., None]
    kg = kc * jnp.exp(gcum[..., -1:] - gcum)[..., None]
    glast = jnp.exp(gcum[..., -1])

    def step(state, inp):
        u_n, w_n, attn_n, qg_n, kg_n, gl_n = inp
        v_new = u_n - jnp.einsum('bhcd,bhde->bhce', w_n, state)
        o = jnp.einsum('bhcd,bhde->bhce', qg_n, state) + jnp.einsum('bhij,bhje->bhie', attn_n, v_new)
        state = state * gl_n[..., None, None] + jnp.einsum('bhcd,bhce->bhde', kg_n, v_new)
        return state, o

    xs = tuple(jnp.moveaxis(t, 2, 0) for t in (u, w, attn, qg, kg, glast))
    _, o = lax.scan(step, jnp.zeros((b, h, dk, dv), F32), xs)
    return jnp.moveaxis(o, 0, 2).reshape(b, h, l, dv)


def _deltanet_branch(qkv, beta_in, a_in, gate, conv_w, a_log, dt_bias, norm_g):
    qkv = jax.nn.silu(_causal_conv(qkv, conv_w.astype(F32)))
    q, k, v = jnp.split(qkv, [DN_HEADS * DN_DK, 2 * DN_HEADS * DN_DK], axis=-1)
    q, k, v = _heads(q, DN_HEADS), _heads(k, DN_HEADS), _heads(v, DN_HEADS)
    beta = jax.nn.sigmoid(beta_in).transpose(0, 2, 1)
    g = (-jnp.exp(a_log.astype(F32)) * jax.nn.softplus(a_in + dt_bias.astype(F32))).transpose(0, 2, 1)
    o = _gated_delta_rule(q, k, v, beta, g)
    return _merge(_head_rmsnorm(o, norm_g)) * jax.nn.silu(gate)


def _s5(u, lam_re, lam_im, b_re, b_im, c_re, c_im, d, log_dt):
    bsz, l, _ = u.shape
    ug = u.reshape(bsz, l, S5_GROUPS, S5_GROUP)
    lam_re, lam_im = lam_re.astype(F32), lam_im.astype(F32)
    dt = jnp.exp(log_dt.astype(F32))[:, None]
    mag = jnp.exp(lam_re * dt)
    ang = lam_im * dt
    a_re, a_im = mag * jnp.cos(ang), mag * jnp.sin(ang)
    den = lam_re * lam_re + lam_im * lam_im
    nr, ni = a_re - 1.0, a_im
    coef_re = (nr * lam_re + ni * lam_im) / den
    coef_im = (ni * lam_re - nr * lam_im) / den
    b_re, b_im = b_re.astype(F32), b_im.astype(F32)
    bb_re = coef_re[..., None] * b_re - coef_im[..., None] * b_im
    bb_im = coef_re[..., None] * b_im + coef_im[..., None] * b_re
    x_re = jnp.einsum('gph,blgh->blgp', bb_re, ug)
    x_im = jnp.einsum('gph,blgh->blgp', bb_im, ug)
    a_re_f = jnp.broadcast_to(a_re, x_re.shape)
    a_im_f = jnp.broadcast_to(a_im, x_im.shape)

    def combine(left, right):
        a1r, a1i, b1r, b1i = left
        a2r, a2i, b2r, b2i = right
        return (a2r * a1r - a2i * a1i, a2r * a1i + a2i * a1r,
                a2r * b1r - a2i * b1i + b2r, a2r * b1i + a2i * b1r + b2i)

    _, _, s_re, s_im = lax.associative_scan(combine, (a_re_f, a_im_f, x_re, x_im), axis=1)
    y = (jnp.einsum('ghp,blgp->blgh', c_re.astype(F32), s_re)
         - jnp.einsum('ghp,blgp->blgh', c_im.astype(F32), s_im)
         + d.astype(F32) * ug)
    return y.reshape(bsz, l, -1)


def _s5_branch(u, gate, lam_re, lam_im, b_re, b_im, c_re, c_im, d, log_dt, w_glu, b_glu):
    y = jax.nn.gelu(_s5(u, lam_re, lam_im, b_re, b_im, c_re, c_im, d, log_dt))
    y = y * jax.nn.sigmoid(y @ w_glu.astype(F32) + b_glu.astype(F32))
    return y * jax.nn.silu(gate)


def _gla(q, k, v, gk):
    b, h, l, dk = q.shape
    dv = v.shape[-1]
    c = GLA_CHUNK
    q = q * (dk ** -0.5)
    qc, kc, vc, gc = _chunk(q, c), _chunk(k, c), _chunk(v, c), _chunk(gk, c)
    cum = jnp.cumsum(gc, axis=3)
    causal = jnp.tril(jnp.ones((c, c), dtype=bool))
    rel = cum[:, :, :, :, None, :] - cum[:, :, :, None, :, :]
    rel = jnp.where(causal[:, :, None], rel, -jnp.inf)
    scores = jnp.sum(qc[:, :, :, :, None, :] * kc[:, :, :, None, :, :] * jnp.exp(rel), axis=-1)
    o_intra = jnp.einsum('bhnij,bhnje->bhnie', scores, vc)
    cum_last = cum[:, :, :, -1]
    kv = jnp.einsum('bhncd,bhnce->bhnde', kc * jnp.exp(cum_last[:, :, :, None] - cum), vc)

    def step(state, inp):
        kv_n, dec_n = inp
        return state * dec_n[..., None] + kv_n, state

    _, s_prev = lax.scan(step, jnp.zeros((b, h, dk, dv), F32),
                         (jnp.moveaxis(kv, 2, 0), jnp.moveaxis(jnp.exp(cum_last), 2, 0)))
    s_prev = jnp.moveaxis(s_prev, 0, 2)
    o_inter = jnp.einsum('bhncd,bhnde->bhnce', qc * jnp.exp(cum), s_prev)
    return (o_intra + o_inter).reshape(b, h, l, dv)


def _gla_branch(q, k, v, gate_code, gate, w_gk, b_gk, norm_g):
    gk = jax.nn.log_sigmoid(gate_code @ w_gk.astype(F32) + b_gk.astype(F32)) / GLA_GATE_TAU
    o = _gla(_heads(q, GLA_HEADS), _heads(k, GLA_HEADS), _heads(v, GLA_HEADS), _heads(gk, GLA_HEADS))
    return _merge(_head_rmsnorm(o, norm_g)) * jax.nn.silu(gate)


def setup_inputs(seed: int = 0) -> dict:
    key = jax.random.key(seed)
    ks = jax.random.split(key, 24)

    def nrm(k, shape, scale):
        return jax.random.normal(k, shape, F32) * scale

    x = nrm(ks[0], (BATCH, SEQ, D_MODEL), 1.0)
    norm_pre = 1.0 + nrm(ks[1], (DEPTH, D_MODEL), 0.02)
    norm_post = 1.0 + nrm(ks[2], (DEPTH, D_MODEL), 0.02)
    w_in = nrm(ks[3], (DEPTH, D_MODEL, IN_COLS), D_MODEL ** -0.5)
    w_out = nrm(ks[4], (DEPTH, D_MIX, D_MODEL), D_MIX ** -0.5)
    ret_norm = 1.0 + nrm(ks[5], (DEPTH, HEAD_DIM), 0.02)
    dn_conv = nrm(ks[6], (DEPTH, DN_CONV, DN_QKV), DN_CONV ** -0.5)
    dn_a_log = jnp.log(jax.random.uniform(ks[7], (DEPTH, DN_HEADS), F32, 1.0, 16.0))
    dt = jnp.exp(jax.random.uniform(ks[8], (DEPTH, DN_HEADS), F32, math.log(1e-3), math.log(1e-1)))
    dn_dt_bias = dt + jnp.log(-jnp.expm1(-dt))
    dn_norm = 1.0 + nrm(ks[9], (DEPTH, DN_DV), 0.02)
    n_idx = jnp.arange(S5_STATE, dtype=F32)
    s5_lam_re = -0.5 + nrm(ks[10], (DEPTH, S5_GROUPS, S5_STATE), 0.01)
    s5_lam_im = math.pi * n_idx + nrm(ks[11], (DEPTH, S5_GROUPS, S5_STATE), 0.01)
    s5_b_re = nrm(ks[12], (DEPTH, S5_GROUPS, S5_STATE, S5_GROUP), (2 * S5_GROUP) ** -0.5)
    s5_b_im = nrm(ks[13], (DEPTH, S5_GROUPS, S5_STATE, S5_GROUP), (2 * S5_GROUP) ** -0.5)
    s5_c_re = nrm(ks[14], (DEPTH, S5_GROUPS, S5_GROUP, S5_STATE), S5_STATE ** -0.5)
    s5_c_im = nrm(ks[15], (DEPTH, S5_GROUPS, S5_GROUP, S5_STATE), S5_STATE ** -0.5)
    s5_d = nrm(ks[16], (DEPTH, S5_GROUPS, S5_GROUP), 1.0)
    s5_log_dt = jax.random.uniform(ks[17], (DEPTH, S5_GROUPS), F32, math.log(S5_DT_MIN), math.log(S5_DT_MAX))
    s5_w_glu = nrm(ks[18], (DEPTH, BRANCH_W, BRANCH_W), BRANCH_W ** -0.5)
    s5_b_glu = nrm(ks[19], (DEPTH, BRANCH_W), 0.01)
    gla_w_gk = nrm(ks[20], (DEPTH, GLA_GATE_RANK, GLA_QK), GLA_GATE_RANK ** -0.5)
    gla_b_gk = nrm(ks[21], (DEPTH, GLA_QK), 0.01)
    gla_norm = 1.0 + nrm(ks[22], (DEPTH, GLA_DV), 0.02)
    return {"x": x, "norm_pre": norm_pre, "norm_post": norm_post, "w_in": w_in, "w_out": w_out,
            "ret_norm": ret_norm, "dn_conv": dn_conv, "dn_a_log": dn_a_log, "dn_dt_bias": dn_dt_bias,
            "dn_norm": dn_norm, "s5_lam_re": s5_lam_re, "s5_lam_im": s5_lam_im, "s5_b_re": s5_b_re,
            "s5_b_im": s5_b_im, "s5_c_re": s5_c_re, "s5_c_im": s5_c_im, "s5_d": s5_d,
            "s5_log_dt": s5_log_dt, "s5_w_glu": s5_w_glu, "s5_b_glu": s5_b_glu,
            "gla_w_gk": gla_w_gk, "gla_b_gk": gla_b_gk, "gla_norm": gla_norm}


def reference(x, norm_pre, norm_post, w_in, w_out, ret_norm, dn_conv, dn_a_log, dn_dt_bias, dn_norm,
              s5_lam_re, s5_lam_im, s5_b_re, s5_b_im, s5_c_re, s5_c_im, s5_d, s5_log_dt, s5_w_glu, s5_b_glu,
              gla_w_gk, gla_b_gk, gla_norm):
    for i in range(DEPTH):
        h = _rmsnorm(x, norm_pre[i])
        p = jnp.einsum('bld,dc->blc', h, w_in[i]).astype(F32)
        (rq, rk, rv, rg, dqkv, dbeta, da, dg, su, sg, gq, gkk, gv, gcode, gg) = _split_cols(p, IN_SPLITS)
        y_ret = _retention_branch(rq, rk, rv, rg, ret_norm[i])
        y_dn = _deltanet_branch(dqkv, dbeta, da, dg, dn_conv[i], dn_a_log[i], dn_dt_bias[i], dn_norm[i])
        y_s5 = _s5_branch(su, sg, s5_lam_re[i], s5_lam_im[i], s5_b_re[i], s5_b_im[i], s5_c_re[i], s5_c_im[i],
                          s5_d[i], s5_log_dt[i], s5_w_glu[i], s5_b_glu[i])
        y_gla = _gla_branch(gq, gkk, gv, gcode, gg, gla_w_gk[i], gla_b_gk[i], gla_norm[i])
        y = jnp.concatenate([y_ret, y_dn, y_s5, y_gla], axis=-1).astype(x.dtype)
        o = jnp.einsum('blm,md->bld', y, w_out[i])
        x = x + _rmsnorm(o, norm_post[i])
    return x
```

```python
import functools
import math

import jax
import jax.numpy as jnp
import numpy as np
from jax import lax
from jax.experimental import pallas as pl
from jax.experimental.pallas import tpu as pltpu

F32 = jnp.float32
BF16 = jnp.bfloat16

D_MODEL = 1024
BRANCH_W = 256
N_HEADS = 4
HEAD_DIM = 64
EPS = 1e-6
ROPE_BASE = 10000.0
DN_CONV = 4
S5_GROUP = 16
S5_GROUPS = 16
S5_STATE = 64
GLA_DK = 32
GLA_QK = 128
GLA_GATE_RANK = 16
GLA_GATE_TAU = 16.0
IN_SPLITS = [256, 256, 256, 256, 768, 4, 4, 256, 256, 256, 128, 128, 256, 16, 256]

CHUNK = 64
S5_LAGS = 16
LANE = 128
VMEM_LIMIT_BYTES = 56 * 1024 * 1024

C_RQ, C_RK, C_RV, C_RG = 0, 256, 512, 768
C_DQKV, C_DG = 1024, 1792
C_SU, C_SG = 2048, 2304
C_GQ, C_GK, C_GV, C_GG = 2560, 2688, 2816, 3072
C_SMALL = 3328
P_COLS = 3456
SM_BETA, SM_A, SM_CODE = 0, 4, 8


def _bf(x):
    return x.astype(BF16)


def _dot(a, b):
    return jnp.dot(a, b, preferred_element_type=F32)


def _dot_nt(a, b):
    return lax.dot_general(a, b, (((1,), (1,)), ((), ())), preferred_element_type=F32)


def _dot_tn(a, b):
    return lax.dot_general(a, b, (((0,), (0,)), ((), ())), preferred_element_type=F32)


def _split2(x):
    x1 = _bf(x)
    x2 = _bf(x - x1.astype(F32))
    return x1, x2


def _split3(x):
    x1 = _bf(x)
    r = x - x1.astype(F32)
    x2 = _bf(r)
    x3 = _bf(r - x2.astype(F32))
    return x1, x2, x3


def _dot_x2(x, m):
    x1, x2 = _split2(x)
    return _dot(x1, m) + _dot(x2, m)


def _dot_x3(x, m):
    x1, x2, x3 = _split3(x)
    return _dot(x1, m) + _dot(x2, m) + _dot(x3, m)


def _dot_m3(m, x):
    x1, x2, x3 = _split3(x)
    return _dot(m, x1) + _dot(m, x2) + _dot(m, x3)


def _sigmoid(x):
    return 1.0 / (1.0 + jnp.exp(-x))


def _silu(x):
    return x * _sigmoid(x)


def _softplus(x):
    return jnp.maximum(x, 0.0) + jnp.log(1.0 + jnp.exp(-jnp.abs(x)))


def _stack_heads(x, hm_ref):
    return jnp.concatenate([_bf(x * hm_ref[h:h + 1, :]) for h in range(N_HEADS)], axis=0)


def _head_rmsnorm_gate(o, g_row, gate, hs):
    ms = _dot_x2(o * o, hs) * (1.0 / HEAD_DIM)
    return o * lax.rsqrt(ms + EPS) * g_row * _silu(gate)


def _layer_kernel(
        x_ref, cos_ref, sin_ref,
        gpre_ref, gpost_ref, win_ref, wout_ref,
        hm_ref, hmqk_ref, hmg_ref, hs_ref, bd_ref, bdqk_ref, bdg_ref,
        tril_ref, strict_ref, eye_ref, tri_ref, ones16_ref,
        rdall_ref, rqdec_ref, rkdec_ref, rcd_ref, rnorm_ref,
        dconv_ref, darow_ref, dbias_ref, debeta_ref, deg_ref, dnorm_ref,
        skcat_ref, sbbar_ref, scbd_ref, swre_ref, swim_ref, stre_ref, stim_ref, sa16_ref,
        swglu_ref, sbglu_ref,
        gwhi_ref, gwlo_ref, gb_ref, gnorm_ref,
        out_ref,
        p_ref, y_ref, xc_ref, qkv_ref, us_ref, ucat_ref, zs_ref, sp_ref,
        sret_ref, sdn_ref, sgla_ref, ss5_ref,
        *, tb):
    t_idx = pl.program_id(1)
    n_chunks = tb // CHUNK

    @pl.when(t_idx == 0)
    def _reset():
        sret_ref[...] = jnp.zeros_like(sret_ref)
        sdn_ref[...] = jnp.zeros_like(sdn_ref)
        sgla_ref[...] = jnp.zeros_like(sgla_ref)
        ss5_ref[...] = jnp.zeros_like(ss5_ref)
        xc_ref[0:8, :] = jnp.zeros((8, 768), F32)
        us_ref[0:S5_LAGS, :] = jnp.zeros((S5_LAGS, BRANCH_W), F32)

    x = x_ref[0]
    h = x * lax.rsqrt(jnp.mean(x * x, axis=-1, keepdims=True) + EPS) * gpre_ref[...]
    p_ref[...] = _dot(_bf(h), win_ref[...])

    hs = hs_ref[...]

    def ret_chunk(c, carry):
        r0 = pl.multiple_of(c * CHUNK, CHUNK)
        rows = pl.ds(r0, CHUNK)
        cs = cos_ref[rows, :]
        sn = sin_ref[rows, :]
        qa = p_ref[rows, C_RQ:C_RQ + LANE]
        qb = p_ref[rows, C_RQ + LANE:C_RQ + 2 * LANE]
        ka = p_ref[rows, C_RK:C_RK + LANE]
        kb = p_ref[rows, C_RK + LANE:C_RK + 2 * LANE]
        q = jnp.concatenate([qa * cs - qb * sn, qa * sn + qb * cs], axis=1) * (HEAD_DIM ** -0.5)
        k = jnp.concatenate([ka * cs - kb * sn, ka * sn + kb * cs], axis=1)
        v = p_ref[rows, C_RV:C_RV + BRANCH_W]
        vstack = _stack_heads(v, hm_ref)
        s_all = _dot_nt(_bf(q), _stack_heads(k, hmqk_ref))
        st = sret_ref[...]
        o = _dot(_bf(s_all * rdall_ref[...]), vstack) + _dot_nt(_bf(q * rqdec_ref[...]), _bf(st))
        sret_ref[...] = st * rcd_ref[...] + bdqk_ref[...] * _dot_tn(_bf(v), _bf(k * rkdec_ref[...]))
        gate = p_ref[rows, C_RG:C_RG + BRANCH_W]
        y_ref[rows, 0:BRANCH_W] = _bf(_head_rmsnorm_gate(o, rnorm_ref[...], gate, hs))
        return carry

    lax.fori_loop(0, n_chunks, ret_chunk, 0)

    xc_ref[8:8 + tb, :] = p_ref[:, C_DQKV:C_DQKV + 768]
    conv = dconv_ref[DN_CONV - 1:DN_CONV, :] * xc_ref[8:8 + tb, :]
    for i in range(DN_CONV - 1):
        conv = conv + dconv_ref[i:i + 1, :] * xc_ref[5 + i:5 + i + tb, :]
    qkv_ref[...] = _silu(conv)
    xc_ref[0:8, :] = xc_ref[tb:tb + 8, :]

    def dn_chunk(c, carry):
        r0 = pl.multiple_of(c * CHUNK, CHUNK)
        rows = pl.ds(r0, CHUNK)
        qkv = qkv_ref[rows, :]
        q = qkv[:, 0:256]
        k = qkv[:, 256:512]
        v = qkv[:, 512:768]
        ss = _dot_x2(jnp.concatenate([q * q, k * k], axis=0), hs)
        qn = q * lax.rsqrt(ss[0:CHUNK] + EPS) * (HEAD_DIM ** -0.5)
        kn = k * lax.rsqrt(ss[CHUNK:2 * CHUNK] + EPS)
        small = p_ref[rows, C_SMALL:C_SMALL + LANE]
        beta = _dot_x2(_sigmoid(small), debeta_ref[...])
        g_s = -darow_ref[...] * _softplus(small + dbias_ref[...])
        gcum = _dot_x3(_dot_m3(tri_ref[...], g_s), deg_ref[...])
        grow = _dot_m3(ones16_ref[...], gcum * eye_ref[...])[0:1, :]
        dec = jnp.exp(jnp.minimum(gcum - grow, 0.0))
        egc = jnp.exp(gcum)
        glast = gcum[CHUNK - 1:CHUNK, :]
        kg = kn * jnp.exp(glast - gcum)
        kbeta = kn * beta
        vbeta = v * beta
        aa = _dot_nt(_bf(jnp.concatenate([kbeta, qn], axis=0)), _stack_heads(kn, hm_ref))
        a_all = aa[0:CHUNK] * dec * strict_ref[...]
        attn = aa[CHUNK:2 * CHUNK] * dec * tril_ref[...]
        bd = bd_ref[...]

        def blockdiag(m):
            mb = _bf(m)
            return jnp.concatenate([mb, mb, mb, mb], axis=0) * bd

        pw = -a_all
        t_all = eye_ref[...] + pw
        for _ in range(5):
            pw = _dot(_bf(pw), blockdiag(pw))
            t_all = t_all + _dot(_bf(t_all), blockdiag(pw))
        rhs = jnp.concatenate([_stack_heads(vbeta, hm_ref),
                               _stack_heads(kbeta * egc, hm_ref)], axis=1)
        uw = _dot(_bf(t_all), rhs)
        u = uw[:, 0:256]
        w = uw[:, 256:512]
        st = sdn_ref[...]
        wq = _dot_nt(_bf(jnp.concatenate([w, qn * egc], axis=0)), _bf(st))
        v_new = u - wq[0:CHUNK]
        o = wq[CHUNK:2 * CHUNK] + _dot(_bf(attn), _stack_heads(v_new, hm_ref))
        sdn_ref[...] = st * jnp.exp(glast) + bd_ref[...].astype(F32) * _dot_tn(_bf(v_new), _bf(kg))
        gate = p_ref[rows, C_DG:C_DG + BRANCH_W]
        y_ref[rows, BRANCH_W:2 * BRANCH_W] = _bf(_head_rmsnorm_gate(o, dnorm_ref[...], gate, hs))
        return carry

    lax.fori_loop(0, n_chunks, dn_chunk, 0)

    n_s5 = tb // S5_LAGS
    u = p_ref[:, C_SU:C_SU + BRANCH_W]
    us_ref[S5_LAGS:S5_LAGS + tb, :] = u
    row_in_chunk = lax.broadcasted_iota(jnp.int32, (tb, BRANCH_W), 0) % S5_LAGS
    ucat_ref[:, 0:BRANCH_W] = _bf(u)
    for tau in range(1, S5_LAGS):
        shifted = us_ref[pl.ds(S5_LAGS - tau, tb), :]
        ucat_ref[:, tau * BRANCH_W:(tau + 1) * BRANCH_W] = _bf(
            jnp.where(row_in_chunk >= tau, shifted, 0.0))
    y_s5 = _dot(ucat_ref[...], skcat_ref[...])
    xx = _dot(_bf(u), sbbar_ref[...])
    half = S5_GROUPS * S5_STATE
    x_re = xx[:, 0:half].reshape(n_s5, S5_LAGS, half)
    x_im = xx[:, half:2 * half].reshape(n_s5, S5_LAGS, half)
    w_re = swre_ref[...][None]
    w_im = swim_ref[...][None]
    zs_ref[:, 0:half] = jnp.sum(w_re * x_re - w_im * x_im, axis=1)
    zs_ref[:, half:2 * half] = jnp.sum(w_re * x_im + w_im * x_re, axis=1)
    a_re = sa16_ref[0:1, :]
    a_im = sa16_ref[1:2, :]
    s_re = ss5_ref[0:1, :]
    s_im = ss5_ref[1:2, :]
    for n in range(n_s5):
        sp_ref[n:n + 1, 0:half] = s_re
        sp_ref[n:n + 1, half:2 * half] = s_im
        z_re = zs_ref[n:n + 1, 0:half]
        z_im = zs_ref[n:n + 1, half:2 * half]
        s_re, s_im = a_re * s_re - a_im * s_im + z_re, a_re * s_im + a_im * s_re + z_im
    ss5_ref[0:1, :] = s_re
    ss5_ref[1:2, :] = s_im
    sp_re = sp_ref[:, 0:half][:, None, :]
    sp_im = sp_ref[:, half:2 * half][:, None, :]
    t_re = stre_ref[...][None]
    t_im = stim_ref[...][None]
    v_re = (t_re * sp_re - t_im * sp_im).reshape(tb, half)
    v_im = (t_re * sp_im + t_im * sp_re).reshape(tb, half)
    y_s5 = y_s5 + _dot(_bf(jnp.concatenate([v_re, v_im], axis=1)), scbd_ref[...])
    c0 = math.sqrt(2.0 / math.pi)
    y_s5 = 0.5 * y_s5 * (1.0 + jnp.tanh(c0 * (y_s5 + 0.044715 * (y_s5 * y_s5 * y_s5))))
    y_s5 = y_s5 * _sigmoid(_dot(_bf(y_s5), swglu_ref[...]) + sbglu_ref[...])
    y_ref[:, 2 * BRANCH_W:3 * BRANCH_W] = _bf(y_s5 * _silu(p_ref[:, C_SG:C_SG + BRANCH_W]))
    us_ref[0:S5_LAGS, :] = us_ref[tb:tb + S5_LAGS, :]

    def gla_chunk(c, carry):
        r0 = pl.multiple_of(c * CHUNK, CHUNK)
        rows = pl.ds(r0, CHUNK)
        small = p_ref[rows, C_SMALL:C_SMALL + LANE]
        s1, s2 = _split2(small)
        z = _dot(s1, gwhi_ref[...]) + _dot(s2, gwhi_ref[...]) + _dot(s1, gwlo_ref[...]) + gb_ref[...]
        gk = -_softplus(-z) * (1.0 / GLA_GATE_TAU)
        cum = _dot_m3(tri_ref[...], gk)
        clast = cum[CHUNK - 1:CHUNK, :]
        q = p_ref[rows, C_GQ:C_GQ + GLA_QK]
        k = p_ref[rows, C_GK:C_GK + GLA_QK]
        v = p_ref[rows, C_GV:C_GV + BRANCH_W]
        qt = q * jnp.exp(cum) * (GLA_DK ** -0.5)
        kt = k * jnp.exp(-cum)
        kh = k * jnp.exp(clast - cum)
        s_all = _dot_nt(_bf(qt), _stack_heads(kt, hmg_ref))
        st = sgla_ref[...]
        o = _dot(_bf(s_all * tril_ref[...]), _stack_heads(v, hm_ref)) + _dot_nt(_bf(qt), _bf(st))
        sgla_ref[...] = st * jnp.exp(clast) + bdg_ref[...] * _dot_tn(_bf(v), _bf(kh))
        gate = p_ref[rows, C_GG:C_GG + BRANCH_W]
        y_ref[rows, 3 * BRANCH_W:4 * BRANCH_W] = _bf(_head_rmsnorm_gate(o, gnorm_ref[...], gate, hs))
        return carry

    lax.fori_loop(0, n_chunks, gla_chunk, 0)

    o = _dot(y_ref[...], wout_ref[...])
    out_ref[0] = x + o * lax.rsqrt(jnp.mean(o * o, axis=-1, keepdims=True) + EPS) * gpost_ref[...]


def _in_proj_permutation():
    offs = np.concatenate([[0], np.cumsum(IN_SPLITS)])
    (o_rq, o_rk, o_rv, o_rg, o_dqkv, o_dbeta, o_da, o_dg, o_su, o_sg,
     o_gq, o_gk, o_gv, o_gcode, o_gg) = offs[:-1]
    perm = -np.ones((P_COLS,), np.int64)
    n = np.arange(256)
    rot = np.where(n < 128, 64 * (n // 32) + 2 * (n % 32), 64 * ((n - 128) // 32) + 2 * ((n - 128) % 32) + 1)
    perm[C_RQ:C_RQ + 256] = o_rq + rot
    perm[C_RK:C_RK + 256] = o_rk + rot
    perm[C_RV:C_RV + 256] = o_rv + n
    perm[C_RG:C_RG + 256] = o_rg + n
    perm[C_DQKV:C_DQKV + 768] = o_dqkv + np.arange(768)
    perm[C_DG:C_DG + 256] = o_dg + n
    perm[C_SU:C_SU + 256] = o_su + n
    perm[C_SG:C_SG + 256] = o_sg + n
    perm[C_GQ:C_GQ + 128] = o_gq + np.arange(128)
    perm[C_GK:C_GK + 128] = o_gk + np.arange(128)
    perm[C_GV:C_GV + 256] = o_gv + n
    perm[C_GG:C_GG + 256] = o_gg + n
    perm[C_SMALL + SM_BETA:C_SMALL + SM_BETA + 4] = o_dbeta + np.arange(4)
    perm[C_SMALL + SM_A:C_SMALL + SM_A + 4] = o_da + np.arange(4)
    perm[C_SMALL + SM_CODE:C_SMALL + SM_CODE + 16] = o_gcode + np.arange(16)
    return perm


@functools.lru_cache(maxsize=None)
def _constant_tables(seq_len):
    c = CHUNK
    lane = np.arange(256)
    head_std = lane // 64
    head_qk = (lane % 128) // 32
    i = np.arange(c)[:, None]
    j = np.arange(256)[None, :] % 64
    t = {}
    t["hm"] = (head_std[None, :] == np.arange(4)[:, None]).astype(np.float32)
    t["hmqk"] = (head_qk[None, :] == np.arange(4)[:, None]).astype(np.float32)
    t["hmg"] = ((np.arange(128) // 32)[None, :] == np.arange(4)[:, None]).astype(np.float32)
    t["hs"] = (head_std[:, None] == head_std[None, :]).astype(np.float32)
    t["bd"] = t["hs"]
    t["bdqk"] = (head_std[:, None] == head_qk[None, :]).astype(np.float32)
    t["bdg"] = (head_std[:, None] == (np.arange(128) // 32)[None, :]).astype(np.float32)
    t["tril"] = (i >= j).astype(np.float32)
    t["strict"] = (i > j).astype(np.float32)
    t["eye"] = (i == j).astype(np.float32)
    t["tri"] = (np.arange(c)[:, None] >= np.arange(c)[None, :]).astype(np.float32)
    t["ones16"] = np.ones((16, c), np.float32)
    lg = np.log(1.0 - 2.0 ** (-5.0 - np.arange(4, dtype=np.float64)))
    lg_cols = lg[head_std][None, :]
    t["rdall"] = np.where(i >= j, np.exp(lg_cols * np.where(i >= j, i - j, 0)), 0.0).astype(np.float32)
    lg_qk = lg[head_qk][None, :]
    t["rqdec"] = np.exp(lg_qk * (i + 1.0)).astype(np.float32)
    t["rkdec"] = np.exp(lg_qk * (c - 1.0 - i)).astype(np.float32)
    t["rcd"] = np.exp(lg_qk * c).astype(np.float32)
    inv = ROPE_BASE ** (-np.arange(0, HEAD_DIM, 2, dtype=np.float64) / HEAD_DIM)
    ang = np.arange(seq_len, dtype=np.float64)[:, None] * inv[None, :]
    t["cos"] = np.tile(np.cos(ang), (1, 4)).astype(np.float32)
    t["sin"] = np.tile(np.sin(ang), (1, 4)).astype(np.float32)
    return t


def _s5_tables(lam_re, lam_im, b_re, b_im, c_re, c_im, d, log_dt):
    hp = lax.Precision.HIGHEST
    g, p, hc = S5_GROUPS, S5_STATE, S5_GROUP
    lam_re, lam_im = lam_re.astype(F32), lam_im.astype(F32)
    dt = jnp.exp(log_dt.astype(F32))[:, None]
    mag = jnp.exp(lam_re * dt)
    ang = lam_im * dt
    a_re, a_im = mag * jnp.cos(ang), mag * jnp.sin(ang)
    den = lam_re * lam_re + lam_im * lam_im
    nr, ni = a_re - 1.0, a_im
    coef_re = (nr * lam_re + ni * lam_im) / den
    coef_im = (ni * lam_re - nr * lam_im) / den
    b_re, b_im = b_re.astype(F32), b_im.astype(F32)
    bb_re = coef_re[..., None] * b_re - coef_im[..., None] * b_im
    bb_im = coef_re[..., None] * b_im + coef_im[..., None] * b_re

    def apow(n):
        n = jnp.asarray(n, F32)[:, None, None]
        m = jnp.exp(lam_re * dt * n)
        return m * jnp.cos(ang * n), m * jnp.sin(ang * n)

    lags = np.arange(S5_LAGS)
    p_re, p_im = apow(lags)
    c_re, c_im = c_re.astype(F32), c_im.astype(F32)
    ab_re = p_re[..., None] * bb_re[None] - p_im[..., None] * bb_im[None]
    ab_im = p_re[..., None] * bb_im[None] + p_im[..., None] * bb_re[None]
    kk = (jnp.einsum('tgpi,gop->tgio', ab_re, c_re, precision=hp)
          - jnp.einsum('tgpi,gop->tgio', ab_im, c_im, precision=hp))
    kk = kk.at[0].add(d.astype(F32)[:, :, None] * jnp.eye(hc, dtype=F32)[None])
    eye_g = jnp.eye(g, dtype=F32)
    kcat = jnp.einsum('tgio,gh->tgiho', kk, eye_g).reshape(S5_LAGS * g * hc, g * hc)
    bbar = jnp.concatenate([
        jnp.einsum('gpi,gh->gihp', bb_re, eye_g).reshape(g * hc, g * p),
        jnp.einsum('gpi,gh->gihp', bb_im, eye_g).reshape(g * hc, g * p)], axis=1)
    cbd = jnp.concatenate([
        jnp.einsum('gop,gh->gpho', c_re, eye_g).reshape(g * p, g * hc),
        -jnp.einsum('gop,gh->gpho', c_im, eye_g).reshape(g * p, g * hc)], axis=0)
    w_re, w_im = apow(S5_LAGS - 1 - lags)
    t_re, t_im = apow(lags + 1)
    a16_re, a16_im = apow(np.array([S5_LAGS]))
    flat = lambda z: z.reshape(z.shape[0], g * p)
    return dict(skcat=_bf(kcat), sbbar=_bf(bbar), scbd=_bf(cbd),
                swre=flat(w_re), swim=flat(w_im), stre=flat(t_re), stim=flat(t_im),
                sa16=jnp.concatenate([flat(a16_re), flat(a16_im)], axis=0))


def _const_spec(arr):
    nd = arr.ndim
    return pl.BlockSpec(arr.shape, lambda b, t, _nd=nd: (0,) * _nd)


def _layer_call(x, layer_inputs, const_inputs, cos, sin, tb):
    bsz, seq, _ = x.shape
    grid = (bsz, seq // tb)
    half = S5_GROUPS * S5_STATE
    in_specs = [pl.BlockSpec((1, tb, D_MODEL), lambda b, t: (b, t, 0)),
                pl.BlockSpec((tb, LANE), lambda b, t: (t, 0)),
                pl.BlockSpec((tb, LANE), lambda b, t: (t, 0))]
    operands = [x, cos, sin]
    order = ["gpre", "gpost", "win", "wout",
             "hm", "hmqk", "hmg", "hs", "bd", "bdqk", "bdg", "tril", "strict", "eye", "tri", "ones16",
             "rdall", "rqdec", "rkdec", "rcd", "rnorm",
             "dconv", "darow", "dbias", "debeta", "deg", "dnorm",
             "skcat", "sbbar", "scbd", "swre", "swim", "stre", "stim", "sa16", "swglu", "sbglu",
             "gwhi", "gwlo", "gb", "gnorm"]
    for name in order:
        arr = layer_inputs[name] if name in layer_inputs else const_inputs[name]
        operands.append(arr)
        in_specs.append(_const_spec(arr))
    scratch = [
        pltpu.VMEM((tb, P_COLS), F32),
        pltpu.VMEM((tb, D_MODEL), BF16),
        pltpu.VMEM((tb + 8, 768), F32),
        pltpu.VMEM((tb, 768), F32),
        pltpu.VMEM((tb + S5_LAGS, BRANCH_W), F32),
        pltpu.VMEM((tb, S5_LAGS * BRANCH_W), BF16),
        pltpu.VMEM((tb // S5_LAGS, 2 * half), F32),
        pltpu.VMEM((tb // S5_LAGS, 2 * half), F32),
        pltpu.VMEM((BRANCH_W, BRANCH_W), F32),
        pltpu.VMEM((BRANCH_W, BRANCH_W), F32),
        pltpu.VMEM((BRANCH_W, GLA_QK), F32),
        pltpu.VMEM((2, half), F32),
    ]
    return pl.pallas_call(
        functools.partial(_layer_kernel, tb=tb),
        grid=grid,
        in_specs=in_specs,
        out_specs=pl.BlockSpec((1, tb, D_MODEL), lambda b, t: (b, t, 0)),
        out_shape=jax.ShapeDtypeStruct(x.shape, x.dtype),
        scratch_shapes=scratch,
        compiler_params=pltpu.CompilerParams(
            dimension_semantics=("arbitrary", "arbitrary"),
            vmem_limit_bytes=VMEM_LIMIT_BYTES),
        name="hybrid_layer",
    )(*operands)


def _pick_tb(seq):
    for tb in (256, 128, 64):
        if seq % tb == 0:
            return tb
    raise ValueError(f"sequence length {seq} must be a multiple of {CHUNK}")


def kernel(x, norm_pre, norm_post, w_in, w_out, ret_norm, dn_conv, dn_a_log, dn_dt_bias, dn_norm,
           s5_lam_re, s5_lam_im, s5_b_re, s5_b_im, s5_c_re, s5_c_im, s5_d, s5_log_dt, s5_w_glu, s5_b_glu,
           gla_w_gk, gla_b_gk, gla_norm):
    bsz, seq, dm = x.shape
    depth = w_in.shape[0]
    assert dm == D_MODEL and x.dtype == F32
    tb = _pick_tb(seq)
    tabs = _constant_tables(seq)
    bf_names = ("hs", "bd", "tri", "ones16")
    const_inputs = {k: jnp.asarray(v, BF16 if k in bf_names else F32)
                    for k, v in tabs.items() if k not in ("cos", "sin")}
    cos = jnp.asarray(tabs["cos"])
    sin = jnp.asarray(tabs["sin"])

    perm = _in_proj_permutation()
    valid = jnp.asarray(perm >= 0)
    w_in_r = _bf(jnp.where(valid[None, None, :], jnp.take(w_in, jnp.asarray(np.maximum(perm, 0)), axis=2), 0.0))
    w_out_b = _bf(w_out)

    lane128 = np.arange(LANE)
    e_beta = np.zeros((LANE, 256), np.float32)
    e_g = np.zeros((LANE, 256), np.float32)
    for hh in range(N_HEADS):
        e_beta[SM_BETA + hh, 64 * hh:64 * hh + 64] = 1.0
        e_g[SM_A + hh, 64 * hh:64 * hh + 64] = 1.0
    e_beta = jnp.asarray(e_beta, BF16)
    e_g = jnp.asarray(e_g, BF16)

    for i in range(depth):
        a_row = jnp.zeros((1, LANE), F32).at[0, SM_A:SM_A + 4].set(jnp.exp(dn_a_log[i].astype(F32)))
        b_row = jnp.zeros((1, LANE), F32).at[0, SM_A:SM_A + 4].set(dn_dt_bias[i].astype(F32))
        wgk = jnp.zeros((LANE, GLA_QK), F32).at[SM_CODE:SM_CODE + GLA_GATE_RANK, :].set(gla_w_gk[i].astype(F32))
        wgk_hi = _bf(wgk)
        wgk_lo = _bf(wgk - wgk_hi.astype(F32))
        layer_inputs = dict(
            gpre=norm_pre[i].reshape(1, dm).astype(F32),
            gpost=norm_post[i].reshape(1, dm).astype(F32),
            win=w_in_r[i], wout=w_out_b[i],
            rnorm=jnp.tile(ret_norm[i].astype(F32), N_HEADS).reshape(1, 256),
            dconv=dn_conv[i].astype(F32),
            darow=a_row, dbias=b_row, debeta=e_beta, deg=e_g,
            dnorm=jnp.tile(dn_norm[i].astype(F32), N_HEADS).reshape(1, 256),
            swglu=_bf(s5_w_glu[i]), sbglu=s5_b_glu[i].reshape(1, 256).astype(F32),
            gwhi=wgk_hi, gwlo=wgk_lo, gb=gla_b_gk[i].reshape(1, GLA_QK).astype(F32),
            gnorm=jnp.tile(gla_norm[i].astype(F32), N_HEADS).reshape(1, 256),
        )
        layer_inputs.update(_s5_tables(s5_lam_re[i], s5_lam_im[i], s5_b_re[i], s5_b_im[i],
                                       s5_c_re[i], s5_c_im[i], s5_d[i], s5_log_dt[i]))
        x = _layer_call(x, layer_inputs, const_inputs, cos, sin, tb)
    return x
```

```python
import functools
import math

import jax
import jax.numpy as jnp
import numpy as np
from jax import lax
from jax.experimental import pallas as pl
from jax.experimental.pallas import tpu as pltpu

F32 = jnp.float32
BF16 = jnp.bfloat16

D_MODEL = 1024
BRANCH_W = 256
N_HEADS = 4
HEAD_DIM = 64
EPS = 1e-6
ROPE_BASE = 10000.0
DN_CONV = 4
S5_GROUP = 16
S5_GROUPS = 16
S5_STATE = 64
GLA_DK = 32
GLA_QK = 128
GLA_GATE_RANK = 16
GLA_GATE_TAU = 16.0
IN_SPLITS = [256, 256, 256, 256, 768, 4, 4, 256, 256, 256, 128, 128, 256, 16, 256]

CHUNK = 64
S5_LAGS = 16
LANE = 128
VMEM_LIMIT_BYTES = 56 * 1024 * 1024

G_RET, G_DN, G_S5, G_GLA = (0, 1024), (1024, 1152), (2176, 512), (2688, 768)
P_COLS = 3456
R_Q, R_K, R_V, R_G = 0, 256, 512, 768
D_QKV, D_G, D_SMALL = 0, 768, 1024
S_U, S_G = 0, 256
L_Q, L_K, L_V, L_G = 0, 128, 256, 512
C_RQ, C_RK, C_RV, C_RG = (G_RET[0] + o for o in (R_Q, R_K, R_V, R_G))
C_DQKV, C_DG, C_SMALL = (G_DN[0] + o for o in (D_QKV, D_G, D_SMALL))
C_SU, C_SG = (G_S5[0] + o for o in (S_U, S_G))
C_GQ, C_GK, C_GV, C_GG = (G_GLA[0] + o for o in (L_Q, L_K, L_V, L_G))
SM_BETA, SM_A, SM_CODE = 0, 4, 8


def _bf(x):
    return x.astype(BF16)


def _dot(a, b):
    return jnp.dot(a, b, preferred_element_type=F32)


def _dot_nt(a, b):
    return lax.dot_general(a, b, (((1,), (1,)), ((), ())), preferred_element_type=F32)


def _dot_tn(a, b):
    return lax.dot_general(a, b, (((0,), (0,)), ((), ())), preferred_element_type=F32)


def _split2(x):
    x1 = _bf(x)
    x2 = _bf(x - x1.astype(F32))
    return x1, x2


def _split3(x):
    x1 = _bf(x)
    r = x - x1.astype(F32)
    x2 = _bf(r)
    x3 = _bf(r - x2.astype(F32))
    return x1, x2, x3


def _dot_x2(x, m):
    x1, x2 = _split2(x)
    return _dot(x1, m) + _dot(x2, m)


def _dot_x3(x, m):
    x1, x2, x3 = _split3(x)
    return _dot(x1, m) + _dot(x2, m) + _dot(x3, m)


def _dot_m3(m, x):
    x1, x2, x3 = _split3(x)
    return _dot(m, x1) + _dot(m, x2) + _dot(m, x3)


def _sigmoid(x):
    return 1.0 / (1.0 + jnp.exp(-x))


def _silu(x):
    return x * _sigmoid(x)


def _softplus(x):
    return jnp.maximum(x, 0.0) + jnp.log(1.0 + jnp.exp(-jnp.abs(x)))


def _stack_heads(x, hm_ref):
    return jnp.concatenate([_bf(x * hm_ref[h:h + 1, :]) for h in range(N_HEADS)], axis=0)


def _head_rmsnorm_gate(o, g_row, gate, hs):
    ms = _dot_x2(o * o, hs) * (1.0 / HEAD_DIM)
    return o * lax.rsqrt(ms + EPS) * g_row * _silu(gate)


def _layer_kernel(
        x_ref, cos_ref, sin_ref,
        gpre_ref, gpost_ref, win_ref, wout_ref,
        hm_ref, hmqk_ref, hmg_ref, hs_ref, bd_ref, bdqk_ref, bdg_ref,
        tril_ref, strict_ref, eye_ref, eyer_ref, btri_ref, selc_ref,
        rdall_ref, rqdec_ref, rkdec_ref, rcd_ref, rnorm_ref,
        dconv_ref, darow_ref, dbias_ref, debeta_ref, deg_ref, dnorm_ref,
        skcat_ref, sbbar_ref, scbd_ref, swre_ref, swim_ref, stre_ref, stim_ref, sa16_ref,
        swglu_ref, sbglu_ref,
        gwhi_ref, gwlo_ref, gb_ref, gnorm_ref,
        out_ref,
        h_ref, pr_ref, pd_ref, ps_ref, pg_ref, y_ref, xc_ref, qkv_ref, us_ref, ucat_ref, zs_ref, sp_ref,
        sret_ref, sdn_ref, sgla_ref, ss5_ref,
        *, tb):
    t_idx = pl.program_id(1)
    n_chunks = tb // CHUNK

    @pl.when(t_idx == 0)
    def _reset():
        sret_ref[...] = jnp.zeros_like(sret_ref)
        sdn_ref[...] = jnp.zeros_like(sdn_ref)
        sgla_ref[...] = jnp.zeros_like(sgla_ref)
        ss5_ref[...] = jnp.zeros_like(ss5_ref)
        xc_ref[0:8, :] = jnp.zeros((8, 768), F32)
        us_ref[0:S5_LAGS, :] = jnp.zeros((S5_LAGS, BRANCH_W), F32)

    x = x_ref[0]
    h_ref[...] = _bf(x * lax.rsqrt(jnp.mean(x * x, axis=-1, keepdims=True) + EPS) * gpre_ref[...])

    def project(dst_ref, group):
        dst_ref[...] = _dot(h_ref[...], win_ref[:, group[0]:group[0] + group[1]])

    def out_project(branch):
        rows = slice(branch * BRANCH_W, (branch + 1) * BRANCH_W)
        return _dot(y_ref[:, rows], wout_ref[rows, :])

    hs = hs_ref[...]
    project(pr_ref, G_RET)
    project(pd_ref, G_DN)

    cs = cos_ref[...]
    sn = sin_ref[...]
    qa = pr_ref[:, R_Q:R_Q + LANE]
    qb = pr_ref[:, R_Q + LANE:R_Q + 2 * LANE]
    ka = pr_ref[:, R_K:R_K + LANE]
    kb = pr_ref[:, R_K + LANE:R_K + 2 * LANE]
    q = jnp.concatenate([qa * cs - qb * sn, qa * sn + qb * cs], axis=1) * (HEAD_DIM ** -0.5)
    k = jnp.concatenate([ka * cs - kb * sn, ka * sn + kb * cs], axis=1)
    v = pr_ref[:, R_V:R_V + BRANCH_W]
    s_all = _dot_nt(_bf(q), _stack_heads(k, hmqk_ref))
    st = sret_ref[...]
    o = (_dot(_bf(s_all * rdall_ref[...]), _stack_heads(v, hm_ref))
         + _dot_nt(_bf(q * rqdec_ref[...]), _bf(st)))
    sret_ref[...] = st * rcd_ref[...] + bdqk_ref[...] * _dot_tn(_bf(v), _bf(k * rkdec_ref[...]))
    y_ref[:, 0:BRANCH_W] = _bf(_head_rmsnorm_gate(o, rnorm_ref[...], pr_ref[:, R_G:R_G + BRANCH_W], hs))
    o_acc = out_project(0)
    project(ps_ref, G_S5)

    xc_ref[8:8 + tb, :] = pd_ref[:, D_QKV:D_QKV + 768]
    conv = dconv_ref[DN_CONV - 1:DN_CONV, :] * xc_ref[8:8 + tb, :]
    for i in range(DN_CONV - 1):
        conv = conv + dconv_ref[i:i + 1, :] * xc_ref[5 + i:5 + i + tb, :]
    qkv_ref[...] = _silu(conv)
    xc_ref[0:8, :] = xc_ref[tb:tb + 8, :]

    chunks = [slice(c * CHUNK, (c + 1) * CHUNK) for c in range(n_chunks)]
    bd = bd_ref[...]
    bd32 = bd.astype(F32)

    q = qkv_ref[:, 0:256]
    k = qkv_ref[:, 256:512]
    v = qkv_ref[:, 512:768]
    ss = _dot_x2(jnp.concatenate([q * q, k * k], axis=0), hs)
    qn = q * lax.rsqrt(ss[0:tb] + EPS) * (HEAD_DIM ** -0.5)
    kn = k * lax.rsqrt(ss[tb:2 * tb] + EPS)
    small = pd_ref[:, D_SMALL:D_SMALL + LANE]
    beta = _dot_x2(_sigmoid(small), debeta_ref[...])
    g_s = -darow_ref[...] * _softplus(small + dbias_ref[...])
    gcum = _dot_x3(_dot_m3(btri_ref[...], g_s), deg_ref[...])
    grows = _dot_m3(selc_ref[...], gcum * eyer_ref[...])
    egc = jnp.exp(gcum)
    kbeta = kn * beta
    vbeta = v * beta
    kbe = kbeta * egc
    qg = qn * egc

    dec, glast, kg = [], [], []
    for c, sl in enumerate(chunks):
        gc = gcum[sl]
        dec.append(jnp.exp(jnp.minimum(gc - grows[c:c + 1, :], 0.0)))
        glast.append(gc[CHUNK - 1:CHUNK, :])
        kg.append(kn[sl] * jnp.exp(glast[c] - gc))
    aa = [_dot_nt(_bf(jnp.concatenate([kbeta[sl], qn[sl]], axis=0)), _stack_heads(kn[sl], hm_ref))
          for sl in chunks]
    attn = [aa[c][CHUNK:2 * CHUNK] * dec[c] * tril_ref[...] for c in range(n_chunks)]

    def blockdiag(m):
        mb = _bf(m)
        return jnp.concatenate([mb, mb, mb, mb], axis=0) * bd

    pw = [-(aa[c][0:CHUNK] * dec[c] * strict_ref[...]) for c in range(n_chunks)]
    t_all = [eye_ref[...] + pw[c] for c in range(n_chunks)]
    pw = [_dot(_bf(pw[c]), blockdiag(pw[c])) for c in range(n_chunks)]
    for level in range(1, 6):
        both = [_dot(_bf(jnp.concatenate([pw[c], t_all[c]], axis=0)), blockdiag(pw[c]))
                for c in range(n_chunks)]
        pw = [both[c][0:CHUNK] for c in range(n_chunks)]
        t_all = [t_all[c] + both[c][CHUNK:2 * CHUNK] for c in range(n_chunks)]
    uw = [_dot(_bf(t_all[c]), jnp.concatenate([_stack_heads(vbeta[sl], hm_ref),
                                                _stack_heads(kbe[sl], hm_ref)], axis=1))
          for c, sl in enumerate(chunks)]

    st = sdn_ref[...]
    o_parts = []
    for c, sl in enumerate(chunks):
        wq = _dot_nt(_bf(jnp.concatenate([uw[c][:, 256:512], qg[sl]], axis=0)), _bf(st))
        v_new = uw[c][:, 0:256] - wq[0:CHUNK]
        o_parts.append(wq[CHUNK:2 * CHUNK] + _dot(_bf(attn[c]), _stack_heads(v_new, hm_ref)))
        st = st * jnp.exp(glast[c]) + bd32 * _dot_tn(_bf(v_new), _bf(kg[c]))
    sdn_ref[...] = st
    o = jnp.concatenate(o_parts, axis=0)
    y_ref[:, BRANCH_W:2 * BRANCH_W] = _bf(
        _head_rmsnorm_gate(o, dnorm_ref[...], pd_ref[:, D_G:D_G + BRANCH_W], hs))
    o_acc = o_acc + out_project(1)
    project(pg_ref, G_GLA)

    n_s5 = tb // S5_LAGS
    u = ps_ref[:, S_U:S_U + BRANCH_W]
    us_ref[S5_LAGS:S5_LAGS + tb, :] = u
    row_in_chunk = lax.broadcasted_iota(jnp.int32, (tb, BRANCH_W), 0) % S5_LAGS
    ucat_ref[:, 0:BRANCH_W] = _bf(u)
    for tau in range(1, S5_LAGS):
        shifted = us_ref[pl.ds(S5_LAGS - tau, tb), :]
        ucat_ref[:, tau * BRANCH_W:(tau + 1) * BRANCH_W] = _bf(
            jnp.where(row_in_chunk >= tau, shifted, 0.0))
    y_s5 = _dot(ucat_ref[...], skcat_ref[...])
    xx = _dot(_bf(u), sbbar_ref[...])
    half = S5_GROUPS * S5_STATE
    x_re = xx[:, 0:half].reshape(n_s5, S5_LAGS, half)
    x_im = xx[:, half:2 * half].reshape(n_s5, S5_LAGS, half)
    w_re = swre_ref[...][None]
    w_im = swim_ref[...][None]
    zs_ref[:, 0:half] = jnp.sum(w_re * x_re - w_im * x_im, axis=1)
    zs_ref[:, half:2 * half] = jnp.sum(w_re * x_im + w_im * x_re, axis=1)
    a_re = sa16_ref[0:1, :]
    a_im = sa16_ref[1:2, :]
    s_re = ss5_ref[0:1, :]
    s_im = ss5_ref[1:2, :]
    for n in range(n_s5):
        sp_ref[n:n + 1, 0:half] = s_re
        sp_ref[n:n + 1, half:2 * half] = s_im
        z_re = zs_ref[n:n + 1, 0:half]
        z_im = zs_ref[n:n + 1, half:2 * half]
        s_re, s_im = a_re * s_re - a_im * s_im + z_re, a_re * s_im + a_im * s_re + z_im
    ss5_ref[0:1, :] = s_re
    ss5_ref[1:2, :] = s_im
    sp_re = sp_ref[:, 0:half][:, None, :]
    sp_im = sp_ref[:, half:2 * half][:, None, :]
    t_re = stre_ref[...][None]
    t_im = stim_ref[...][None]
    v_re = (t_re * sp_re - t_im * sp_im).reshape(tb, half)
    v_im = (t_re * sp_im + t_im * sp_re).reshape(tb, half)
    y_s5 = y_s5 + _dot(_bf(jnp.concatenate([v_re, v_im], axis=1)), scbd_ref[...])
    c0 = math.sqrt(2.0 / math.pi)
    y_s5 = 0.5 * y_s5 * (1.0 + jnp.tanh(c0 * (y_s5 + 0.044715 * (y_s5 * y_s5 * y_s5))))
    y_s5 = y_s5 * _sigmoid(_dot(_bf(y_s5), swglu_ref[...]) + sbglu_ref[...])
    y_ref[:, 2 * BRANCH_W:3 * BRANCH_W] = _bf(y_s5 * _silu(ps_ref[:, S_G:S_G + BRANCH_W]))
    us_ref[0:S5_LAGS, :] = us_ref[tb:tb + S5_LAGS, :]
    o_acc = o_acc + out_project(2)

    small = pd_ref[:, D_SMALL:D_SMALL + LANE]
    s1, s2 = _split2(small)
    z = _dot(s1, gwhi_ref[...]) + _dot(s2, gwhi_ref[...]) + _dot(s1, gwlo_ref[...]) + gb_ref[...]
    gk = -_softplus(-z) * (1.0 / GLA_GATE_TAU)
    cum = _dot_m3(btri_ref[...], gk)
    q = pg_ref[:, L_Q:L_Q + GLA_QK]
    k = pg_ref[:, L_K:L_K + GLA_QK]
    v = pg_ref[:, L_V:L_V + BRANCH_W]
    qt = _bf(q * jnp.exp(cum) * (GLA_DK ** -0.5))
    kt = k * jnp.exp(-cum)
    clast = [cum[sl][CHUNK - 1:CHUNK, :] for sl in chunks]
    s_all = [_dot_nt(qt[sl], _stack_heads(kt[sl], hmg_ref)) for sl in chunks]
    o_intra = [_dot(_bf(s_all[c] * tril_ref[...]), _stack_heads(v[sl], hm_ref))
               for c, sl in enumerate(chunks)]
    kv = [bdg_ref[...] * _dot_tn(_bf(v[sl]), _bf(k[sl] * jnp.exp(clast[c] - cum[sl])))
          for c, sl in enumerate(chunks)]
    st = sgla_ref[...]
    o_parts = []
    for c, sl in enumerate(chunks):
        o_parts.append(o_intra[c] + _dot_nt(qt[sl], _bf(st)))
        st = st * jnp.exp(clast[c]) + kv[c]
    sgla_ref[...] = st
    o = jnp.concatenate(o_parts, axis=0)
    y_ref[:, 3 * BRANCH_W:4 * BRANCH_W] = _bf(
        _head_rmsnorm_gate(o, gnorm_ref[...], pg_ref[:, L_G:L_G + BRANCH_W], hs))

    o = o_acc + out_project(3)
    out_ref[0] = x_ref[0] + o * lax.rsqrt(jnp.mean(o * o, axis=-1, keepdims=True) + EPS) * gpost_ref[...]


def _in_proj_permutation():
    offs = np.concatenate([[0], np.cumsum(IN_SPLITS)])
    (o_rq, o_rk, o_rv, o_rg, o_dqkv, o_dbeta, o_da, o_dg, o_su, o_sg,
     o_gq, o_gk, o_gv, o_gcode, o_gg) = offs[:-1]
    perm = -np.ones((P_COLS,), np.int64)
    n = np.arange(256)
    rot = np.where(n < 128, 64 * (n // 32) + 2 * (n % 32), 64 * ((n - 128) // 32) + 2 * ((n - 128) % 32) + 1)
    perm[C_RQ:C_RQ + 256] = o_rq + rot
    perm[C_RK:C_RK + 256] = o_rk + rot
    perm[C_RV:C_RV + 256] = o_rv + n
    perm[C_RG:C_RG + 256] = o_rg + n
    perm[C_DQKV:C_DQKV + 768] = o_dqkv + np.arange(768)
    perm[C_DG:C_DG + 256] = o_dg + n
    perm[C_SU:C_SU + 256] = o_su + n
    perm[C_SG:C_SG + 256] = o_sg + n
    perm[C_GQ:C_GQ + 128] = o_gq + np.arange(128)
    perm[C_GK:C_GK + 128] = o_gk + np.arange(128)
    perm[C_GV:C_GV + 256] = o_gv + n
    perm[C_GG:C_GG + 256] = o_gg + n
    perm[C_SMALL + SM_BETA:C_SMALL + SM_BETA + 4] = o_dbeta + np.arange(4)
    perm[C_SMALL + SM_A:C_SMALL + SM_A + 4] = o_da + np.arange(4)
    perm[C_SMALL + SM_CODE:C_SMALL + SM_CODE + 16] = o_gcode + np.arange(16)
    return perm


@functools.lru_cache(maxsize=None)
def _constant_tables(seq_len, tb):
    c = CHUNK
    lane = np.arange(256)
    head_std = lane // 64
    head_qk = (lane % 128) // 32
    i = np.arange(c)[:, None]
    j = np.arange(256)[None, :] % 64
    t = {}
    t["hm"] = (head_std[None, :] == np.arange(4)[:, None]).astype(np.float32)
    t["hmqk"] = (head_qk[None, :] == np.arange(4)[:, None]).astype(np.float32)
    t["hmg"] = ((np.arange(128) // 32)[None, :] == np.arange(4)[:, None]).astype(np.float32)
    t["hs"] = (head_std[:, None] == head_std[None, :]).astype(np.float32)
    t["bd"] = t["hs"]
    t["bdqk"] = (head_std[:, None] == head_qk[None, :]).astype(np.float32)
    t["bdg"] = (head_std[:, None] == (np.arange(128) // 32)[None, :]).astype(np.float32)
    t["tril"] = (i >= j).astype(np.float32)
    t["strict"] = (i > j).astype(np.float32)
    t["eye"] = (i == j).astype(np.float32)
    r = np.arange(tb)
    same_chunk = (r[:, None] // c) == (r[None, :] // c)
    t["btri"] = (same_chunk & (r[:, None] >= r[None, :])).astype(np.float32)
    t["selc"] = ((r[None, :] // c) == np.arange(16)[:, None]).astype(np.float32)
    t["eyer"] = np.tile(t["eye"], (tb // c, 1))
    lg = np.log(1.0 - 2.0 ** (-5.0 - np.arange(4, dtype=np.float64)))
    ri = r[:, None]
    rj = np.arange(4 * tb)[None, :] % tb
    lg_cols = lg[np.arange(4 * tb) // tb][None, :]
    t["rdall"] = np.where(ri >= rj, np.exp(lg_cols * np.where(ri >= rj, ri - rj, 0)), 0.0).astype(np.float32)
    lg_qk = lg[head_qk][None, :]
    t["rqdec"] = np.exp(lg_qk * (ri + 1.0)).astype(np.float32)
    t["rkdec"] = np.exp(lg_qk * (tb - 1.0 - ri)).astype(np.float32)
    t["rcd"] = np.exp(lg_qk * tb).astype(np.float32)
    inv = ROPE_BASE ** (-np.arange(0, HEAD_DIM, 2, dtype=np.float64) / HEAD_DIM)
    ang = np.arange(seq_len, dtype=np.float64)[:, None] * inv[None, :]
    t["cos"] = np.tile(np.cos(ang), (1, 4)).astype(np.float32)
    t["sin"] = np.tile(np.sin(ang), (1, 4)).astype(np.float32)
    return t


def _s5_tables(lam_re, lam_im, b_re, b_im, c_re, c_im, d, log_dt):
    hp = lax.Precision.HIGHEST
    g, p, hc = S5_GROUPS, S5_STATE, S5_GROUP
    lam_re, lam_im = lam_re.astype(F32), lam_im.astype(F32)
    dt = jnp.exp(log_dt.astype(F32))[:, None]
    mag = jnp.exp(lam_re * dt)
    ang = lam_im * dt
    a_re, a_im = mag * jnp.cos(ang), mag * jnp.sin(ang)
    den = lam_re * lam_re + lam_im * lam_im
    nr, ni = a_re - 1.0, a_im
    coef_re = (nr * lam_re + ni * lam_im) / den
    coef_im = (ni * lam_re - nr * lam_im) / den
    b_re, b_im = b_re.astype(F32), b_im.astype(F32)
    bb_re = coef_re[..., None] * b_re - coef_im[..., None] * b_im
    bb_im = coef_re[..., None] * b_im + coef_im[..., None] * b_re

    def apow(n):
        n = jnp.asarray(n, F32)[:, None, None]
        m = jnp.exp(lam_re * dt * n)
        return m * jnp.cos(ang * n), m * jnp.sin(ang * n)

    lags = np.arange(S5_LAGS)
    p_re, p_im = apow(lags)
    c_re, c_im = c_re.astype(F32), c_im.astype(F32)
    ab_re = p_re[..., None] * bb_re[None] - p_im[..., None] * bb_im[None]
    ab_im = p_re[..., None] * bb_im[None] + p_im[..., None] * bb_re[None]
    kk = (jnp.einsum('tgpi,gop->tgio', ab_re, c_re, precision=hp)
          - jnp.einsum('tgpi,gop->tgio', ab_im, c_im, precision=hp))
    kk = kk.at[0].add(d.astype(F32)[:, :, None] * jnp.eye(hc, dtype=F32)[None])
    eye_g = jnp.eye(g, dtype=F32)
    kcat = jnp.einsum('tgio,gh->tgiho', kk, eye_g).reshape(S5_LAGS * g * hc, g * hc)
    bbar = jnp.concatenate([
        jnp.einsum('gpi,gh->gihp', bb_re, eye_g).reshape(g * hc, g * p),
        jnp.einsum('gpi,gh->gihp', bb_im, eye_g).reshape(g * hc, g * p)], axis=1)
    cbd = jnp.concatenate([
        jnp.einsum('gop,gh->gpho', c_re, eye_g).reshape(g * p, g * hc),
        -jnp.einsum('gop,gh->gpho', c_im, eye_g).reshape(g * p, g * hc)], axis=0)
    w_re, w_im = apow(S5_LAGS - 1 - lags)
    t_re, t_im = apow(lags + 1)
    a16_re, a16_im = apow(np.array([S5_LAGS]))
    flat = lambda z: z.reshape(z.shape[0], g * p)
    return dict(skcat=_bf(kcat), sbbar=_bf(bbar), scbd=_bf(cbd),
                swre=flat(w_re), swim=flat(w_im), stre=flat(t_re), stim=flat(t_im),
                sa16=jnp.concatenate([flat(a16_re), flat(a16_im)], axis=0))


def _const_spec(arr):
    nd = arr.ndim
    return pl.BlockSpec(arr.shape, lambda b, t, _nd=nd: (0,) * _nd)


def _layer_call(x, layer_inputs, const_inputs, cos, sin, tb):
    bsz, seq, _ = x.shape
    grid = (bsz, seq // tb)
    half = S5_GROUPS * S5_STATE
    in_specs = [pl.BlockSpec((1, tb, D_MODEL), lambda b, t: (b, t, 0)),
                pl.BlockSpec((tb, LANE), lambda b, t: (t, 0)),
                pl.BlockSpec((tb, LANE), lambda b, t: (t, 0))]
    operands = [x, cos, sin]
    order = ["gpre", "gpost", "win", "wout",
             "hm", "hmqk", "hmg", "hs", "bd", "bdqk", "bdg", "tril", "strict", "eye", "eyer", "btri", "selc",
             "rdall", "rqdec", "rkdec", "rcd", "rnorm",
             "dconv", "darow", "dbias", "debeta", "deg", "dnorm",
             "skcat", "sbbar", "scbd", "swre", "swim", "stre", "stim", "sa16", "swglu", "sbglu",
             "gwhi", "gwlo", "gb", "gnorm"]
    for name in order:
        arr = layer_inputs[name] if name in layer_inputs else const_inputs[name]
        operands.append(arr)
        in_specs.append(_const_spec(arr))
    scratch = [
        pltpu.VMEM((tb, D_MODEL), BF16),
        pltpu.VMEM((tb, G_RET[1]), F32),
        pltpu.VMEM((tb, G_DN[1]), F32),
        pltpu.VMEM((tb, G_S5[1]), F32),
        pltpu.VMEM((tb, G_GLA[1]), F32),
        pltpu.VMEM((tb, D_MODEL), BF16),
        pltpu.VMEM((tb + 8, 768), F32),
        pltpu.VMEM((tb, 768), F32),
        pltpu.VMEM((tb + S5_LAGS, BRANCH_W), F32),
        pltpu.VMEM((tb, S5_LAGS * BRANCH_W), BF16),
        pltpu.VMEM((tb // S5_LAGS, 2 * half), F32),
        pltpu.VMEM((tb // S5_LAGS, 2 * half), F32),
        pltpu.VMEM((BRANCH_W, BRANCH_W), F32),
        pltpu.VMEM((BRANCH_W, BRANCH_W), F32),
        pltpu.VMEM((BRANCH_W, GLA_QK), F32),
        pltpu.VMEM((2, half), F32),
    ]
    return pl.pallas_call(
        functools.partial(_layer_kernel, tb=tb),
        grid=grid,
        in_specs=in_specs,
        out_specs=pl.BlockSpec((1, tb, D_MODEL), lambda b, t: (b, t, 0)),
        out_shape=jax.ShapeDtypeStruct(x.shape, x.dtype),
        scratch_shapes=scratch,
        compiler_params=pltpu.CompilerParams(
            dimension_semantics=("arbitrary", "arbitrary"),
            vmem_limit_bytes=VMEM_LIMIT_BYTES),
        name="hybrid_layer",
    )(*operands)


def _pick_tb(seq):
    for tb in (256, 128, 64):
        if seq % tb == 0:
            return tb
    raise ValueError(f"sequence length {seq} must be a multiple of {CHUNK}")


def kernel(x, norm_pre, norm_post, w_in, w_out, ret_norm, dn_conv, dn_a_log, dn_dt_bias, dn_norm,
           s5_lam_re, s5_lam_im, s5_b_re, s5_b_im, s5_c_re, s5_c_im, s5_d, s5_log_dt, s5_w_glu, s5_b_glu,
           gla_w_gk, gla_b_gk, gla_norm):
    bsz, seq, dm = x.shape
    depth = w_in.shape[0]
    assert dm == D_MODEL and x.dtype == F32
    tb = _pick_tb(seq)
    tabs = _constant_tables(seq, tb)
    bf_names = ("hs", "bd", "btri", "selc")
    const_inputs = {k: jnp.asarray(v, BF16 if k in bf_names else F32)
                    for k, v in tabs.items() if k not in ("cos", "sin")}
    cos = jnp.asarray(tabs["cos"])
    sin = jnp.asarray(tabs["sin"])

    perm = _in_proj_permutation()
    valid = jnp.asarray(perm >= 0)
    w_in_r = _bf(jnp.where(valid[None, None, :], jnp.take(w_in, jnp.asarray(np.maximum(perm, 0)), axis=2), 0.0))
    w_out_b = _bf(w_out)

    lane128 = np.arange(LANE)
    e_beta = np.zeros((LANE, 256), np.float32)
    e_g = np.zeros((LANE, 256), np.float32)
    for hh in range(N_HEADS):
        e_beta[SM_BETA + hh, 64 * hh:64 * hh + 64] = 1.0
        e_g[SM_A + hh, 64 * hh:64 * hh + 64] = 1.0
    e_beta = jnp.asarray(e_beta, BF16)
    e_g = jnp.asarray(e_g, BF16)

    for i in range(depth):
        a_row = jnp.zeros((1, LANE), F32).at[0, SM_A:SM_A + 4].set(jnp.exp(dn_a_log[i].astype(F32)))
        b_row = jnp.zeros((1, LANE), F32).at[0, SM_A:SM_A + 4].set(dn_dt_bias[i].astype(F32))
        wgk = jnp.zeros((LANE, GLA_QK), F32).at[SM_CODE:SM_CODE + GLA_GATE_RANK, :].set(gla_w_gk[i].astype(F32))
        wgk_hi = _bf(wgk)
        wgk_lo = _bf(wgk - wgk_hi.astype(F32))
        layer_inputs = dict(
            gpre=norm_pre[i].reshape(1, dm).astype(F32),
            gpost=norm_post[i].reshape(1, dm).astype(F32),
            win=w_in_r[i], wout=w_out_b[i],
            rnorm=jnp.tile(ret_norm[i].astype(F32), N_HEADS).reshape(1, 256),
            dconv=dn_conv[i].astype(F32),
            darow=a_row, dbias=b_row, debeta=e_beta, deg=e_g,
            dnorm=jnp.tile(dn_norm[i].astype(F32), N_HEADS).reshape(1, 256),
            swglu=_bf(s5_w_glu[i]), sbglu=s5_b_glu[i].reshape(1, 256).astype(F32),
            gwhi=wgk_hi, gwlo=wgk_lo, gb=gla_b_gk[i].reshape(1, GLA_QK).astype(F32),
            gnorm=jnp.tile(gla_norm[i].astype(F32), N_HEADS).reshape(1, 256),
        )
        layer_inputs.update(_s5_tables(s5_lam_re[i], s5_lam_im[i], s5_b_re[i], s5_b_im[i],
                                       s5_c_re[i], s5_c_im[i], s5_d[i], s5_log_dt[i]))
        x = _layer_call(x, layer_inputs, const_inputs, cos, sin, tb)
    return x
```

```python
import functools
import math

import jax
import jax.numpy as jnp
import numpy as np
from jax import lax
from jax.experimental import pallas as pl
from jax.experimental.pallas import tpu as pltpu

F32 = jnp.float32
BF16 = jnp.bfloat16

D_MODEL = 1024
BRANCH_W = 256
N_HEADS = 4
HEAD_DIM = 64
EPS = 1e-6
ROPE_BASE = 10000.0
DN_CONV = 4
S5_GROUP = 16
S5_GROUPS = 16
S5_STATE = 64
S5_HALF = S5_GROUPS * S5_STATE
GLA_DK = 32
GLA_QK = 128
GLA_GATE_RANK = 16
GLA_GATE_TAU = 16.0
IN_SPLITS = [256, 256, 256, 256, 768, 4, 4, 256, 256, 256, 128, 128, 256, 16, 256]

CHUNK = 64
S5_LAGS = 16
LANE = 128
PROJ_COLS = 256
VMEM_LIMIT_BYTES = 56 * 1024 * 1024

G_RET, G_DN, G_S5, G_GLA = (0, 1024), (1024, 1152), (2176, 512), (2688, 768)
P_COLS = 3456
R_Q, R_K, R_V, R_G = 0, 256, 512, 768
D_SMALL, D_QKV, D_G = 0, 128, 896
S_U, S_G = 0, 256
L_Q, L_K, L_V, L_G = 0, 128, 256, 512
SM_BETA, SM_A, SM_CODE = 0, 4, 8


def _bf(x):
    return x.astype(BF16)


def _dot(a, b):
    return jnp.dot(a, b, preferred_element_type=F32)


def _dot_nt(a, b):
    return lax.dot_general(a, b, (((1,), (1,)), ((), ())), preferred_element_type=F32)


def _dot_tn(a, b):
    return lax.dot_general(a, b, (((0,), (0,)), ((), ())), preferred_element_type=F32)


def _split2(x):
    x1 = _bf(x)
    x2 = _bf(x - x1.astype(F32))
    return x1, x2


def _split3(x):
    x1 = _bf(x)
    r = x - x1.astype(F32)
    x2 = _bf(r)
    x3 = _bf(r - x2.astype(F32))
    return x1, x2, x3


def _dot_x2(x, m):
    x1, x2 = _split2(x)
    return _dot(x1, m) + _dot(x2, m)


def _dot_x3(x, m):
    x1, x2, x3 = _split3(x)
    return _dot(x1, m) + _dot(x2, m) + _dot(x3, m)


def _dot_m3(m, x):
    x1, x2, x3 = _split3(x)
    return _dot(m, x1) + _dot(m, x2) + _dot(m, x3)


def _sigmoid(x):
    return 1.0 / (1.0 + jnp.exp(-x))


def _silu(x):
    return x * _sigmoid(x)


def _softplus(x):
    return jnp.maximum(x, 0.0) + jnp.log(1.0 + jnp.exp(-jnp.abs(x)))


def _stack_heads(x, hm_ref):
    xb = _bf(x)
    rows = x.shape[0]
    return jnp.concatenate([xb * hm_ref[h, 0:rows, :] for h in range(N_HEADS)], axis=0)


def _head_rmsnorm_gate(o, g_row, gate, hs):
    ms = _dot_x2(o * o, hs) * (1.0 / HEAD_DIM)
    return o * lax.rsqrt(ms + EPS) * g_row * _silu(gate)


def _interleave(feeder, consumers):
    live = []
    feeding = True
    while feeding or live:
        if feeding:
            try:
                tag = next(feeder)
                while tag is not None:
                    live.append(consumers.pop(tag))
                    tag = next(feeder)
            except StopIteration:
                feeding = False
        for g in list(live):
            try:
                next(g)
            except StopIteration:
                live.remove(g)
    assert not consumers


def _layer_kernel(
        x_ref, cos_ref, sin_ref,
        gpre_ref, gpost_ref, win_ref, wout_ref,
        hm_ref, hmqk_ref, hmg_ref, hs_ref, bd_ref, bdqk_ref, bdg_ref,
        tril_ref, strict_ref, eye_ref, eyer_ref, btri_ref, selc_ref,
        rdall_ref, rqdec_ref, rkdec_ref, rcd_ref, rnorm_ref,
        dconv_ref, darow_ref, dbias_ref, debeta_ref, deg_ref, dnorm_ref,
        skcat_ref, sbbar_ref, scbd_ref, swre_ref, swim_ref, stre_ref, stim_ref, sa16_ref,
        swglu_ref, sbglu_ref,
        gwhi_ref, gwlo_ref, gb_ref, gnorm_ref,
        out_ref,
        h_ref, pr_ref, pd_ref, ps_ref, pg_ref, y_ref, xc_ref, qkv_ref, us_ref, ucat_ref, zs_ref, sp_ref,
        sret_ref, sdn_ref, sgla_ref, ss5_ref,
        *, tb, nb):
    t_idx = pl.program_id(1)
    n_chunks = tb // CHUNK
    n_s5 = tb // S5_LAGS
    batches = range(nb)
    rr = nb * tb

    def rb(b):
        return slice(b * tb, (b + 1) * tb)

    units = [(b, c, slice(b * tb + c * CHUNK, b * tb + (c + 1) * CHUNK))
             for c in range(n_chunks) for b in batches]
    n_units = len(units)

    @pl.when(t_idx == 0)
    def _reset():
        sret_ref[...] = jnp.zeros_like(sret_ref)
        sdn_ref[...] = jnp.zeros_like(sdn_ref)
        sgla_ref[...] = jnp.zeros_like(sgla_ref)
        ss5_ref[...] = jnp.zeros_like(ss5_ref)
        for b in batches:
            xc_ref[b, 0:8, :] = jnp.zeros((8, 768), F32)
            us_ref[b, 0:S5_LAGS, :] = jnp.zeros((S5_LAGS, BRANCH_W), F32)

    for b in batches:
        x = x_ref[b]
        h_ref[rb(b), :] = _bf(x * lax.rsqrt(jnp.mean(x * x, axis=-1, keepdims=True) + EPS) * gpre_ref[...])

    def projections():
        for name, dst_ref, (start, width) in (("s5", ps_ref, G_S5), ("deltanet", pd_ref, G_DN),
                                               ("retention", pr_ref, G_RET), ("gla", pg_ref, G_GLA)):
            for c0 in range(0, width, PROJ_COLS):
                w = min(PROJ_COLS, width - c0)
                dst_ref[:, c0:c0 + w] = _dot(h_ref[...], win_ref[:, start + c0:start + c0 + w])
                yield None
            yield name

    o_partial = []

    def out_project(branch):
        rows = slice(branch * BRANCH_W, (branch + 1) * BRANCH_W)
        o_partial.append(_dot(y_ref[:, rows], wout_ref[rows, :]))

    hs = hs_ref[...]
    bd = bd_ref[...]
    bd32 = bd.astype(F32)

    def retention():
        cs = cos_ref[...]
        sn = sin_ref[...]
        ret_q, ret_k, ret_v = [], [], []
        for b in batches:
            qa = pr_ref[rb(b), R_Q:R_Q + LANE]
            qb = pr_ref[rb(b), R_Q + LANE:R_Q + 2 * LANE]
            ka = pr_ref[rb(b), R_K:R_K + LANE]
            kb = pr_ref[rb(b), R_K + LANE:R_K + 2 * LANE]
            ret_q.append(jnp.concatenate([qa * cs - qb * sn, qa * sn + qb * cs], axis=1) * (HEAD_DIM ** -0.5))
            ret_k.append(jnp.concatenate([ka * cs - kb * sn, ka * sn + kb * cs], axis=1))
            ret_v.append(pr_ref[rb(b), R_V:R_V + BRANCH_W])
            yield
        s_all = []
        for b in batches:
            s_all.append(_dot_nt(_bf(ret_q[b]), _stack_heads(ret_k[b], hmqk_ref)))
            yield
        o_ret = []
        for b in batches:
            o_ret.append(_dot(_bf(s_all[b] * rdall_ref[...]), _stack_heads(ret_v[b], hm_ref))
                         + _dot_nt(_bf(ret_q[b] * rqdec_ref[...]), _bf(sret_ref[b])))
            yield
        for b in batches:
            sret_ref[b] = (sret_ref[b] * rcd_ref[...]
                           + bdqk_ref[...] * _dot_tn(_bf(ret_v[b]), _bf(ret_k[b] * rkdec_ref[...])))
            yield
        o = jnp.concatenate(o_ret, axis=0)
        y_ref[:, 0:BRANCH_W] = _bf(_head_rmsnorm_gate(o, rnorm_ref[...], pr_ref[:, R_G:R_G + BRANCH_W], hs))
        yield
        out_project(0)

    def deltanet():
        for b in batches:
            xc_ref[b, 8:8 + tb, :] = pd_ref[rb(b), D_QKV:D_QKV + 768]
            conv = dconv_ref[DN_CONV - 1:DN_CONV, :] * xc_ref[b, 8:8 + tb, :]
            for i in range(DN_CONV - 1):
                conv = conv + dconv_ref[i:i + 1, :] * xc_ref[b, 5 + i:5 + i + tb, :]
            qkv_ref[rb(b), :] = _silu(conv)
            xc_ref[b, 0:8, :] = xc_ref[b, tb:tb + 8, :]
            yield

        q = qkv_ref[:, 0:256]
        k = qkv_ref[:, 256:512]
        v = qkv_ref[:, 512:768]
        ss = _dot_x2(jnp.concatenate([q * q, k * k], axis=0), hs)
        qn = q * lax.rsqrt(ss[0:rr] + EPS) * (HEAD_DIM ** -0.5)
        kn = k * lax.rsqrt(ss[rr:2 * rr] + EPS)
        yield
        small = pd_ref[:, D_SMALL:D_SMALL + LANE]
        beta = _dot_x2(_sigmoid(small), debeta_ref[...])
        g_s = -darow_ref[...] * _softplus(small + dbias_ref[...])
        gcum_s = jnp.concatenate([_dot_m3(btri_ref[...], g_s[rb(b)]) for b in batches], axis=0)
        yield
        gcum = _dot_x3(gcum_s, deg_ref[...])
        grows = [_dot_m3(selc_ref[...], gcum[rb(b)] * eyer_ref[...]) for b in batches]
        yield
        egc = jnp.exp(gcum)
        kbeta = kn * beta
        vbeta = v * beta
        kbe = kbeta * egc
        qg = qn * egc
        yield

        dec, glast, kgt = [], [], []
        for b, c, sl in units:
            gc = gcum[sl]
            dec.append(jnp.exp(jnp.minimum(gc - grows[b][c:c + 1, :], 0.0)))
            glast.append(gc[CHUNK - 1:CHUNK, :])
            kgt.append(_bf((kn[sl] * jnp.exp(glast[-1] - gc)).T))
            if b == nb - 1:
                yield
        aa = []
        for b, _, sl in units:
            aa.append(_dot_nt(_bf(jnp.concatenate([kbeta[sl], qn[sl]], axis=0)), _stack_heads(kn[sl], hm_ref)))
            if b == nb - 1:
                yield
        attn = [aa[u][CHUNK:2 * CHUNK] * dec[u] * tril_ref[...] for u in range(n_units)]

        def blockdiag(m):
            mb = _bf(m)
            return jnp.concatenate([mb, mb, mb, mb], axis=0) * bd

        pw = [-(aa[u][0:CHUNK] * dec[u] * strict_ref[...]) for u in range(n_units)]
        t_all = [eye_ref[...] + pw[u] for u in range(n_units)]
        yield
        pw = [_dot(_bf(pw[u]), blockdiag(pw[u])) for u in range(n_units)]
        yield
        for level in range(1, 6):
            both = [_dot(_bf(jnp.concatenate([pw[u], t_all[u]], axis=0)), blockdiag(pw[u]))
                    for u in range(n_units)]
            pw = [both[u][0:CHUNK] for u in range(n_units)]
            t_all = [t_all[u] + both[u][CHUNK:2 * CHUNK] for u in range(n_units)]
            yield
        uw = []
        for u, (b, _, sl) in enumerate(units):
            uw.append(_dot(_bf(t_all[u]), jnp.concatenate([_stack_heads(vbeta[sl], hm_ref),
                                                           _stack_heads(kbe[sl], hm_ref)], axis=1)))
            if b == nb - 1:
                yield

        st = [sdn_ref[b] for b in batches]
        o_parts = {}
        for u, (b, c, sl) in enumerate(units):
            wq = _dot(_bf(jnp.concatenate([uw[u][:, 256:512], qg[sl]], axis=0)), _bf(st[b]))
            v_new = uw[u][:, 0:256] - wq[0:CHUNK]
            o_parts[(b, c)] = wq[CHUNK:2 * CHUNK] + _dot(_bf(attn[u]), _stack_heads(v_new, hm_ref))
            st[b] = st[b] * jnp.exp(glast[u]) + bd32 * _dot(kgt[u], _bf(v_new))
            if b == nb - 1:
                yield
        for b in batches:
            sdn_ref[b] = st[b]
        o = jnp.concatenate([o_parts[(b, c)] for b in batches for c in range(n_chunks)], axis=0)
        y_ref[:, BRANCH_W:2 * BRANCH_W] = _bf(
            _head_rmsnorm_gate(o, dnorm_ref[...], pd_ref[:, D_G:D_G + BRANCH_W], hs))
        yield
        out_project(1)

    def s5():
        u_in = ps_ref[:, S_U:S_U + BRANCH_W]
        row_in_chunk = lax.broadcasted_iota(jnp.int32, (tb, BRANCH_W), 0) % S5_LAGS
        ucat_ref[:, 0:BRANCH_W] = _bf(u_in)
        for b in batches:
            us_ref[b, S5_LAGS:S5_LAGS + tb, :] = u_in[rb(b)]
            for tau in range(1, S5_LAGS):
                shifted = us_ref[b, S5_LAGS - tau:S5_LAGS - tau + tb, :]
                ucat_ref[rb(b), tau * BRANCH_W:(tau + 1) * BRANCH_W] = _bf(
                    jnp.where(row_in_chunk >= tau, shifted, 0.0))
            us_ref[b, 0:S5_LAGS, :] = us_ref[b, tb:tb + S5_LAGS, :]
            yield
        xx = _dot(_bf(u_in), sbbar_ref[...])
        yield
        x_re = xx[:, 0:S5_HALF].reshape(nb * n_s5, S5_LAGS, S5_HALF)
        x_im = xx[:, S5_HALF:2 * S5_HALF].reshape(nb * n_s5, S5_LAGS, S5_HALF)
        w_re = swre_ref[...][None]
        w_im = swim_ref[...][None]
        zs_ref[:, 0:S5_HALF] = jnp.sum(w_re * x_re - w_im * x_im, axis=1)
        yield
        zs_ref[:, S5_HALF:2 * S5_HALF] = jnp.sum(w_re * x_im + w_im * x_re, axis=1)
        yield
        y_s5 = jnp.concatenate([_dot(ucat_ref[rb(b), :], skcat_ref[...]) for b in batches], axis=0)
        yield
        a_re = sa16_ref[0:1, :]
        a_im = sa16_ref[1:2, :]
        s_re = [ss5_ref[b, 0:1, :] for b in batches]
        s_im = [ss5_ref[b, 1:2, :] for b in batches]
        for n in range(n_s5):
            for b in batches:
                r = b * n_s5 + n
                sp_ref[r:r + 1, 0:S5_HALF] = s_re[b]
                sp_ref[r:r + 1, S5_HALF:2 * S5_HALF] = s_im[b]
                z_re = zs_ref[r:r + 1, 0:S5_HALF]
                z_im = zs_ref[r:r + 1, S5_HALF:2 * S5_HALF]
                s_re[b], s_im[b] = (a_re * s_re[b] - a_im * s_im[b] + z_re,
                                    a_re * s_im[b] + a_im * s_re[b] + z_im)
            if n % 2 == 1:
                yield
        for b in batches:
            ss5_ref[b, 0:1, :] = s_re[b]
            ss5_ref[b, 1:2, :] = s_im[b]
        sp_re = sp_ref[:, 0:S5_HALF][:, None, :]
        sp_im = sp_ref[:, S5_HALF:2 * S5_HALF][:, None, :]
        t_re = stre_ref[...][None]
        t_im = stim_ref[...][None]
        v_re = (t_re * sp_re - t_im * sp_im).reshape(rr, S5_HALF)
        yield
        v_im = (t_re * sp_im + t_im * sp_re).reshape(rr, S5_HALF)
        yield
        vv = _bf(jnp.concatenate([v_re, v_im], axis=1))
        y_s5 = y_s5 + jnp.concatenate([_dot(vv[rb(b)], scbd_ref[...]) for b in batches], axis=0)
        yield
        c0 = math.sqrt(2.0 / math.pi)
        y_s5 = 0.5 * y_s5 * (1.0 + jnp.tanh(c0 * (y_s5 + 0.044715 * (y_s5 * y_s5 * y_s5))))
        y_s5 = y_s5 * _sigmoid(_dot(_bf(y_s5), swglu_ref[...]) + sbglu_ref[...])
        y_ref[:, 2 * BRANCH_W:3 * BRANCH_W] = _bf(y_s5 * _silu(ps_ref[:, S_G:S_G + BRANCH_W]))
        yield
        out_project(2)

    def gla():
        small = pd_ref[:, D_SMALL:D_SMALL + LANE]
        s1, s2 = _split2(small)
        z = _dot(s1, gwhi_ref[...]) + _dot(s2, gwhi_ref[...]) + _dot(s1, gwlo_ref[...]) + gb_ref[...]
        gk = -_softplus(-z) * (1.0 / GLA_GATE_TAU)
        yield
        cum = jnp.concatenate([_dot_m3(btri_ref[...], gk[rb(b)]) for b in batches], axis=0)
        yield
        q = pg_ref[:, L_Q:L_Q + GLA_QK]
        k = pg_ref[:, L_K:L_K + GLA_QK]
        v = pg_ref[:, L_V:L_V + BRANCH_W]
        qt = _bf(q * jnp.exp(cum) * (GLA_DK ** -0.5))
        kt = k * jnp.exp(-cum)
        clast = [cum[sl][CHUNK - 1:CHUNK, :] for _, _, sl in units]
        yield
        s_all = []
        for b, _, sl in units:
            s_all.append(_dot_nt(qt[sl], _stack_heads(kt[sl], hmg_ref)))
            if b == nb - 1:
                yield
        o_intra = []
        for u, (b, _, sl) in enumerate(units):
            o_intra.append(_dot(_bf(s_all[u] * tril_ref[...]), _stack_heads(v[sl], hm_ref)))
            if b == nb - 1:
                yield
        kv = []
        for u, (b, _, sl) in enumerate(units):
            kv.append(bdg_ref[...] * _dot_tn(_bf(v[sl]), _bf(k[sl] * jnp.exp(clast[u] - cum[sl]))))
            if b == nb - 1:
                yield
        st = [sgla_ref[b] for b in batches]
        o_parts = {}
        for u, (b, c, sl) in enumerate(units):
            o_parts[(b, c)] = o_intra[u] + _dot_nt(qt[sl], _bf(st[b]))
            st[b] = st[b] * jnp.exp(clast[u]) + kv[u]
            if b == nb - 1:
                yield
        for b in batches:
            sgla_ref[b] = st[b]
        o = jnp.concatenate([o_parts[(b, c)] for b in batches for c in range(n_chunks)], axis=0)
        y_ref[:, 3 * BRANCH_W:4 * BRANCH_W] = _bf(
            _head_rmsnorm_gate(o, gnorm_ref[...], pg_ref[:, L_G:L_G + BRANCH_W], hs))
        yield
        out_project(3)

    _interleave(projections(), {"s5": s5(), "deltanet": deltanet(), "retention": retention(), "gla": gla()})

    o = o_partial[0] + o_partial[1] + o_partial[2] + o_partial[3]
    o = o * lax.rsqrt(jnp.mean(o * o, axis=-1, keepdims=True) + EPS) * gpost_ref[...]
    for b in batches:
        out_ref[b] = x_ref[b] + o[rb(b)]


def _reorder_in_proj(w_in):
    offs = np.concatenate([[0], np.cumsum(IN_SPLITS)])
    (o_rq, o_rk, o_rv, o_rg, o_dqkv, o_dbeta, o_da, o_dg, o_su, o_sg,
     o_gq, o_gk, o_gv, o_gcode, o_gg) = [int(o) for o in offs[:-1]]
    dp, dm, _ = w_in.shape
    w = _bf(w_in)

    def cols(o, n):
        return w[:, :, o:o + n]

    def deinterleave(o):
        t = cols(o, 256).reshape(dp, dm, N_HEADS, HEAD_DIM // 2, 2)
        return jnp.transpose(t, (0, 1, 4, 2, 3)).reshape(dp, dm, 256)

    small = jnp.concatenate([cols(o_dbeta, 4), cols(o_da, 4), cols(o_gcode, 16),
                             jnp.zeros((dp, dm, LANE - 24), BF16)], axis=2)
    pieces = [deinterleave(o_rq), deinterleave(o_rk), cols(o_rv, 256), cols(o_rg, 256),
              small, cols(o_dqkv, 768), cols(o_dg, 256),
              cols(o_su, 256), cols(o_sg, 256),
              cols(o_gq, 128), cols(o_gk, 128), cols(o_gv, 256), cols(o_gg, 256)]
    out = jnp.concatenate(pieces, axis=2)
    assert out.shape[2] == P_COLS
    return out


@functools.lru_cache(maxsize=None)
def _constant_tables(seq_len, tb):
    c = CHUNK
    lane = np.arange(256)
    head_std = lane // 64
    head_qk = (lane % 128) // 32
    head_g = np.arange(128) // 32
    i = np.arange(c)[:, None]
    j = np.arange(256)[None, :] % 64
    t = {}
    heads = np.arange(4)[:, None, None]
    t["hm"] = np.broadcast_to(head_std[None, None, :] == heads, (4, tb, 256)).astype(np.float32)
    t["hmqk"] = np.broadcast_to(head_qk[None, None, :] == heads, (4, tb, 256)).astype(np.float32)
    t["hmg"] = np.broadcast_to(head_g[None, None, :] == heads, (4, c, 128)).astype(np.float32)
    t["hs"] = (head_std[:, None] == head_std[None, :]).astype(np.float32)
    t["bd"] = t["hs"]
    t["bdqk"] = (head_std[:, None] == head_qk[None, :]).astype(np.float32)
    t["bdg"] = (head_std[:, None] == head_g[None, :]).astype(np.float32)
    t["tril"] = (i >= j).astype(np.float32)
    t["strict"] = (i > j).astype(np.float32)
    t["eye"] = (i == j).astype(np.float32)
    r = np.arange(tb)
    same_chunk = (r[:, None] // c) == (r[None, :] // c)
    t["btri"] = (same_chunk & (r[:, None] >= r[None, :])).astype(np.float32)
    t["selc"] = ((r[None, :] // c) == np.arange(16)[:, None]).astype(np.float32)
    t["eyer"] = np.tile(t["eye"], (tb // c, 1))
    lg = np.log(1.0 - 2.0 ** (-5.0 - np.arange(4, dtype=np.float64)))
    ri = r[:, None]
    rj = np.arange(4 * tb)[None, :] % tb
    lg_cols = lg[np.arange(4 * tb) // tb][None, :]
    t["rdall"] = np.where(ri >= rj, np.exp(lg_cols * np.where(ri >= rj, ri - rj, 0)), 0.0).astype(np.float32)
    lg_qk = lg[head_qk][None, :]
    t["rqdec"] = np.exp(lg_qk * (ri + 1.0)).astype(np.float32)
    t["rkdec"] = np.exp(lg_qk * (tb - 1.0 - ri)).astype(np.float32)
    t["rcd"] = np.exp(lg_qk * tb).astype(np.float32)
    inv = ROPE_BASE ** (-np.arange(0, HEAD_DIM, 2, dtype=np.float64) / HEAD_DIM)
    ang = np.arange(seq_len, dtype=np.float64)[:, None] * inv[None, :]
    t["cos"] = np.tile(np.cos(ang), (1, 4)).astype(np.float32)
    t["sin"] = np.tile(np.sin(ang), (1, 4)).astype(np.float32)
    t["tile16"] = np.tile(np.eye(S5_GROUP, dtype=np.float32), (1, S5_GROUPS))
    t["tile64"] = np.tile(np.eye(S5_STATE, dtype=np.float32), (1, S5_GROUPS))
    return t


def _s5_tables(lam_re, lam_im, b_re, b_im, c_re, c_im, d, log_dt, tabs):
    hp = lax.Precision.HIGHEST
    g, p, hc = S5_GROUPS, S5_STATE, S5_GROUP
    nl = lam_re.shape[0]
    lam_re, lam_im = lam_re.astype(F32), lam_im.astype(F32)
    dt = jnp.exp(log_dt.astype(F32))[..., None]
    mag = jnp.exp(lam_re * dt)
    ang = lam_im * dt
    a_re, a_im = mag * jnp.cos(ang), mag * jnp.sin(ang)
    den = lam_re * lam_re + lam_im * lam_im
    nr, ni = a_re - 1.0, a_im
    coef_re = (nr * lam_re + ni * lam_im) / den
    coef_im = (ni * lam_re - nr * lam_im) / den
    b_re, b_im = b_re.astype(F32), b_im.astype(F32)
    bb_re = coef_re[..., None] * b_re - coef_im[..., None] * b_im
    bb_im = coef_re[..., None] * b_im + coef_im[..., None] * b_re

    def apow(n):
        n = jnp.asarray(n, F32)[None, :, None, None]
        m = jnp.exp((lam_re * dt)[:, None] * n)
        return m * jnp.cos(ang[:, None] * n), m * jnp.sin(ang[:, None] * n)

    lags = np.arange(S5_LAGS)
    p_re, p_im = apow(lags)
    c_re, c_im = c_re.astype(F32), c_im.astype(F32)
    ab_re = p_re[..., None] * bb_re[:, None] - p_im[..., None] * bb_im[:, None]
    ab_im = p_re[..., None] * bb_im[:, None] + p_im[..., None] * bb_re[:, None]
    kk = (jnp.einsum('ntgpi,ngop->ntgio', ab_re, c_re, precision=hp)
          - jnp.einsum('ntgpi,ngop->ntgio', ab_im, c_im, precision=hp))
    kk = kk.at[:, 0].add(d.astype(F32)[..., None] * jnp.eye(hc, dtype=F32))
    grp256 = np.arange(g * hc) // hc
    grp1024 = np.arange(g * p) // p
    tile16 = jnp.asarray(tabs["tile16"])
    tile64 = jnp.asarray(tabs["tile64"])
    kcat = jnp.einsum('nro,oc->nrc', kk.reshape(nl, S5_LAGS * g * hc, hc), tile16, precision=hp)
    kcat = jnp.where(jnp.asarray(np.tile(grp256, S5_LAGS)[:, None] == grp256[None, :]), kcat, 0.0)

    def rows_to_state(bb):
        m = jnp.einsum('nrp,pc->nrc', jnp.transpose(bb, (0, 1, 3, 2)).reshape(nl, g * hc, p), tile64,
                       precision=hp)
        return jnp.where(jnp.asarray(grp256[:, None] == grp1024[None, :]), m, 0.0)

    def state_to_rows(cc):
        m = jnp.einsum('nro,oc->nrc', jnp.transpose(cc, (0, 1, 3, 2)).reshape(nl, g * p, hc), tile16,
                       precision=hp)
        return jnp.where(jnp.asarray(grp1024[:, None] == grp256[None, :]), m, 0.0)

    bbar = jnp.concatenate([rows_to_state(bb_re), rows_to_state(bb_im)], axis=2)
    cbd = jnp.concatenate([state_to_rows(c_re), -state_to_rows(c_im)], axis=1)
    w_re, w_im = apow(S5_LAGS - 1 - lags)
    t_re, t_im = apow(lags + 1)
    a16_re, a16_im = apow(np.array([S5_LAGS]))
    flat = lambda z: z.reshape(nl, z.shape[1], g * p)
    return dict(skcat=_bf(kcat), sbbar=_bf(bbar), scbd=_bf(cbd),
                swre=flat(w_re), swim=flat(w_im), stre=flat(t_re), stim=flat(t_im),
                sa16=jnp.concatenate([flat(a16_re), flat(a16_im)], axis=1))


def _const_spec(arr):
    nd = arr.ndim
    return pl.BlockSpec(arr.shape, lambda b, t, _nd=nd: (0,) * _nd)


def _layer_call(x, layer_inputs, const_inputs, cos, sin, tb, nb):
    bsz, seq, _ = x.shape
    grid = (bsz // nb, seq // tb)
    rr = nb * tb
    in_specs = [pl.BlockSpec((nb, tb, D_MODEL), lambda b, t: (b, t, 0)),
                pl.BlockSpec((tb, LANE), lambda b, t: (t, 0)),
                pl.BlockSpec((tb, LANE), lambda b, t: (t, 0))]
    operands = [x, cos, sin]
    order = ["gpre", "gpost", "win", "wout",
             "hm", "hmqk", "hmg", "hs", "bd", "bdqk", "bdg", "tril", "strict", "eye", "eyer", "btri", "selc",
             "rdall", "rqdec", "rkdec", "rcd", "rnorm",
             "dconv", "darow", "dbias", "debeta", "deg", "dnorm",
             "skcat", "sbbar", "scbd", "swre", "swim", "stre", "stim", "sa16", "swglu", "sbglu",
             "gwhi", "gwlo", "gb", "gnorm"]
    for name in order:
        arr = layer_inputs[name] if name in layer_inputs else const_inputs[name]
        operands.append(arr)
        in_specs.append(_const_spec(arr))
    scratch = [
        pltpu.VMEM((rr, D_MODEL), BF16),
        pltpu.VMEM((rr, G_RET[1]), F32),
        pltpu.VMEM((rr, G_DN[1]), F32),
        pltpu.VMEM((rr, G_S5[1]), F32),
        pltpu.VMEM((rr, G_GLA[1]), F32),
        pltpu.VMEM((rr, D_MODEL), BF16),
        pltpu.VMEM((nb, tb + 8, 768), F32),
        pltpu.VMEM((rr, 768), F32),
        pltpu.VMEM((nb, tb + S5_LAGS, BRANCH_W), F32),
        pltpu.VMEM((rr, S5_LAGS * BRANCH_W), BF16),
        pltpu.VMEM((rr // S5_LAGS, 2 * S5_HALF), F32),
        pltpu.VMEM((rr // S5_LAGS, 2 * S5_HALF), F32),
        pltpu.VMEM((nb, BRANCH_W, BRANCH_W), F32),
        pltpu.VMEM((nb, BRANCH_W, BRANCH_W), F32),
        pltpu.VMEM((nb, BRANCH_W, GLA_QK), F32),
        pltpu.VMEM((nb, 2, S5_HALF), F32),
    ]
    return pl.pallas_call(
        functools.partial(_layer_kernel, tb=tb, nb=nb),
        grid=grid,
        in_specs=in_specs,
        out_specs=pl.BlockSpec((nb, tb, D_MODEL), lambda b, t: (b, t, 0)),
        out_shape=jax.ShapeDtypeStruct(x.shape, x.dtype),
        scratch_shapes=scratch,
        compiler_params=pltpu.CompilerParams(
            dimension_semantics=("arbitrary", "arbitrary"),
            vmem_limit_bytes=VMEM_LIMIT_BYTES),
        name="hybrid_layer",
    )(*operands)


def _pick_tiles(bsz, seq):
    nb = 2 if bsz % 2 == 0 else 1
    for tb in (256, 128, 64):
        if seq % tb == 0:
            return tb, nb
    raise ValueError(f"sequence length {seq} must be a multiple of {CHUNK}")


def kernel(x, norm_pre, norm_post, w_in, w_out, ret_norm, dn_conv, dn_a_log, dn_dt_bias, dn_norm,
           s5_lam_re, s5_lam_im, s5_b_re, s5_b_im, s5_c_re, s5_c_im, s5_d, s5_log_dt, s5_w_glu, s5_b_glu,
           gla_w_gk, gla_b_gk, gla_norm):
    bsz, seq, dm = x.shape
    depth = w_in.shape[0]
    assert dm == D_MODEL and x.dtype == F32
    tb, nb = _pick_tiles(bsz, seq)
    tabs = _constant_tables(seq, tb)
    bf_names = ("hm", "hmqk", "hmg", "hs", "bd", "btri", "selc")
    skip = ("cos", "sin", "tile16", "tile64")
    const_inputs = {k: jnp.asarray(v, BF16 if k in bf_names else F32)
                    for k, v in tabs.items() if k not in skip}
    cos = jnp.asarray(tabs["cos"])
    sin = jnp.asarray(tabs["sin"])

    w_in_r = _reorder_in_proj(w_in)
    w_out_b = _bf(w_out)

    e_beta = np.zeros((LANE, 256), np.float32)
    e_g = np.zeros((LANE, 256), np.float32)
    for hh in range(N_HEADS):
        e_beta[SM_BETA + hh, 64 * hh:64 * hh + 64] = 1.0
        e_g[SM_A + hh, 64 * hh:64 * hh + 64] = 1.0
    e_beta = jnp.asarray(e_beta, BF16)
    e_g = jnp.asarray(e_g, BF16)

    a_rows = jnp.zeros((depth, 1, LANE), F32).at[:, 0, SM_A:SM_A + 4].set(jnp.exp(dn_a_log.astype(F32)))
    b_rows = jnp.zeros((depth, 1, LANE), F32).at[:, 0, SM_A:SM_A + 4].set(dn_dt_bias.astype(F32))
    wgk = jnp.zeros((depth, LANE, GLA_QK), F32).at[:, SM_CODE:SM_CODE + GLA_GATE_RANK, :].set(gla_w_gk.astype(F32))
    wgk_hi = _bf(wgk)
    wgk_lo = _bf(wgk - wgk_hi.astype(F32))
    s5 = _s5_tables(s5_lam_re, s5_lam_im, s5_b_re, s5_b_im, s5_c_re, s5_c_im, s5_d, s5_log_dt, tabs)
    tile4 = lambda g: jnp.tile(g.astype(F32), (1, N_HEADS)).reshape(depth, 1, 256)
    rnorm, dnorm, gnorm = tile4(ret_norm), tile4(dn_norm), tile4(gla_norm)
    swglu = _bf(s5_w_glu)

    for i in range(depth):
        layer_inputs = dict(
            gpre=norm_pre[i].reshape(1, dm).astype(F32),
            gpost=norm_post[i].reshape(1, dm).astype(F32),
            win=w_in_r[i], wout=w_out_b[i],
            rnorm=rnorm[i],
            dconv=dn_conv[i].astype(F32),
            darow=a_rows[i], dbias=b_rows[i], debeta=e_beta, deg=e_g,
            dnorm=dnorm[i],
            swglu=swglu[i], sbglu=s5_b_glu[i].reshape(1, 256).astype(F32),
            gwhi=wgk_hi[i], gwlo=wgk_lo[i], gb=gla_b_gk[i].reshape(1, GLA_QK).astype(F32),
            gnorm=gnorm[i],
        )
        layer_inputs.update({k: v[i] for k, v in s5.items()})
        x = _layer_call(x, layer_inputs, const_inputs, cos, sin, tb, nb)
    return x
```

```python
import functools
import math

import jax
import jax.numpy as jnp
import numpy as np
from jax import lax
from jax.experimental import pallas as pl
from jax.experimental.pallas import tpu as pltpu

F32 = jnp.float32
BF16 = jnp.bfloat16

D_MODEL = 1024
BRANCH_W = 256
N_HEADS = 4
HEAD_DIM = 64
EPS = 1e-6
ROPE_BASE = 10000.0
DN_CONV = 4
S5_GROUP = 16
S5_GROUPS = 16
S5_STATE = 64
S5_HALF = S5_GROUPS * S5_STATE
GLA_DK = 32
GLA_QK = 128
GLA_GATE_RANK = 16
GLA_GATE_TAU = 16.0
IN_SPLITS = [256, 256, 256, 256, 768, 4, 4, 256, 256, 256, 128, 128, 256, 16, 256]

CHUNK = 64
S5_LAGS = 8
LANE = 128
PROJ_COLS = 256
VMEM_LIMIT_BYTES = 56 * 1024 * 1024

G_RET, G_DN, G_S5, G_GLA = (0, 1024), (1024, 1152), (2176, 512), (2688, 896)
P_COLS = 3584
R_Q, R_K, R_V, R_G = 0, 256, 512, 768
D_QKV, D_SMALL, D_G = 0, 768, 896
S_U, S_G = 0, 256
L_Q, L_K, L_V, L_CODE, L_G = 0, 128, 256, 512, 640
SM_BETA, SM_A = 0, 4


def _bf(x):
    return x.astype(BF16)


def _dot(a, b):
    return jnp.dot(a, b, preferred_element_type=F32)


def _dot_nt(a, b):
    return lax.dot_general(a, b, (((1,), (1,)), ((), ())), preferred_element_type=F32)


def _dot_tn(a, b):
    return lax.dot_general(a, b, (((0,), (0,)), ((), ())), preferred_element_type=F32)


def _split2(x):
    x1 = _bf(x)
    x2 = _bf(x - x1.astype(F32))
    return x1, x2


def _dot_x2(x, m):
    x1, x2 = _split2(x)
    return _dot(x1, m) + _dot(x2, m)


def _dot_m2(m, x):
    x1, x2 = _split2(x)
    return _dot(m, x1) + _dot(m, x2)


def _sigmoid(x):
    return 1.0 / (1.0 + jnp.exp(-x))


def _silu(x):
    return x * _sigmoid(x)


def _softplus(x):
    return jnp.maximum(x, 0.0) + jnp.log(1.0 + jnp.exp(-jnp.abs(x)))


def _stack_heads(x, hm_ref):
    xb = _bf(x)
    rows = x.shape[0]
    return jnp.concatenate([xb * hm_ref[h, 0:rows, :] for h in range(N_HEADS)], axis=0)


def _head_rmsnorm_gate(o, g_row, gate, hs):
    ms = _dot(_bf(o * o), hs) * (1.0 / HEAD_DIM)
    return o * lax.rsqrt(ms + EPS) * g_row * _silu(gate)


def _interleave(feeder, consumers):
    live = []
    feeding = True
    while feeding or live:
        if feeding:
            try:
                tag = next(feeder)
                while tag is not None:
                    live.append(consumers.pop(tag))
                    tag = next(feeder)
            except StopIteration:
                feeding = False
        for g in list(live):
            try:
                next(g)
            except StopIteration:
                live.remove(g)
    assert not consumers


def _layer_kernel(
        x_ref, cos_ref, sin_ref,
        gpre_ref, gpost_ref, win_ref, wout_ref,
        hm_ref, hmqk_ref, hmg_ref, hs_ref, bd_ref, bdqk_ref, bdg_ref,
        tril_ref, strict_ref, eye_ref, eyer_ref, btri_ref, selc_ref,
        rdall_ref, rqdec_ref, rkdec_ref, rcd_ref, rnorm_ref,
        dconv_ref, darow_ref, dbias_ref, debeta_ref, deg_ref, dnorm_ref,
        skcat_ref, sbbar_ref, scbd_ref, stre_ref, stim_ref, sa_ref,
        swglu_ref, sbglu_ref,
        gwgk_ref, gb_ref, gnorm_ref,
        out_ref,
        h_ref, pr_ref, pd_ref, ps_ref, pg_ref, y_ref, xc_ref, qkv_ref, us_ref, ucat_ref, xx_ref, zs_ref, sp_ref,
        sret_ref, sdn_ref, sgla_ref, ss5_ref,
        *, tb, nb):
    t_idx = pl.program_id(1)
    n_chunks = tb // CHUNK
    n_s5 = tb // S5_LAGS
    batches = range(nb)
    rr = nb * tb

    def rb(b):
        return slice(b * tb, (b + 1) * tb)

    units = [(b, c, slice(b * tb + c * CHUNK, b * tb + (c + 1) * CHUNK))
             for c in range(n_chunks) for b in batches]
    n_units = len(units)

    @pl.when(t_idx == 0)
    def _reset():
        sret_ref[...] = jnp.zeros_like(sret_ref)
        sdn_ref[...] = jnp.zeros_like(sdn_ref)
        sgla_ref[...] = jnp.zeros_like(sgla_ref)
        ss5_ref[...] = jnp.zeros_like(ss5_ref)
        for b in batches:
            xc_ref[b, 0:8, :] = jnp.zeros((8, 768), F32)
            us_ref[b, 0:S5_LAGS, :] = jnp.zeros((S5_LAGS, BRANCH_W), F32)

    for b in batches:
        x = x_ref[b]
        h_ref[rb(b), :] = _bf(x * lax.rsqrt(jnp.mean(x * x, axis=-1, keepdims=True) + EPS) * gpre_ref[...])

    def projections():
        for name, dst_ref, (start, width) in (("deltanet", pd_ref, G_DN), ("s5", ps_ref, G_S5),
                                               ("gla", pg_ref, G_GLA), ("retention", pr_ref, G_RET)):
            for c0 in range(0, width, PROJ_COLS):
                w = min(PROJ_COLS, width - c0)
                dst_ref[:, c0:c0 + w] = _dot(h_ref[...], win_ref[:, start + c0:start + c0 + w])
                yield None
            yield name

    o_partial = []

    def out_project(branch):
        rows = slice(branch * BRANCH_W, (branch + 1) * BRANCH_W)
        o_partial.append(_dot(y_ref[:, rows], wout_ref[rows, :]))

    hs = hs_ref[...]
    bd = bd_ref[...]
    bd32 = bd.astype(F32)

    def retention():
        cs = cos_ref[...]
        sn = sin_ref[...]
        ret_q, ret_k, ret_v = [], [], []
        for b in batches:
            qa = pr_ref[rb(b), R_Q:R_Q + LANE]
            qb = pr_ref[rb(b), R_Q + LANE:R_Q + 2 * LANE]
            ka = pr_ref[rb(b), R_K:R_K + LANE]
            kb = pr_ref[rb(b), R_K + LANE:R_K + 2 * LANE]
            ret_q.append(jnp.concatenate([qa * cs - qb * sn, qa * sn + qb * cs], axis=1) * (HEAD_DIM ** -0.5))
            ret_k.append(jnp.concatenate([ka * cs - kb * sn, ka * sn + kb * cs], axis=1))
            ret_v.append(pr_ref[rb(b), R_V:R_V + BRANCH_W])
            yield
        s_all = []
        for b in batches:
            s_all.append(_dot_nt(_bf(ret_q[b]), _stack_heads(ret_k[b], hmqk_ref)))
            yield
        o_ret = []
        for b in batches:
            o_ret.append(_dot(_bf(s_all[b] * rdall_ref[...]), _stack_heads(ret_v[b], hm_ref))
                         + _dot_nt(_bf(ret_q[b] * rqdec_ref[...]), _bf(sret_ref[b])))
            yield
        for b in batches:
            sret_ref[b] = (sret_ref[b] * rcd_ref[...]
                           + bdqk_ref[...] * _dot_tn(_bf(ret_v[b]), _bf(ret_k[b] * rkdec_ref[...])))
            yield
        o = jnp.concatenate(o_ret, axis=0)
        y_ref[:, 0:BRANCH_W] = _bf(_head_rmsnorm_gate(o, rnorm_ref[...], pr_ref[:, R_G:R_G + BRANCH_W], hs))
        yield
        out_project(0)

    def deltanet():
        for b in batches:
            xc_ref[b, 8:8 + tb, :] = pd_ref[rb(b), D_QKV:D_QKV + 768]
            conv = dconv_ref[DN_CONV - 1:DN_CONV, :] * xc_ref[b, 8:8 + tb, :]
            for i in range(DN_CONV - 1):
                conv = conv + dconv_ref[i:i + 1, :] * xc_ref[b, 5 + i:5 + i + tb, :]
            qkv_ref[rb(b), :] = _silu(conv)
            xc_ref[b, 0:8, :] = xc_ref[b, tb:tb + 8, :]
            yield

        q = qkv_ref[:, 0:256]
        k = qkv_ref[:, 256:512]
        v = qkv_ref[:, 512:768]
        ss = _dot(_bf(jnp.concatenate([q * q, k * k], axis=0)), hs)
        qn = q * lax.rsqrt(ss[0:rr] + EPS) * (HEAD_DIM ** -0.5)
        kn = k * lax.rsqrt(ss[rr:2 * rr] + EPS)
        yield
        small = pd_ref[:, D_SMALL:D_SMALL + LANE]
        beta = _dot(_bf(_sigmoid(small)), debeta_ref[...])
        g_s = -darow_ref[...] * _softplus(small + dbias_ref[...])
        gcum_s = jnp.concatenate([_dot_m2(btri_ref[...], g_s[rb(b)]) for b in batches], axis=0)
        yield
        gcum = _dot_x2(gcum_s, deg_ref[...])
        grows = [_dot_m2(selc_ref[...], gcum[rb(b)] * eyer_ref[...]) for b in batches]
        yield
        egc = jnp.exp(gcum)
        kbeta = kn * beta
        vbeta = v * beta
        kbe = kbeta * egc
        qg = qn * egc
        yield

        dec, glast, kgt = [], [], []
        for b, c, sl in units:
            gc = gcum[sl]
            dec.append(jnp.exp(jnp.minimum(gc - grows[b][c:c + 1, :], 0.0)))
            glast.append(gc[CHUNK - 1:CHUNK, :])
            kgt.append(_bf((kn[sl] * jnp.exp(glast[-1] - gc)).T))
            if b == nb - 1:
                yield
        aa = []
        for b, _, sl in units:
            aa.append(_dot_nt(_bf(jnp.concatenate([kbeta[sl], qn[sl]], axis=0)), _stack_heads(kn[sl], hm_ref)))
            if b == nb - 1:
                yield
        attn = [aa[u][CHUNK:2 * CHUNK] * dec[u] * tril_ref[...] for u in range(n_units)]

        def blockdiag(m):
            mb = _bf(m)
            return jnp.concatenate([mb, mb, mb, mb], axis=0) * bd

        pw = [-(aa[u][0:CHUNK] * dec[u] * strict_ref[...]) for u in range(n_units)]
        t_all = [eye_ref[...] + pw[u] for u in range(n_units)]
        yield
        pw = [_dot(_bf(pw[u]), blockdiag(pw[u])) for u in range(n_units)]
        yield
        for level in range(1, 6):
            both = [_dot(_bf(jnp.concatenate([pw[u], t_all[u]], axis=0)), blockdiag(pw[u]))
                    for u in range(n_units)]
            pw = [both[u][0:CHUNK] for u in range(n_units)]
            t_all = [t_all[u] + both[u][CHUNK:2 * CHUNK] for u in range(n_units)]
            yield
        uw = []
        for u, (b, _, sl) in enumerate(units):
            uw.append(_dot(_bf(t_all[u]), jnp.concatenate([_stack_heads(vbeta[sl], hm_ref),
                                                           _stack_heads(kbe[sl], hm_ref)], axis=1)))
            if b == nb - 1:
                yield

        st = [sdn_ref[b] for b in batches]
        o_parts = {}
        for u, (b, c, sl) in enumerate(units):
            wq = _dot(_bf(jnp.concatenate([uw[u][:, 256:512], qg[sl]], axis=0)), _bf(st[b]))
            v_new = uw[u][:, 0:256] - wq[0:CHUNK]
            o_parts[(b, c)] = wq[CHUNK:2 * CHUNK] + _dot(_bf(attn[u]), _stack_heads(v_new, hm_ref))
            st[b] = st[b] * jnp.exp(glast[u]) + bd32 * _dot(kgt[u], _bf(v_new))
            if b == nb - 1:
                yield
        for b in batches:
            sdn_ref[b] = st[b]
        o = jnp.concatenate([o_parts[(b, c)] for b in batches for c in range(n_chunks)], axis=0)
        y_ref[:, BRANCH_W:2 * BRANCH_W] = _bf(
            _head_rmsnorm_gate(o, dnorm_ref[...], pd_ref[:, D_G:D_G + BRANCH_W], hs))
        yield
        out_project(1)

    def s5():
        u_in = ps_ref[:, S_U:S_U + BRANCH_W]
        row_in_chunk = lax.broadcasted_iota(jnp.int32, (tb, BRANCH_W), 0) % S5_LAGS
        ucat_ref[:, 0:BRANCH_W] = _bf(u_in)
        for b in batches:
            us_ref[b, S5_LAGS:S5_LAGS + tb, :] = u_in[rb(b)]
            for tau in range(1, S5_LAGS):
                shifted = us_ref[b, S5_LAGS - tau:S5_LAGS - tau + tb, :]
                ucat_ref[rb(b), tau * BRANCH_W:(tau + 1) * BRANCH_W] = _bf(
                    jnp.where(row_in_chunk >= tau, shifted, 0.0))
            us_ref[b, 0:S5_LAGS, :] = us_ref[b, tb:tb + S5_LAGS, :]
            yield
        half_tiles = S5_HALF // LANE

        def load_tiles(ref, rows, first):
            return jnp.concatenate([ref[first + t, rows, :] for t in range(half_tiles)], axis=1)

        def store_tiles(ref, rows, first, val):
            for t in range(half_tiles):
                ref[first + t, rows, :] = val[:, t * LANE:(t + 1) * LANE]

        xx = _dot(_bf(u_in), sbbar_ref[...])
        store_tiles(xx_ref, slice(None), 0, xx[:, 0:S5_HALF])
        store_tiles(xx_ref, slice(None), half_tiles, xx[:, S5_HALF:2 * S5_HALF])
        yield
        a1_re = sa_ref[0:1, :]
        a1_im = sa_ref[1:2, :]
        z_re = load_tiles(xx_ref, pl.ds(0, nb * n_s5, stride=S5_LAGS), 0)
        z_im = load_tiles(xx_ref, pl.ds(0, nb * n_s5, stride=S5_LAGS), half_tiles)
        for j in range(1, S5_LAGS):
            xj_re = load_tiles(xx_ref, pl.ds(j, nb * n_s5, stride=S5_LAGS), 0)
            xj_im = load_tiles(xx_ref, pl.ds(j, nb * n_s5, stride=S5_LAGS), half_tiles)
            z_re, z_im = a1_re * z_re - a1_im * z_im + xj_re, a1_re * z_im + a1_im * z_re + xj_im
            if j % 2 == 1:
                yield
        store_tiles(zs_ref, slice(None), 0, z_re)
        store_tiles(zs_ref, slice(None), half_tiles, z_im)
        yield
        y_s5 = jnp.concatenate([_dot(ucat_ref[rb(b), :], skcat_ref[...]) for b in batches], axis=0)
        yield
        al_re = sa_ref[2:3, :]
        al_im = sa_ref[3:4, :]
        s_re = ss5_ref[0]
        s_im = ss5_ref[1]
        for n in range(n_s5):
            rows_n = pl.ds(n, nb, stride=n_s5)
            store_tiles(sp_ref, rows_n, 0, s_re)
            store_tiles(sp_ref, rows_n, half_tiles, s_im)
            zn_re = load_tiles(zs_ref, rows_n, 0)
            zn_im = load_tiles(zs_ref, rows_n, half_tiles)
            s_re, s_im = al_re * s_re - al_im * s_im + zn_re, al_re * s_im + al_im * s_re + zn_im
            if n % 4 == 3:
                yield
        ss5_ref[0] = s_re
        ss5_ref[1] = s_im
        sp_re = load_tiles(sp_ref, slice(None), 0)[:, None, :]
        sp_im = load_tiles(sp_ref, slice(None), half_tiles)[:, None, :]
        t_re = stre_ref[...][None]
        t_im = stim_ref[...][None]
        v_re = (t_re * sp_re - t_im * sp_im).reshape(rr, S5_HALF)
        yield
        v_im = (t_re * sp_im + t_im * sp_re).reshape(rr, S5_HALF)
        yield
        vv = _bf(jnp.concatenate([v_re, v_im], axis=1))
        y_s5 = y_s5 + jnp.concatenate([_dot(vv[rb(b)], scbd_ref[...]) for b in batches], axis=0)
        yield
        c0 = math.sqrt(2.0 / math.pi)
        y_s5 = 0.5 * y_s5 * (1.0 + jnp.tanh(c0 * (y_s5 + 0.044715 * (y_s5 * y_s5 * y_s5))))
        y_s5 = y_s5 * _sigmoid(_dot(_bf(y_s5), swglu_ref[...]) + sbglu_ref[...])
        y_ref[:, 2 * BRANCH_W:3 * BRANCH_W] = _bf(y_s5 * _silu(ps_ref[:, S_G:S_G + BRANCH_W]))
        yield
        out_project(2)

    def gla():
        z = _dot(_bf(pg_ref[:, L_CODE:L_CODE + LANE]), gwgk_ref[...]) + gb_ref[...]
        gk = -_softplus(-z) * (1.0 / GLA_GATE_TAU)
        yield
        cum = jnp.concatenate([_dot_m2(btri_ref[...], gk[rb(b)]) for b in batches], axis=0)
        yield
        q = pg_ref[:, L_Q:L_Q + GLA_QK]
        k = pg_ref[:, L_K:L_K + GLA_QK]
        v = pg_ref[:, L_V:L_V + BRANCH_W]
        qt = _bf(q * jnp.exp(cum) * (GLA_DK ** -0.5))
        kt = k * jnp.exp(-cum)
        clast = [cum[sl][CHUNK - 1:CHUNK, :] for _, _, sl in units]
        yield
        s_all = []
        for b, _, sl in units:
            s_all.append(_dot_nt(qt[sl], _stack_heads(kt[sl], hmg_ref)))
            if b == nb - 1:
                yield
        o_intra = []
        for u, (b, _, sl) in enumerate(units):
            o_intra.append(_dot(_bf(s_all[u] * tril_ref[...]), _stack_heads(v[sl], hm_ref)))
            if b == nb - 1:
                yield
        kv = []
        for u, (b, _, sl) in enumerate(units):
            kv.append(bdg_ref[...] * _dot_tn(_bf(v[sl]), _bf(k[sl] * jnp.exp(clast[u] - cum[sl]))))
            if b == nb - 1:
                yield
        st = [sgla_ref[b] for b in batches]
        o_parts = {}
        for u, (b, c, sl) in enumerate(units):
            o_parts[(b, c)] = o_intra[u] + _dot_nt(qt[sl], _bf(st[b]))
            st[b] = st[b] * jnp.exp(clast[u]) + kv[u]
            if b == nb - 1:
                yield
        for b in batches:
            sgla_ref[b] = st[b]
        o = jnp.concatenate([o_parts[(b, c)] for b in batches for c in range(n_chunks)], axis=0)
        y_ref[:, 3 * BRANCH_W:4 * BRANCH_W] = _bf(
            _head_rmsnorm_gate(o, gnorm_ref[...], pg_ref[:, L_G:L_G + BRANCH_W], hs))
        yield
        out_project(3)

    _interleave(projections(), {"s5": s5(), "deltanet": deltanet(), "retention": retention(), "gla": gla()})

    o = o_partial[0] + o_partial[1] + o_partial[2] + o_partial[3]
    o = o * lax.rsqrt(jnp.mean(o * o, axis=-1, keepdims=True) + EPS) * gpost_ref[...]
    for b in batches:
        out_ref[b] = x_ref[b] + o[rb(b)]


def _reorder_in_proj(w_in):
    offs = np.concatenate([[0], np.cumsum(IN_SPLITS)])
    (o_rq, o_rk, o_rv, o_rg, o_dqkv, o_dbeta, o_da, o_dg, o_su, o_sg,
     o_gq, o_gk, o_gv, o_gcode, o_gg) = [int(o) for o in offs[:-1]]
    dp, dm, _ = w_in.shape
    w = _bf(w_in)

    def cols(o, n):
        return w[:, :, o:o + n]

    def deinterleave(o):
        t = cols(o, 256).reshape(dp, dm, N_HEADS, HEAD_DIM // 2, 2)
        return jnp.transpose(t, (0, 1, 4, 2, 3)).reshape(dp, dm, 256)

    assert o_da == o_dbeta + 4 and o_dg == o_da + 4 and o_gg == o_gcode + GLA_GATE_RANK
    pieces = [deinterleave(o_rq), deinterleave(o_rk),
              w[:, :, o_rv:o_dg], jnp.zeros((dp, dm, LANE - 8), BF16),
              w[:, :, o_dg:o_gg], jnp.zeros((dp, dm, LANE - GLA_GATE_RANK), BF16),
              w[:, :, o_gg:]]
    out = jnp.concatenate(pieces, axis=2)
    assert out.shape[2] == P_COLS
    return out


@functools.lru_cache(maxsize=None)
def _constant_tables(seq_len, tb):
    c = CHUNK
    lane = np.arange(256)
    head_std = lane // 64
    head_qk = (lane % 128) // 32
    head_g = np.arange(128) // 32
    i = np.arange(c)[:, None]
    j = np.arange(256)[None, :] % 64
    t = {}
    heads = np.arange(4)[:, None, None]
    t["hm"] = np.broadcast_to(head_std[None, None, :] == heads, (4, tb, 256)).astype(np.float32)
    t["hmqk"] = np.broadcast_to(head_qk[None, None, :] == heads, (4, tb, 256)).astype(np.float32)
    t["hmg"] = np.broadcast_to(head_g[None, None, :] == heads, (4, c, 128)).astype(np.float32)
    t["hs"] = (head_std[:, None] == head_std[None, :]).astype(np.float32)
    t["bd"] = t["hs"]
    t["bdqk"] = (head_std[:, None] == head_qk[None, :]).astype(np.float32)
    t["bdg"] = (head_std[:, None] == head_g[None, :]).astype(np.float32)
    t["tril"] = (i >= j).astype(np.float32)
    t["strict"] = (i > j).astype(np.float32)
    t["eye"] = (i == j).astype(np.float32)
    r = np.arange(tb)
    same_chunk = (r[:, None] // c) == (r[None, :] // c)
    t["btri"] = (same_chunk & (r[:, None] >= r[None, :])).astype(np.float32)
    t["selc"] = ((r[None, :] // c) == np.arange(16)[:, None]).astype(np.float32)
    t["eyer"] = np.tile(t["eye"], (tb // c, 1))
    lg = np.log(1.0 - 2.0 ** (-5.0 - np.arange(4, dtype=np.float64)))
    ri = r[:, None]
    rj = np.arange(4 * tb)[None, :] % tb
    lg_cols = lg[np.arange(4 * tb) // tb][None, :]
    t["rdall"] = np.where(ri >= rj, np.exp(lg_cols * np.where(ri >= rj, ri - rj, 0)), 0.0).astype(np.float32)
    lg_qk = lg[head_qk][None, :]
    t["rqdec"] = np.exp(lg_qk * (ri + 1.0)).astype(np.float32)
    t["rkdec"] = np.exp(lg_qk * (tb - 1.0 - ri)).astype(np.float32)
    t["rcd"] = np.exp(lg_qk * tb).astype(np.float32)
    inv = ROPE_BASE ** (-np.arange(0, HEAD_DIM, 2, dtype=np.float64) / HEAD_DIM)
    ang = np.arange(seq_len, dtype=np.float64)[:, None] * inv[None, :]
    t["cos"] = np.tile(np.cos(ang), (1, 4)).astype(np.float32)
    t["sin"] = np.tile(np.sin(ang), (1, 4)).astype(np.float32)
    t["tile16"] = np.tile(np.eye(S5_GROUP, dtype=np.float32), (1, S5_GROUPS))
    t["tile64"] = np.tile(np.eye(S5_STATE, dtype=np.float32), (1, S5_GROUPS))
    return t


def _s5_tables(lam_re, lam_im, b_re, b_im, c_re, c_im, d, log_dt, tabs):
    hp = lax.Precision.HIGHEST
    g, p, hc = S5_GROUPS, S5_STATE, S5_GROUP
    nl = lam_re.shape[0]
    lam_re, lam_im = lam_re.astype(F32), lam_im.astype(F32)
    dt = jnp.exp(log_dt.astype(F32))[..., None]
    mag = jnp.exp(lam_re * dt)
    ang = lam_im * dt
    a_re, a_im = mag * jnp.cos(ang), mag * jnp.sin(ang)
    den = lam_re * lam_re + lam_im * lam_im
    nr, ni = a_re - 1.0, a_im
    coef_re = (nr * lam_re + ni * lam_im) / den
    coef_im = (ni * lam_re - nr * lam_im) / den
    b_re, b_im = b_re.astype(F32), b_im.astype(F32)
    bb_re = coef_re[..., None] * b_re - coef_im[..., None] * b_im
    bb_im = coef_re[..., None] * b_im + coef_im[..., None] * b_re

    def apow(n):
        n = jnp.asarray(n, F32)[None, :, None, None]
        m = jnp.exp((lam_re * dt)[:, None] * n)
        return m * jnp.cos(ang[:, None] * n), m * jnp.sin(ang[:, None] * n)

    lags = np.arange(S5_LAGS)
    p_re, p_im = apow(lags)
    c_re, c_im = c_re.astype(F32), c_im.astype(F32)
    ab_re = p_re[..., None] * bb_re[:, None] - p_im[..., None] * bb_im[:, None]
    ab_im = p_re[..., None] * bb_im[:, None] + p_im[..., None] * bb_re[:, None]
    kk = (jnp.einsum('ntgpi,ngop->ntgio', ab_re, c_re, precision=hp)
          - jnp.einsum('ntgpi,ngop->ntgio', ab_im, c_im, precision=hp))
    kk = kk.at[:, 0].add(d.astype(F32)[..., None] * jnp.eye(hc, dtype=F32))
    grp256 = np.arange(g * hc) // hc
    grp1024 = np.arange(g * p) // p
    tile16 = jnp.asarray(tabs["tile16"])
    tile64 = jnp.asarray(tabs["tile64"])
    kcat = jnp.einsum('nro,oc->nrc', kk.reshape(nl, S5_LAGS * g * hc, hc), tile16, precision=hp)
    kcat = jnp.where(jnp.asarray(np.tile(grp256, S5_LAGS)[:, None] == grp256[None, :]), kcat, 0.0)

    def rows_to_state(bb):
        m = jnp.einsum('nrp,pc->nrc', jnp.transpose(bb, (0, 1, 3, 2)).reshape(nl, g * hc, p), tile64,
                       precision=hp)
        return jnp.where(jnp.asarray(grp256[:, None] == grp1024[None, :]), m, 0.0)

    def state_to_rows(cc):
        m = jnp.einsum('nro,oc->nrc', jnp.transpose(cc, (0, 1, 3, 2)).reshape(nl, g * p, hc), tile16,
                       precision=hp)
        return jnp.where(jnp.asarray(grp1024[:, None] == grp256[None, :]), m, 0.0)

    bbar = jnp.concatenate([rows_to_state(bb_re), rows_to_state(bb_im)], axis=2)
    cbd = jnp.concatenate([state_to_rows(c_re), -state_to_rows(c_im)], axis=1)
    t_re, t_im = apow(lags + 1)
    a1_re, a1_im = apow(np.array([1]))
    al_re, al_im = apow(np.array([S5_LAGS]))
    flat = lambda z: z.reshape(nl, z.shape[1], g * p)
    return dict(skcat=_bf(kcat), sbbar=_bf(bbar), scbd=_bf(cbd),
                stre=flat(t_re), stim=flat(t_im),
                sa=jnp.concatenate([flat(a1_re), flat(a1_im), flat(al_re), flat(al_im)], axis=1))


def _const_spec(arr):
    nd = arr.ndim
    return pl.BlockSpec(arr.shape, lambda b, t, _nd=nd: (0,) * _nd)


def _layer_spec(arr, layer):
    nd = arr.ndim - 1
    return pl.BlockSpec((None,) + arr.shape[1:], lambda b, t, _nd=nd, _l=layer: (_l,) + (0,) * _nd)


def _layer_call(x, layer, stacked, const_inputs, cos, sin, tb, nb):
    bsz, seq, _ = x.shape
    grid = (bsz // nb, seq // tb)
    rr = nb * tb
    in_specs = [pl.BlockSpec((nb, tb, D_MODEL), lambda b, t: (b, t, 0)),
                pl.BlockSpec((tb, LANE), lambda b, t: (t, 0)),
                pl.BlockSpec((tb, LANE), lambda b, t: (t, 0))]
    operands = [x, cos, sin]
    order = ["gpre", "gpost", "win", "wout",
             "hm", "hmqk", "hmg", "hs", "bd", "bdqk", "bdg", "tril", "strict", "eye", "eyer", "btri", "selc",
             "rdall", "rqdec", "rkdec", "rcd", "rnorm",
             "dconv", "darow", "dbias", "debeta", "deg", "dnorm",
             "skcat", "sbbar", "scbd", "stre", "stim", "sa", "swglu", "sbglu",
             "gwgk", "gb", "gnorm"]
    for name in order:
        if name in stacked:
            operands.append(stacked[name])
            in_specs.append(_layer_spec(stacked[name], layer))
        else:
            operands.append(const_inputs[name])
            in_specs.append(_const_spec(const_inputs[name]))
    scratch = [
        pltpu.VMEM((rr, D_MODEL), BF16),
        pltpu.VMEM((rr, G_RET[1]), F32),
        pltpu.VMEM((rr, G_DN[1]), F32),
        pltpu.VMEM((rr, G_S5[1]), F32),
        pltpu.VMEM((rr, G_GLA[1]), F32),
        pltpu.VMEM((rr, D_MODEL), BF16),
        pltpu.VMEM((nb, tb + 8, 768), F32),
        pltpu.VMEM((rr, 768), F32),
        pltpu.VMEM((nb, tb + S5_LAGS, BRANCH_W), F32),
        pltpu.VMEM((rr, S5_LAGS * BRANCH_W), BF16),
        pltpu.VMEM((2 * S5_HALF // LANE, rr, LANE), F32),
        pltpu.VMEM((2 * S5_HALF // LANE, rr // S5_LAGS, LANE), F32),
        pltpu.VMEM((2 * S5_HALF // LANE, rr // S5_LAGS, LANE), F32),
        pltpu.VMEM((nb, BRANCH_W, BRANCH_W), F32),
        pltpu.VMEM((nb, BRANCH_W, BRANCH_W), F32),
        pltpu.VMEM((nb, BRANCH_W, GLA_QK), F32),
        pltpu.VMEM((2, nb, S5_HALF), F32),
    ]
    return pl.pallas_call(
        functools.partial(_layer_kernel, tb=tb, nb=nb),
        grid=grid,
        in_specs=in_specs,
        out_specs=pl.BlockSpec((nb, tb, D_MODEL), lambda b, t: (b, t, 0)),
        out_shape=jax.ShapeDtypeStruct(x.shape, x.dtype),
        scratch_shapes=scratch,
        compiler_params=pltpu.CompilerParams(
            dimension_semantics=("arbitrary", "arbitrary"),
            vmem_limit_bytes=VMEM_LIMIT_BYTES),
        name="hybrid_layer",
    )(*operands)


def _pick_tiles(bsz, seq):
    nb = 2 if bsz % 2 == 0 else 1
    for tb in (256, 128, 64):
        if seq % tb == 0:
            return tb, nb
    raise ValueError(f"sequence length {seq} must be a multiple of {CHUNK}")


def kernel(x, norm_pre, norm_post, w_in, w_out, ret_norm, dn_conv, dn_a_log, dn_dt_bias, dn_norm,
           s5_lam_re, s5_lam_im, s5_b_re, s5_b_im, s5_c_re, s5_c_im, s5_d, s5_log_dt, s5_w_glu, s5_b_glu,
           gla_w_gk, gla_b_gk, gla_norm):
    bsz, seq, dm = x.shape
    depth = w_in.shape[0]
    assert dm == D_MODEL and x.dtype == F32
    tb, nb = _pick_tiles(bsz, seq)
    tabs = _constant_tables(seq, tb)
    bf_names = ("hm", "hmqk", "hmg", "hs", "bd", "btri", "selc")
    skip = ("cos", "sin", "tile16", "tile64")
    const_inputs = {k: jnp.asarray(v, BF16 if k in bf_names else F32)
                    for k, v in tabs.items() if k not in skip}
    cos = jnp.asarray(tabs["cos"])
    sin = jnp.asarray(tabs["sin"])

    w_in_r = _reorder_in_proj(w_in)
    w_out_b = _bf(w_out)

    e_beta = np.zeros((LANE, 256), np.float32)
    e_g = np.zeros((LANE, 256), np.float32)
    for hh in range(N_HEADS):
        e_beta[SM_BETA + hh, 64 * hh:64 * hh + 64] = 1.0
        e_g[SM_A + hh, 64 * hh:64 * hh + 64] = 1.0
    e_beta = jnp.asarray(e_beta, BF16)
    e_g = jnp.asarray(e_g, BF16)

    a_rows = jnp.zeros((depth, 1, LANE), F32).at[:, 0, SM_A:SM_A + 4].set(jnp.exp(dn_a_log.astype(F32)))
    b_rows = jnp.zeros((depth, 1, LANE), F32).at[:, 0, SM_A:SM_A + 4].set(dn_dt_bias.astype(F32))
    wgk = jnp.zeros((depth, LANE, GLA_QK), BF16).at[:, 0:GLA_GATE_RANK, :].set(_bf(gla_w_gk))
    s5 = _s5_tables(s5_lam_re, s5_lam_im, s5_b_re, s5_b_im, s5_c_re, s5_c_im, s5_d, s5_log_dt, tabs)
    tile4 = lambda g: jnp.tile(g.astype(F32), (1, N_HEADS)).reshape(depth, 1, 256)
    rnorm, dnorm, gnorm = tile4(ret_norm), tile4(dn_norm), tile4(gla_norm)
    swglu = _bf(s5_w_glu)

    stacked = dict(
        gpre=norm_pre.reshape(depth, 1, dm).astype(F32),
        gpost=norm_post.reshape(depth, 1, dm).astype(F32),
        win=w_in_r, wout=w_out_b,
        rnorm=rnorm,
        dconv=dn_conv.astype(F32),
        darow=a_rows, dbias=b_rows,
        dnorm=dnorm,
        swglu=swglu, sbglu=s5_b_glu.reshape(depth, 1, 256).astype(F32),
        gwgk=wgk, gb=gla_b_gk.reshape(depth, 1, GLA_QK).astype(F32),
        gnorm=gnorm,
    )
    stacked.update(s5)
    const_inputs.update(debeta=e_beta, deg=e_g)
    for i in range(depth):
        x = _layer_call(x, i, stacked, const_inputs, cos, sin, tb, nb)
    return x
```

```python
import functools
import math

import jax
import jax.numpy as jnp
import numpy as np
from jax import lax
from jax.experimental import pallas as pl
from jax.experimental.pallas import tpu as pltpu

F32 = jnp.float32
BF16 = jnp.bfloat16

D_MODEL = 1024
BRANCH_W = 256
N_HEADS = 4
HEAD_DIM = 64
EPS = 1e-6
ROPE_BASE = 10000.0
DN_CONV = 4
S5_GROUP = 16
S5_GROUPS = 16
S5_STATE = 64
S5_HALF = S5_GROUPS * S5_STATE
GLA_DK = 32
GLA_QK = 128
GLA_GATE_RANK = 16
GLA_GATE_TAU = 16.0
IN_SPLITS = [256, 256, 256, 256, 768, 4, 4, 256, 256, 256, 128, 128, 256, 16, 256]

CHUNK = 64
S5_LAGS = 8
LANE = 128
PROJ_COLS = 256
VMEM_LIMIT_BYTES = 56 * 1024 * 1024

G_RET, G_DN, G_S5, G_GLA = (0, 1024), (1024, 1152), (2176, 512), (2688, 896)
P_COLS = 3584
R_Q, R_K, R_V, R_G = 0, 256, 512, 768
D_QKV, D_SMALL, D_G = 0, 768, 896
S_U, S_G = 0, 256
L_Q, L_K, L_V, L_CODE, L_G = 0, 128, 256, 512, 640
SM_BETA, SM_A = 0, 4


def _bf(x):
    return x.astype(BF16)


def _dot(a, b):
    return jnp.dot(a, b, preferred_element_type=F32)


def _dot_nt(a, b):
    return lax.dot_general(a, b, (((1,), (1,)), ((), ())), preferred_element_type=F32)


def _dot_tn(a, b):
    return lax.dot_general(a, b, (((0,), (0,)), ((), ())), preferred_element_type=F32)


def _split2(x):
    x1 = _bf(x)
    x2 = _bf(x - x1.astype(F32))
    return x1, x2


def _dot_x2(x, m):
    x1, x2 = _split2(x)
    return _dot(x1, m) + _dot(x2, m)


def _dot_m2(m, x):
    x1, x2 = _split2(x)
    return _dot(m, x1) + _dot(m, x2)


def _sigmoid(x):
    return 1.0 / (1.0 + jnp.exp(-x))


def _silu(x):
    return x * _sigmoid(x)


def _softplus(x):
    return jnp.maximum(x, 0.0) + jnp.log(1.0 + jnp.exp(-jnp.abs(x)))


def _stack_heads(x, hm_ref):
    xb = _bf(x)
    rows = x.shape[0]
    return jnp.concatenate([xb * hm_ref[h, 0:rows, :] for h in range(N_HEADS)], axis=0)


def _head_rmsnorm_gate(o, g_row, gate, hs):
    ms = _dot(_bf(o * o), hs) * (1.0 / HEAD_DIM)
    return o * lax.rsqrt(ms + EPS) * g_row * _silu(gate)


def _interleave(feeder, consumers):
    live = []
    feeding = True
    while feeding or live:
        if feeding:
            try:
                tag = next(feeder)
                while tag is not None:
                    live.append(consumers.pop(tag))
                    tag = next(feeder)
            except StopIteration:
                feeding = False
        for g in list(live):
            try:
                next(g)
            except StopIteration:
                live.remove(g)
    assert not consumers


def _layer_kernel(
        x_ref, cos_ref, sin_ref,
        gpre_ref, gpost_ref, win_ref, wout_ref,
        hm_ref, hmqk_ref, hmg_ref, hs_ref, bd_ref, bdqk_ref, bdg_ref,
        tril_ref, strict_ref, eye_ref, eyer_ref, btri_ref, selc_ref, sperm_ref,
        rdall_ref, rqdec_ref, rkdec_ref, rcd_ref, rnorm_ref,
        dconv_ref, darow_ref, dbias_ref, debeta_ref, deg_ref, dnorm_ref,
        skcat_ref, sbbar_ref, scbd_ref, stre_ref, stim_ref, sa_ref,
        swglu_ref, sbglu_ref,
        gwgk_ref, gb_ref, gnorm_ref,
        out_ref,
        h_ref, pr_ref, pd_ref, ps_ref, pg_ref, y_ref, xc_ref, qkv_ref, us_ref, ucat_ref, zs_ref, sp_ref,
        sret_ref, sdn_ref, sgla_ref, ss5_ref,
        *, tb, nb):
    t_idx = pl.program_id(1)
    n_chunks = tb // CHUNK
    n_s5 = tb // S5_LAGS
    batches = range(nb)
    rr = nb * tb

    def rb(b):
        return slice(b * tb, (b + 1) * tb)

    units = [(b, c, slice(b * tb + c * CHUNK, b * tb + (c + 1) * CHUNK))
             for c in range(n_chunks) for b in batches]
    n_units = len(units)

    @pl.when(t_idx == 0)
    def _reset():
        sret_ref[...] = jnp.zeros_like(sret_ref)
        sdn_ref[...] = jnp.zeros_like(sdn_ref)
        sgla_ref[...] = jnp.zeros_like(sgla_ref)
        ss5_ref[...] = jnp.zeros_like(ss5_ref)
        for b in batches:
            xc_ref[b, 0:8, :] = jnp.zeros((8, 768), F32)
            us_ref[b, 0:S5_LAGS, :] = jnp.zeros((S5_LAGS, BRANCH_W), F32)

    for b in batches:
        x = x_ref[b]
        h_ref[rb(b), :] = _bf(x * lax.rsqrt(jnp.mean(x * x, axis=-1, keepdims=True) + EPS) * gpre_ref[...])

    def projections():
        for name, dst_ref, (start, width) in (("deltanet", pd_ref, G_DN), ("s5", ps_ref, G_S5),
                                               ("gla", pg_ref, G_GLA), ("retention", pr_ref, G_RET)):
            for c0 in range(0, width, PROJ_COLS):
                w = min(PROJ_COLS, width - c0)
                dst_ref[:, c0:c0 + w] = _dot(h_ref[...], win_ref[:, start + c0:start + c0 + w])
                yield None
            yield name

    o_partial = []

    def out_project(branch):
        rows = slice(branch * BRANCH_W, (branch + 1) * BRANCH_W)
        o_partial.append(_dot(y_ref[:, rows], wout_ref[rows, :]))

    hs = hs_ref[...]
    bd = bd_ref[...]
    bd32 = bd.astype(F32)

    def retention():
        cs = cos_ref[...]
        sn = sin_ref[...]
        ret_q, ret_k, ret_v = [], [], []
        for b in batches:
            qa = pr_ref[rb(b), R_Q:R_Q + LANE]
            qb = pr_ref[rb(b), R_Q + LANE:R_Q + 2 * LANE]
            ka = pr_ref[rb(b), R_K:R_K + LANE]
            kb = pr_ref[rb(b), R_K + LANE:R_K + 2 * LANE]
            ret_q.append(jnp.concatenate([qa * cs - qb * sn, qa * sn + qb * cs], axis=1) * (HEAD_DIM ** -0.5))
            ret_k.append(jnp.concatenate([ka * cs - kb * sn, ka * sn + kb * cs], axis=1))
            ret_v.append(pr_ref[rb(b), R_V:R_V + BRANCH_W])
            yield
        s_all = []
        for b in batches:
            s_all.append(_dot_nt(_bf(ret_q[b]), _stack_heads(ret_k[b], hmqk_ref)))
            yield
        o_ret = []
        for b in batches:
            o_ret.append(_dot(_bf(s_all[b] * rdall_ref[...]), _stack_heads(ret_v[b], hm_ref))
                         + _dot_nt(_bf(ret_q[b] * rqdec_ref[...]), _bf(sret_ref[b])))
            yield
        for b in batches:
            sret_ref[b] = (sret_ref[b] * rcd_ref[...]
                           + bdqk_ref[...] * _dot_tn(_bf(ret_v[b]), _bf(ret_k[b] * rkdec_ref[...])))
            yield
        o = jnp.concatenate(o_ret, axis=0)
        y_ref[:, 0:BRANCH_W] = _bf(_head_rmsnorm_gate(o, rnorm_ref[...], pr_ref[:, R_G:R_G + BRANCH_W], hs))
        yield
        out_project(0)

    def deltanet():
        for b in batches:
            xc_ref[b, 8:8 + tb, :] = pd_ref[rb(b), D_QKV:D_QKV + 768]
            conv = dconv_ref[DN_CONV - 1:DN_CONV, :] * xc_ref[b, 8:8 + tb, :]
            for i in range(DN_CONV - 1):
                conv = conv + dconv_ref[i:i + 1, :] * xc_ref[b, 5 + i:5 + i + tb, :]
            qkv_ref[rb(b), :] = _silu(conv)
            xc_ref[b, 0:8, :] = xc_ref[b, tb:tb + 8, :]
            yield

        q = qkv_ref[:, 0:256]
        k = qkv_ref[:, 256:512]
        v = qkv_ref[:, 512:768]
        ss = _dot(_bf(jnp.concatenate([q * q, k * k], axis=0)), hs)
        qn = q * lax.rsqrt(ss[0:rr] + EPS) * (HEAD_DIM ** -0.5)
        kn = k * lax.rsqrt(ss[rr:2 * rr] + EPS)
        yield
        small = pd_ref[:, D_SMALL:D_SMALL + LANE]
        beta = _dot(_bf(_sigmoid(small)), debeta_ref[...])
        g_s = -darow_ref[...] * _softplus(small + dbias_ref[...])
        gcum_s = jnp.concatenate([_dot_m2(btri_ref[...], g_s[rb(b)]) for b in batches], axis=0)
        yield
        gcum = _dot_x2(gcum_s, deg_ref[...])
        grows = [_dot_m2(selc_ref[...], gcum[rb(b)] * eyer_ref[...]) for b in batches]
        yield
        egc = jnp.exp(gcum)
        kbeta = kn * beta
        vbeta = v * beta
        kbe = kbeta * egc
        qg = qn * egc
        yield

        dec, glast, kgt = [], [], []
        for b, c, sl in units:
            gc = gcum[sl]
            dec.append(jnp.exp(jnp.minimum(gc - grows[b][c:c + 1, :], 0.0)))
            glast.append(gc[CHUNK - 1:CHUNK, :])
            kgt.append(_bf((kn[sl] * jnp.exp(glast[-1] - gc)).T))
            if b == nb - 1:
                yield
        aa = []
        for b, _, sl in units:
            aa.append(_dot_nt(_bf(jnp.concatenate([kbeta[sl], qn[sl]], axis=0)), _stack_heads(kn[sl], hm_ref)))
            if b == nb - 1:
                yield
        attn = [aa[u][CHUNK:2 * CHUNK] * dec[u] * tril_ref[...] for u in range(n_units)]

        def blockdiag(m):
            mb = _bf(m)
            return jnp.concatenate([mb, mb, mb, mb], axis=0) * bd

        pw = [-(aa[u][0:CHUNK] * dec[u] * strict_ref[...]) for u in range(n_units)]
        t_all = [eye_ref[...] + pw[u] for u in range(n_units)]
        yield
        pw = [_dot(_bf(pw[u]), blockdiag(pw[u])) for u in range(n_units)]
        yield
        for level in range(1, 5):
            both = [_dot(_bf(jnp.concatenate([pw[u], t_all[u]], axis=0)), blockdiag(pw[u]))
                    for u in range(n_units)]
            pw = [both[u][0:CHUNK] for u in range(n_units)]
            t_all = [t_all[u] + both[u][CHUNK:2 * CHUNK] for u in range(n_units)]
            yield
        t_all = [t_all[u] + _dot(_bf(t_all[u]), blockdiag(pw[u])) for u in range(n_units)]
        yield
        uw = []
        for u, (b, _, sl) in enumerate(units):
            uw.append(_dot(_bf(t_all[u]), jnp.concatenate([_stack_heads(vbeta[sl], hm_ref),
                                                           _stack_heads(kbe[sl], hm_ref)], axis=1)))
            if b == nb - 1:
                yield

        st = [sdn_ref[b] for b in batches]
        o_parts = {}
        for u, (b, c, sl) in enumerate(units):
            wq = _dot(_bf(jnp.concatenate([uw[u][:, 256:512], qg[sl]], axis=0)), _bf(st[b]))
            v_new = uw[u][:, 0:256] - wq[0:CHUNK]
            o_parts[(b, c)] = wq[CHUNK:2 * CHUNK] + _dot(_bf(attn[u]), _stack_heads(v_new, hm_ref))
            st[b] = st[b] * jnp.exp(glast[u]) + bd32 * _dot(kgt[u], _bf(v_new))
            if b == nb - 1:
                yield
        for b in batches:
            sdn_ref[b] = st[b]
        o = jnp.concatenate([o_parts[(b, c)] for b in batches for c in range(n_chunks)], axis=0)
        y_ref[:, BRANCH_W:2 * BRANCH_W] = _bf(
            _head_rmsnorm_gate(o, dnorm_ref[...], pd_ref[:, D_G:D_G + BRANCH_W], hs))
        yield
        out_project(1)

    def s5():
        u_in = ps_ref[:, S_U:S_U + BRANCH_W]
        row_in_chunk = lax.broadcasted_iota(jnp.int32, (tb, BRANCH_W), 0) % S5_LAGS
        ucat_ref[:, 0:BRANCH_W] = _bf(u_in)
        for b in batches:
            us_ref[b, S5_LAGS:S5_LAGS + tb, :] = u_in[rb(b)]
            for tau in range(1, S5_LAGS):
                shifted = us_ref[b, S5_LAGS - tau:S5_LAGS - tau + tb, :]
                ucat_ref[rb(b), tau * BRANCH_W:(tau + 1) * BRANCH_W] = _bf(
                    jnp.where(row_in_chunk >= tau, shifted, 0.0))
            us_ref[b, 0:S5_LAGS, :] = us_ref[b, tb:tb + S5_LAGS, :]
            yield
        u_perm = _bf(_dot(sperm_ref[...], _bf(u_in)))
        xx = _dot(u_perm, sbbar_ref[...])
        yield
        a1_re = sa_ref[0:1, :]
        a1_im = sa_ref[1:2, :]
        n_cb = nb * n_s5
        z_re = xx[0:n_cb, 0:S5_HALF]
        z_im = xx[0:n_cb, S5_HALF:2 * S5_HALF]
        for j in range(1, S5_LAGS):
            xj_re = xx[j * n_cb:(j + 1) * n_cb, 0:S5_HALF]
            xj_im = xx[j * n_cb:(j + 1) * n_cb, S5_HALF:2 * S5_HALF]
            z_re, z_im = a1_re * z_re - a1_im * z_im + xj_re, a1_re * z_im + a1_im * z_re + xj_im
            if j % 2 == 1:
                yield
        zs_ref[:, 0:S5_HALF] = z_re
        zs_ref[:, S5_HALF:2 * S5_HALF] = z_im
        yield
        y_s5 = jnp.concatenate([_dot(ucat_ref[rb(b), :], skcat_ref[...]) for b in batches], axis=0)
        yield
        al_re = sa_ref[2:3, :]
        al_im = sa_ref[3:4, :]
        s_re = ss5_ref[0]
        s_im = ss5_ref[1]
        for n in range(n_s5):
            rows_n = slice(n * nb, (n + 1) * nb)
            sp_ref[rows_n, 0:S5_HALF] = s_re
            sp_ref[rows_n, S5_HALF:2 * S5_HALF] = s_im
            zn_re = zs_ref[rows_n, 0:S5_HALF]
            zn_im = zs_ref[rows_n, S5_HALF:2 * S5_HALF]
            s_re, s_im = al_re * s_re - al_im * s_im + zn_re, al_re * s_im + al_im * s_re + zn_im
            if n % 4 == 3:
                yield
        ss5_ref[0] = s_re
        ss5_ref[1] = s_im
        sp_re = sp_ref[:, 0:S5_HALF][:, None, :]
        sp_im = sp_ref[:, S5_HALF:2 * S5_HALF][:, None, :]
        t_re = stre_ref[...][None]
        t_im = stim_ref[...][None]

        def batch_major(v):
            tiles = [v[(n * nb + b) * S5_LAGS:(n * nb + b + 1) * S5_LAGS] for b in batches for n in range(n_s5)]
            return jnp.concatenate(tiles, axis=0)

        v_re = batch_major((t_re * sp_re - t_im * sp_im).reshape(rr, S5_HALF))
        yield
        v_im = batch_major((t_re * sp_im + t_im * sp_re).reshape(rr, S5_HALF))
        yield
        vv = _bf(jnp.concatenate([v_re, v_im], axis=1))
        y_s5 = y_s5 + jnp.concatenate([_dot(vv[rb(b)], scbd_ref[...]) for b in batches], axis=0)
        yield
        c0 = math.sqrt(2.0 / math.pi)
        y_s5 = 0.5 * y_s5 * (1.0 + jnp.tanh(c0 * (y_s5 + 0.044715 * (y_s5 * y_s5 * y_s5))))
        y_s5 = y_s5 * _sigmoid(_dot(_bf(y_s5), swglu_ref[...]) + sbglu_ref[...])
        y_ref[:, 2 * BRANCH_W:3 * BRANCH_W] = _bf(y_s5 * _silu(ps_ref[:, S_G:S_G + BRANCH_W]))
        yield
        out_project(2)

    def gla():
        z = _dot(_bf(pg_ref[:, L_CODE:L_CODE + LANE]), gwgk_ref[...]) + gb_ref[...]
        gk = -_softplus(-z) * (1.0 / GLA_GATE_TAU)
        yield
        cum = jnp.concatenate([_dot_m2(btri_ref[...], gk[rb(b)]) for b in batches], axis=0)
        yield
        q = pg_ref[:, L_Q:L_Q + GLA_QK]
        k = pg_ref[:, L_K:L_K + GLA_QK]
        v = pg_ref[:, L_V:L_V + BRANCH_W]
        qt = _bf(q * jnp.exp(cum) * (GLA_DK ** -0.5))
        kt = k * jnp.exp(-cum)
        clast = [cum[sl][CHUNK - 1:CHUNK, :] for _, _, sl in units]
        yield
        s_all = []
        for b, _, sl in units:
            s_all.append(_dot_nt(qt[sl], _stack_heads(kt[sl], hmg_ref)))
            if b == nb - 1:
                yield
        o_intra = []
        for u, (b, _, sl) in enumerate(units):
            o_intra.append(_dot(_bf(s_all[u] * tril_ref[...]), _stack_heads(v[sl], hm_ref)))
            if b == nb - 1:
                yield
        kv = []
        for u, (b, _, sl) in enumerate(units):
            kv.append(bdg_ref[...] * _dot_tn(_bf(v[sl]), _bf(k[sl] * jnp.exp(clast[u] - cum[sl]))))
            if b == nb - 1:
                yield
        st = [sgla_ref[b] for b in batches]
        o_parts = {}
        for u, (b, c, sl) in enumerate(units):
            o_parts[(b, c)] = o_intra[u] + _dot_nt(qt[sl], _bf(st[b]))
            st[b] = st[b] * jnp.exp(clast[u]) + kv[u]
            if b == nb - 1:
                yield
        for b in batches:
            sgla_ref[b] = st[b]
        o = jnp.concatenate([o_parts[(b, c)] for b in batches for c in range(n_chunks)], axis=0)
        y_ref[:, 3 * BRANCH_W:4 * BRANCH_W] = _bf(
            _head_rmsnorm_gate(o, gnorm_ref[...], pg_ref[:, L_G:L_G + BRANCH_W], hs))
        yield
        out_project(3)

    _interleave(projections(), {"s5": s5(), "deltanet": deltanet(), "retention": retention(), "gla": gla()})

    o = o_partial[0] + o_partial[1] + o_partial[2] + o_partial[3]
    o = o * lax.rsqrt(jnp.mean(o * o, axis=-1, keepdims=True) + EPS) * gpost_ref[...]
    for b in batches:
        out_ref[b] = x_ref[b] + o[rb(b)]


def _reorder_in_proj(w_in):
    offs = np.concatenate([[0], np.cumsum(IN_SPLITS)])
    (o_rq, o_rk, o_rv, o_rg, o_dqkv, o_dbeta, o_da, o_dg, o_su, o_sg,
     o_gq, o_gk, o_gv, o_gcode, o_gg) = [int(o) for o in offs[:-1]]
    dp, dm, _ = w_in.shape
    w = w_in

    def deinterleave(o):
        t = w[:, :, o:o + 256].reshape(dp, dm, N_HEADS, HEAD_DIM // 2, 2)
        return jnp.transpose(t, (0, 1, 4, 2, 3)).reshape(dp, dm, 256)

    assert o_da == o_dbeta + 4 and o_dg == o_da + 4 and o_gg == o_gcode + GLA_GATE_RANK
    pieces = [deinterleave(o_rq), deinterleave(o_rk),
              w[:, :, o_rv:o_dg], jnp.zeros((dp, dm, LANE - 8), w.dtype),
              w[:, :, o_dg:o_gg], jnp.zeros((dp, dm, LANE - GLA_GATE_RANK), w.dtype),
              w[:, :, o_gg:]]
    out = _bf(jnp.concatenate(pieces, axis=2))
    assert out.shape[2] == P_COLS
    return out


@functools.lru_cache(maxsize=None)
def _constant_tables(seq_len, tb, nb):
    c = CHUNK
    lane = np.arange(256)
    head_std = lane // 64
    head_qk = (lane % 128) // 32
    head_g = np.arange(128) // 32
    i = np.arange(c)[:, None]
    j = np.arange(256)[None, :] % 64
    t = {}
    heads = np.arange(4)[:, None, None]
    t["hm"] = np.broadcast_to(head_std[None, None, :] == heads, (4, tb, 256)).astype(np.float32)
    t["hmqk"] = np.broadcast_to(head_qk[None, None, :] == heads, (4, tb, 256)).astype(np.float32)
    t["hmg"] = np.broadcast_to(head_g[None, None, :] == heads, (4, c, 128)).astype(np.float32)
    t["hs"] = (head_std[:, None] == head_std[None, :]).astype(np.float32)
    t["bd"] = t["hs"]
    t["bdqk"] = (head_std[:, None] == head_qk[None, :]).astype(np.float32)
    t["bdg"] = (head_std[:, None] == head_g[None, :]).astype(np.float32)
    t["tril"] = (i >= j).astype(np.float32)
    t["strict"] = (i > j).astype(np.float32)
    t["eye"] = (i == j).astype(np.float32)
    r = np.arange(tb)
    same_chunk = (r[:, None] // c) == (r[None, :] // c)
    t["btri"] = (same_chunk & (r[:, None] >= r[None, :])).astype(np.float32)
    t["selc"] = ((r[None, :] // c) == np.arange(16)[:, None]).astype(np.float32)
    n_s5 = tb // S5_LAGS
    jj, nn, bb = np.meshgrid(np.arange(S5_LAGS), np.arange(n_s5), np.arange(nb), indexing="ij")
    old = (bb * tb + nn * S5_LAGS + jj).reshape(-1)
    sperm = np.zeros((nb * tb, nb * tb), np.float32)
    sperm[np.arange(nb * tb), old] = 1.0
    t["sperm"] = sperm
    t["eyer"] = np.tile(t["eye"], (tb // c, 1))
    lg = np.log(1.0 - 2.0 ** (-5.0 - np.arange(4, dtype=np.float64)))
    ri = r[:, None]
    rj = np.arange(4 * tb)[None, :] % tb
    lg_cols = lg[np.arange(4 * tb) // tb][None, :]
    t["rdall"] = np.where(ri >= rj, np.exp(lg_cols * np.where(ri >= rj, ri - rj, 0)), 0.0).astype(np.float32)
    lg_qk = lg[head_qk][None, :]
    t["rqdec"] = np.exp(lg_qk * (ri + 1.0)).astype(np.float32)
    t["rkdec"] = np.exp(lg_qk * (tb - 1.0 - ri)).astype(np.float32)
    t["rcd"] = np.exp(lg_qk * tb).astype(np.float32)
    inv = ROPE_BASE ** (-np.arange(0, HEAD_DIM, 2, dtype=np.float64) / HEAD_DIM)
    ang = np.arange(seq_len, dtype=np.float64)[:, None] * inv[None, :]
    t["cos"] = np.tile(np.cos(ang), (1, 4)).astype(np.float32)
    t["sin"] = np.tile(np.sin(ang), (1, 4)).astype(np.float32)
    t["tile16"] = np.tile(np.eye(S5_GROUP, dtype=np.float32), (1, S5_GROUPS))
    t["tile64"] = np.tile(np.eye(S5_STATE, dtype=np.float32), (1, S5_GROUPS))
    return t


def _s5_tables(lam_re, lam_im, b_re, b_im, c_re, c_im, d, log_dt, tabs):
    hp = lax.Precision.HIGHEST
    g, p, hc = S5_GROUPS, S5_STATE, S5_GROUP
    nl = lam_re.shape[0]
    lam_re, lam_im = lam_re.astype(F32), lam_im.astype(F32)
    dt = jnp.exp(log_dt.astype(F32))[..., None]
    mag = jnp.exp(lam_re * dt)
    ang = lam_im * dt
    a_re, a_im = mag * jnp.cos(ang), mag * jnp.sin(ang)
    den = lam_re * lam_re + lam_im * lam_im
    nr, ni = a_re - 1.0, a_im
    coef_re = (nr * lam_re + ni * lam_im) / den
    coef_im = (ni * lam_re - nr * lam_im) / den
    b_re, b_im = b_re.astype(F32), b_im.astype(F32)
    bb_re = coef_re[..., None] * b_re - coef_im[..., None] * b_im
    bb_im = coef_re[..., None] * b_im + coef_im[..., None] * b_re

    def apow(n):
        n = jnp.asarray(n, F32)[None, :, None, None]
        m = jnp.exp((lam_re * dt)[:, None] * n)
        return m * jnp.cos(ang[:, None] * n), m * jnp.sin(ang[:, None] * n)

    lags = np.arange(S5_LAGS)
    p_re, p_im = apow(lags)
    c_re, c_im = c_re.astype(F32), c_im.astype(F32)
    ab_re = p_re[..., None] * bb_re[:, None] - p_im[..., None] * bb_im[:, None]
    ab_im = p_re[..., None] * bb_im[:, None] + p_im[..., None] * bb_re[:, None]
    kk = (jnp.einsum('ntgpi,ngop->ntgio', ab_re, c_re, precision=hp)
          - jnp.einsum('ntgpi,ngop->ntgio', ab_im, c_im, precision=hp))
    kk = kk.at[:, 0].add(d.astype(F32)[..., None] * jnp.eye(hc, dtype=F32))
    grp256 = np.arange(g * hc) // hc
    grp1024 = np.arange(g * p) // p
    tile16 = jnp.asarray(tabs["tile16"])
    tile64 = jnp.asarray(tabs["tile64"])
    kcat = jnp.einsum('nro,oc->nrc', kk.reshape(nl, S5_LAGS * g * hc, hc), tile16, precision=hp)
    kcat = jnp.where(jnp.asarray(np.tile(grp256, S5_LAGS)[:, None] == grp256[None, :]), kcat, 0.0)

    def rows_to_state(bb):
        m = jnp.einsum('nrp,pc->nrc', jnp.transpose(bb, (0, 1, 3, 2)).reshape(nl, g * hc, p), tile64,
                       precision=hp)
        return jnp.where(jnp.asarray(grp256[:, None] == grp1024[None, :]), m, 0.0)

    def state_to_rows(cc):
        m = jnp.einsum('nro,oc->nrc', jnp.transpose(cc, (0, 1, 3, 2)).reshape(nl, g * p, hc), tile16,
                       precision=hp)
        return jnp.where(jnp.asarray(grp1024[:, None] == grp256[None, :]), m, 0.0)

    bbar = jnp.concatenate([rows_to_state(bb_re), rows_to_state(bb_im)], axis=2)
    cbd = jnp.concatenate([state_to_rows(c_re), -state_to_rows(c_im)], axis=1)
    t_re, t_im = apow(lags + 1)
    a1_re, a1_im = apow(np.array([1]))
    al_re, al_im = apow(np.array([S5_LAGS]))
    flat = lambda z: z.reshape(nl, z.shape[1], g * p)
    return dict(skcat=_bf(kcat), sbbar=_bf(bbar), scbd=_bf(cbd),
                stre=flat(t_re), stim=flat(t_im),
                sa=jnp.concatenate([flat(a1_re), flat(a1_im), flat(al_re), flat(al_im)], axis=1))


def _const_spec(arr):
    nd = arr.ndim
    return pl.BlockSpec(arr.shape, lambda b, t, _nd=nd: (0,) * _nd)


def _layer_spec(arr, layer):
    nd = arr.ndim - 1
    return pl.BlockSpec((None,) + arr.shape[1:], lambda b, t, _nd=nd, _l=layer: (_l,) + (0,) * _nd)


def _layer_call(x, layer, stacked, const_inputs, cos, sin, tb, nb):
    bsz, seq, _ = x.shape
    grid = (bsz // nb, seq // tb)
    rr = nb * tb
    in_specs = [pl.BlockSpec((nb, tb, D_MODEL), lambda b, t: (b, t, 0)),
                pl.BlockSpec((tb, LANE), lambda b, t: (t, 0)),
                pl.BlockSpec((tb, LANE), lambda b, t: (t, 0))]
    operands = [x, cos, sin]
    order = ["gpre", "gpost", "win", "wout",
             "hm", "hmqk", "hmg", "hs", "bd", "bdqk", "bdg", "tril", "strict", "eye", "eyer", "btri", "selc", "sperm",
             "rdall", "rqdec", "rkdec", "rcd", "rnorm",
             "dconv", "darow", "dbias", "debeta", "deg", "dnorm",
             "skcat", "sbbar", "scbd", "stre", "stim", "sa", "swglu", "sbglu",
             "gwgk", "gb", "gnorm"]
    for name in order:
        if name in stacked:
            operands.append(stacked[name])
            in_specs.append(_layer_spec(stacked[name], layer))
        else:
            operands.append(const_inputs[name])
            in_specs.append(_const_spec(const_inputs[name]))
    scratch = [
        pltpu.VMEM((rr, D_MODEL), BF16),
        pltpu.VMEM((rr, G_RET[1]), F32),
        pltpu.VMEM((rr, G_DN[1]), F32),
        pltpu.VMEM((rr, G_S5[1]), F32),
        pltpu.VMEM((rr, G_GLA[1]), F32),
        pltpu.VMEM((rr, D_MODEL), BF16),
        pltpu.VMEM((nb, tb + 8, 768), F32),
        pltpu.VMEM((rr, 768), F32),
        pltpu.VMEM((nb, tb + S5_LAGS, BRANCH_W), F32),
        pltpu.VMEM((rr, S5_LAGS * BRANCH_W), BF16),
        pltpu.VMEM((rr // S5_LAGS, 2 * S5_HALF), F32),
        pltpu.VMEM((rr // S5_LAGS, 2 * S5_HALF), F32),
        pltpu.VMEM((nb, BRANCH_W, BRANCH_W), F32),
        pltpu.VMEM((nb, BRANCH_W, BRANCH_W), F32),
        pltpu.VMEM((nb, BRANCH_W, GLA_QK), F32),
        pltpu.VMEM((2, nb, S5_HALF), F32),
    ]
    return pl.pallas_call(
        functools.partial(_layer_kernel, tb=tb, nb=nb),
        grid=grid,
        in_specs=in_specs,
        out_specs=pl.BlockSpec((nb, tb, D_MODEL), lambda b, t: (b, t, 0)),
        out_shape=jax.ShapeDtypeStruct(x.shape, x.dtype),
        scratch_shapes=scratch,
        compiler_params=pltpu.CompilerParams(
            dimension_semantics=("arbitrary", "arbitrary"),
            vmem_limit_bytes=VMEM_LIMIT_BYTES),
        name="hybrid_layer",
    )(*operands)


def _pick_tiles(bsz, seq):
    nb = 2 if bsz % 2 == 0 else 1
    for tb in (256, 128, 64):
        if seq % tb == 0:
            return tb, nb
    raise ValueError(f"sequence length {seq} must be a multiple of {CHUNK}")


def kernel(x, norm_pre, norm_post, w_in, w_out, ret_norm, dn_conv, dn_a_log, dn_dt_bias, dn_norm,
           s5_lam_re, s5_lam_im, s5_b_re, s5_b_im, s5_c_re, s5_c_im, s5_d, s5_log_dt, s5_w_glu, s5_b_glu,
           gla_w_gk, gla_b_gk, gla_norm):
    bsz, seq, dm = x.shape
    depth = w_in.shape[0]
    assert dm == D_MODEL and x.dtype == F32
    tb, nb = _pick_tiles(bsz, seq)
    tabs = _constant_tables(seq, tb, nb)
    bf_names = ("hm", "hmqk", "hmg", "hs", "bd", "btri", "selc", "sperm")
    skip = ("cos", "sin", "tile16", "tile64")
    const_inputs = {k: jnp.asarray(v, BF16 if k in bf_names else F32)
                    for k, v in tabs.items() if k not in skip}
    cos = jnp.asarray(tabs["cos"])
    sin = jnp.asarray(tabs["sin"])

    w_in_r = _reorder_in_proj(w_in)
    w_out_b = _bf(w_out)

    e_beta = np.zeros((LANE, 256), np.float32)
    e_g = np.zeros((LANE, 256), np.float32)
    for hh in range(N_HEADS):
        e_beta[SM_BETA + hh, 64 * hh:64 * hh + 64] = 1.0
        e_g[SM_A + hh, 64 * hh:64 * hh + 64] = 1.0
    e_beta = jnp.asarray(e_beta, BF16)
    e_g = jnp.asarray(e_g, BF16)

    a_rows = jnp.zeros((depth, 1, LANE), F32).at[:, 0, SM_A:SM_A + 4].set(jnp.exp(dn_a_log.astype(F32)))
    b_rows = jnp.zeros((depth, 1, LANE), F32).at[:, 0, SM_A:SM_A + 4].set(dn_dt_bias.astype(F32))
    wgk = jnp.zeros((depth, LANE, GLA_QK), BF16).at[:, 0:GLA_GATE_RANK, :].set(_bf(gla_w_gk))
    s5 = _s5_tables(s5_lam_re, s5_lam_im, s5_b_re, s5_b_im, s5_c_re, s5_c_im, s5_d, s5_log_dt, tabs)
    tile4 = lambda g: jnp.tile(g.astype(F32), (1, N_HEADS)).reshape(depth, 1, 256)
    rnorm, dnorm, gnorm = tile4(ret_norm), tile4(dn_norm), tile4(gla_norm)
    swglu = _bf(s5_w_glu)

    stacked = dict(
        gpre=norm_pre.reshape(depth, 1, dm).astype(F32),
        gpost=norm_post.reshape(depth, 1, dm).astype(F32),
        win=w_in_r, wout=w_out_b,
        rnorm=rnorm,
        dconv=dn_conv.astype(F32),
        darow=a_rows, dbias=b_rows,
        dnorm=dnorm,
        swglu=swglu, sbglu=s5_b_glu.reshape(depth, 1, 256).astype(F32),
        gwgk=wgk, gb=gla_b_gk.reshape(depth, 1, GLA_QK).astype(F32),
        gnorm=gnorm,
    )
    stacked.update(s5)
    const_inputs.update(debeta=e_beta, deg=e_g)
    for i in range(depth):
        x = _layer_call(x, i, stacked, const_inputs, cos, sin, tb, nb)
    return x
```

```python
import functools
import math

import jax
import jax.numpy as jnp
import numpy as np
from jax import lax
from jax.experimental import pallas as pl
from jax.experimental.pallas import tpu as pltpu

F32 = jnp.float32
BF16 = jnp.bfloat16

D_MODEL = 1024
BRANCH_W = 256
N_HEADS = 4
HEAD_DIM = 64
EPS = 1e-6
ROPE_BASE = 10000.0
DN_CONV = 4
S5_GROUP = 16
S5_GROUPS = 16
S5_STATE = 64
S5_HALF = S5_GROUPS * S5_STATE
GLA_DK = 32
GLA_QK = 128
GLA_GATE_RANK = 16
GLA_GATE_TAU = 16.0
IN_SPLITS = [256, 256, 256, 256, 768, 4, 4, 256, 256, 256, 128, 128, 256, 16, 256]

CHUNK = 64
RET_CHUNK = 128
S5_LAGS = 8
LANE = 128
PROJ_COLS = 256
VMEM_LIMIT_BYTES = 56 * 1024 * 1024

_OFF = np.concatenate([[0], np.cumsum(IN_SPLITS)])
(O_RQ, O_RK, O_RV, O_RG, O_DQKV, O_DBETA, O_DA, O_DG, O_SU, O_SG,
 O_GQ, O_GK, O_GV, O_GCODE, O_GG) = (int(o) for o in _OFF[:-1])
assert O_DBETA % LANE == 0 and O_DA == O_DBETA + 4 and O_DG == O_DA + 4
_T = lambda o: o - O_DG
PROJ_GROUPS = {
    "retention": [("rot", 0, 256), ("rot", 256, 256), ("main", O_RV, 256), ("main", O_RG, 256)],
    "deltanet": [("main", O_DQKV, 256), ("main", O_DQKV + 256, 256), ("main", O_DQKV + 512, 256),
                 ("main", O_DBETA, LANE), ("tail", _T(O_DG), 256)],
    "s5": [("tail", _T(O_SU), 256), ("tail", _T(O_SG), 256)],
    "gla": [("tail", _T(O_GQ), 256), ("tail", _T(O_GV), 256), ("tail", _T(O_GCODE), LANE), ("gg", 0, 256)],
}
G_RET, G_DN, G_S5, G_GLA = (sum(w for _, _, w in PROJ_GROUPS[k]) for k in ("retention", "deltanet", "s5", "gla"))
TAIL_COLS = _T(O_GCODE) + LANE
R_Q, R_K, R_V, R_G = 0, 256, 512, 768
D_QKV, D_SMALL, D_G = 0, 768, 896
S_U, S_G = 0, 256
L_Q, L_K, L_V, L_CODE, L_G = 0, 128, 256, 512, 640
SM_BETA, SM_A = 0, 4


def _bf(x):
    return x.astype(BF16)


def _dot(a, b):
    return jnp.dot(a, b, preferred_element_type=F32)


def _dot_nt(a, b):
    return lax.dot_general(a, b, (((1,), (1,)), ((), ())), preferred_element_type=F32)


def _dot_tn(a, b):
    return lax.dot_general(a, b, (((0,), (0,)), ((), ())), preferred_element_type=F32)


def _split2(x):
    x1 = _bf(x)
    x2 = _bf(x - x1.astype(F32))
    return x1, x2


def _dot_x2(x, m):
    x1, x2 = _split2(x)
    return _dot(x1, m) + _dot(x2, m)


def _dot_m2(m, x):
    x1, x2 = _split2(x)
    return _dot(m, x1) + _dot(m, x2)


def _sigmoid(x):
    return 1.0 / (1.0 + jnp.exp(-x))


def _silu(x):
    return x * _sigmoid(x)


def _softplus(x):
    return jnp.maximum(x, 0.0) + jnp.log(1.0 + jnp.exp(-jnp.abs(x)))


def _stack_heads(x, hm_ref):
    xb = _bf(x)
    rows = x.shape[0]
    return jnp.concatenate([xb * hm_ref[h, 0:rows, :] for h in range(N_HEADS)], axis=0)


def _head_rmsnorm_gate(o, g_row, gate, hs):
    ms = _dot(_bf(o * o), hs) * (1.0 / HEAD_DIM)
    return o * lax.rsqrt(ms + EPS) * g_row * _silu(gate)


def _interleave(feeder, consumers):
    live = []
    feeding = True
    while feeding or live:
        if feeding:
            try:
                tag = next(feeder)
                while tag is not None:
                    live.append(consumers.pop(tag))
                    tag = next(feeder)
            except StopIteration:
                feeding = False
        for g in list(live):
            try:
                next(g)
            except StopIteration:
                live.remove(g)
    assert not consumers


def _layer_kernel(
        x_ref, cos_ref, sin_ref,
        gpre_ref, gpost_ref, wrot_ref, wmain_ref, wtail_ref, wgg_ref, wout_ref,
        hm_ref, hmqk_ref, hmg_ref, hs_ref, bd_ref, bdqk_ref, bdg_ref,
        tril_ref, strict_ref, eye_ref, eyer_ref, btri_ref, selc_ref, sperm_ref,
        rdall_ref, rqdec_ref, rkdec_ref, rcd_ref, rnorm_ref,
        dconv_ref, darow_ref, dbias_ref, debeta_ref, deg_ref, dnorm_ref,
        skcat_ref, sbbar_ref, scbd_ref, stre_ref, stim_ref, sa_ref,
        swglu_ref, sbglu_ref,
        gwgk_ref, gb_ref, gnorm_ref,
        out_ref,
        h_ref, pr_ref, pd_ref, ps_ref, pg_ref, y_ref, xc_ref, qkv_ref, us_ref, ucat_ref, zs_ref, sp_ref,
        sret_ref, sdn_ref, sgla_ref, ss5_ref,
        *, tb, nb):
    t_idx = pl.program_id(1)
    n_chunks = tb // CHUNK
    n_s5 = tb // S5_LAGS
    batches = range(nb)
    rr = nb * tb

    def rb(b):
        return slice(b * tb, (b + 1) * tb)

    units = [(b, c, slice(b * tb + c * CHUNK, b * tb + (c + 1) * CHUNK))
             for c in range(n_chunks) for b in batches]
    n_units = len(units)

    @pl.when(t_idx == 0)
    def _reset():
        sret_ref[...] = jnp.zeros_like(sret_ref)
        sdn_ref[...] = jnp.zeros_like(sdn_ref)
        sgla_ref[...] = jnp.zeros_like(sgla_ref)
        ss5_ref[...] = jnp.zeros_like(ss5_ref)
        for b in batches:
            xc_ref[b, 0:8, :] = jnp.zeros((8, 768), F32)
            us_ref[b, 0:S5_LAGS, :] = jnp.zeros((S5_LAGS, BRANCH_W), F32)

    for b in batches:
        x = x_ref[b]
        h_ref[rb(b), :] = _bf(x * lax.rsqrt(jnp.mean(x * x, axis=-1, keepdims=True) + EPS) * gpre_ref[...])

    def projections():
        weights = {"rot": wrot_ref, "main": wmain_ref, "tail": wtail_ref, "gg": wgg_ref}
        for name, dst_ref in (("deltanet", pd_ref), ("s5", ps_ref), ("gla", pg_ref), ("retention", pr_ref)):
            c0 = 0
            for operand, first, width in PROJ_GROUPS[name]:
                dst_ref[:, c0:c0 + width] = _dot(h_ref[...], weights[operand][:, first:first + width])
                c0 += width
                yield None
            yield name

    o_partial = []

    def out_project(branch):
        rows = slice(branch * BRANCH_W, (branch + 1) * BRANCH_W)
        o_partial.append(_dot(y_ref[:, rows], wout_ref[rows, :]))

    hs = hs_ref[...]
    bd = bd_ref[...]
    bd32 = bd.astype(F32)

    def retention():
        rc = min(RET_CHUNK, tb)
        n_rc = tb // rc
        runits = [(b, c, slice(b * tb + c * rc, b * tb + (c + 1) * rc)) for c in range(n_rc) for b in batches]
        ret_q, ret_k, ret_v = {}, {}, {}
        for b, c, sl in runits:
            cs = cos_ref[c * rc:(c + 1) * rc, :]
            sn = sin_ref[c * rc:(c + 1) * rc, :]
            qa = pr_ref[sl, R_Q:R_Q + LANE]
            qb = pr_ref[sl, R_Q + LANE:R_Q + 2 * LANE]
            ka = pr_ref[sl, R_K:R_K + LANE]
            kb = pr_ref[sl, R_K + LANE:R_K + 2 * LANE]
            ret_q[b, c] = jnp.concatenate([qa * cs - qb * sn, qa * sn + qb * cs], axis=1) * (HEAD_DIM ** -0.5)
            ret_k[b, c] = jnp.concatenate([ka * cs - kb * sn, ka * sn + kb * cs], axis=1)
            ret_v[b, c] = pr_ref[sl, R_V:R_V + BRANCH_W]
            if b == nb - 1:
                yield
        o_intra, kv = {}, {}
        for b, c, sl in runits:
            s_all = _dot_nt(_bf(ret_q[b, c]), _stack_heads(ret_k[b, c], hmqk_ref))
            o_intra[b, c] = _dot(_bf(s_all * rdall_ref[...]), _stack_heads(ret_v[b, c], hm_ref))
            kv[b, c] = bdqk_ref[...] * _dot_tn(_bf(ret_v[b, c]), _bf(ret_k[b, c] * rkdec_ref[...]))
            if b == nb - 1:
                yield
        st = [sret_ref[b] for b in batches]
        o_ret = {}
        for b, c, sl in runits:
            o_ret[b, c] = o_intra[b, c] + _dot_nt(_bf(ret_q[b, c] * rqdec_ref[...]), _bf(st[b]))
            st[b] = st[b] * rcd_ref[...] + kv[b, c]
            if b == nb - 1:
                yield
        for b in batches:
            sret_ref[b] = st[b]
        o = jnp.concatenate([o_ret[b, c] for b in batches for c in range(n_rc)], axis=0)
        y_ref[:, 0:BRANCH_W] = _bf(_head_rmsnorm_gate(o, rnorm_ref[...], pr_ref[:, R_G:R_G + BRANCH_W], hs))
        yield
        out_project(0)

    def deltanet():
        for b in batches:
            xc_ref[b, 8:8 + tb, :] = pd_ref[rb(b), D_QKV:D_QKV + 768]
            conv = dconv_ref[DN_CONV - 1:DN_CONV, :] * xc_ref[b, 8:8 + tb, :]
            for i in range(DN_CONV - 1):
                conv = conv + dconv_ref[i:i + 1, :] * xc_ref[b, 5 + i:5 + i + tb, :]
            qkv_ref[rb(b), :] = _silu(conv)
            xc_ref[b, 0:8, :] = xc_ref[b, tb:tb + 8, :]
            yield

        q = qkv_ref[:, 0:256]
        k = qkv_ref[:, 256:512]
        v = qkv_ref[:, 512:768]
        ss = _dot(_bf(jnp.concatenate([q * q, k * k], axis=0)), hs)
        qn = q * lax.rsqrt(ss[0:rr] + EPS) * (HEAD_DIM ** -0.5)
        kn = k * lax.rsqrt(ss[rr:2 * rr] + EPS)
        yield
        small = pd_ref[:, D_SMALL:D_SMALL + LANE]
        beta = _dot(_bf(_sigmoid(small)), debeta_ref[...])
        g_s = -darow_ref[...] * _softplus(small + dbias_ref[...])
        gcum_s = jnp.concatenate([_dot_m2(btri_ref[...], g_s[rb(b)]) for b in batches], axis=0)
        yield
        gcum = _dot_x2(gcum_s, deg_ref[...])
        grows = [_dot_m2(selc_ref[...], gcum[rb(b)] * eyer_ref[...]) for b in batches]
        yield
        egc = jnp.exp(gcum)
        kbeta = kn * beta
        vbeta = v * beta
        kbe = kbeta * egc
        qg = qn * egc
        yield

        dec, glast, kgt = [], [], []
        for b, c, sl in units:
            gc = gcum[sl]
            dec.append(jnp.exp(jnp.minimum(gc - grows[b][c:c + 1, :], 0.0)))
            glast.append(gc[CHUNK - 1:CHUNK, :])
            kgt.append(_bf((kn[sl] * jnp.exp(glast[-1] - gc)).T))
            if b == nb - 1:
                yield
        aa = []
        for b, _, sl in units:
            aa.append(_dot_nt(_bf(jnp.concatenate([kbeta[sl], qn[sl]], axis=0)), _stack_heads(kn[sl], hm_ref)))
            if b == nb - 1:
                yield
        attn = [aa[u][CHUNK:2 * CHUNK] * dec[u] * tril_ref[...] for u in range(n_units)]

        def blockdiag(m):
            mb = _bf(m)
            return jnp.concatenate([mb, mb, mb, mb], axis=0) * bd

        pw = [-(aa[u][0:CHUNK] * dec[u] * strict_ref[...]) for u in range(n_units)]
        t_all = [eye_ref[...] + pw[u] for u in range(n_units)]
        yield
        pw = [_dot(_bf(pw[u]), blockdiag(pw[u])) for u in range(n_units)]
        yield
        for level in range(1, 5):
            both = [_dot(_bf(jnp.concatenate([pw[u], t_all[u]], axis=0)), blockdiag(pw[u]))
                    for u in range(n_units)]
            pw = [both[u][0:CHUNK] for u in range(n_units)]
            t_all = [t_all[u] + both[u][CHUNK:2 * CHUNK] for u in range(n_units)]
            yield
        t_all = [t_all[u] + _dot(_bf(t_all[u]), blockdiag(pw[u])) for u in range(n_units)]
        yield
        uw = []
        for u, (b, _, sl) in enumerate(units):
            uw.append(_dot(_bf(t_all[u]), jnp.concatenate([_stack_heads(vbeta[sl], hm_ref),
                                                           _stack_heads(kbe[sl], hm_ref)], axis=1)))
            if b == nb - 1:
                yield

        st = [sdn_ref[b] for b in batches]
        o_parts = {}
        for u, (b, c, sl) in enumerate(units):
            wq = _dot(_bf(jnp.concatenate([uw[u][:, 256:512], qg[sl]], axis=0)), _bf(st[b]))
            v_new = uw[u][:, 0:256] - wq[0:CHUNK]
            o_parts[(b, c)] = wq[CHUNK:2 * CHUNK] + _dot(_bf(attn[u]), _stack_heads(v_new, hm_ref))
            st[b] = st[b] * jnp.exp(glast[u]) + bd32 * _dot(kgt[u], _bf(v_new))
            if b == nb - 1:
                yield
        for b in batches:
            sdn_ref[b] = st[b]
        o = jnp.concatenate([o_parts[(b, c)] for b in batches for c in range(n_chunks)], axis=0)
        y_ref[:, BRANCH_W:2 * BRANCH_W] = _bf(
            _head_rmsnorm_gate(o, dnorm_ref[...], pd_ref[:, D_G:D_G + BRANCH_W], hs))
        yield
        out_project(1)

    def s5():
        u_in = ps_ref[:, S_U:S_U + BRANCH_W]
        row_in_chunk = lax.broadcasted_iota(jnp.int32, (tb, BRANCH_W), 0) % S5_LAGS
        ucat_ref[:, 0:BRANCH_W] = _bf(u_in)
        for b in batches:
            us_ref[b, S5_LAGS:S5_LAGS + tb, :] = u_in[rb(b)]
            for tau in range(1, S5_LAGS):
                shifted = us_ref[b, S5_LAGS - tau:S5_LAGS - tau + tb, :]
                ucat_ref[rb(b), tau * BRANCH_W:(tau + 1) * BRANCH_W] = _bf(
                    jnp.where(row_in_chunk >= tau, shifted, 0.0))
            us_ref[b, 0:S5_LAGS, :] = us_ref[b, tb:tb + S5_LAGS, :]
            yield
        u_perm = _bf(_dot(sperm_ref[...], _bf(u_in)))
        xx = _dot(u_perm, sbbar_ref[...])
        yield
        a1_re = sa_ref[0:1, :]
        a1_im = sa_ref[1:2, :]
        n_cb = nb * n_s5
        z_re = xx[0:n_cb, 0:S5_HALF]
        z_im = xx[0:n_cb, S5_HALF:2 * S5_HALF]
        for j in range(1, S5_LAGS):
            xj_re = xx[j * n_cb:(j + 1) * n_cb, 0:S5_HALF]
            xj_im = xx[j * n_cb:(j + 1) * n_cb, S5_HALF:2 * S5_HALF]
            z_re, z_im = a1_re * z_re - a1_im * z_im + xj_re, a1_re * z_im + a1_im * z_re + xj_im
            if j % 2 == 1:
                yield
        zs_ref[:, 0:S5_HALF] = z_re
        zs_ref[:, S5_HALF:2 * S5_HALF] = z_im
        yield
        y_s5 = jnp.concatenate([_dot(ucat_ref[rb(b), :], skcat_ref[...]) for b in batches], axis=0)
        yield
        al_re = sa_ref[2:3, :]
        al_im = sa_ref[3:4, :]
        s_re = ss5_ref[0]
        s_im = ss5_ref[1]
        for n in range(n_s5):
            rows_n = slice(n * nb, (n + 1) * nb)
            sp_ref[rows_n, 0:S5_HALF] = s_re
            sp_ref[rows_n, S5_HALF:2 * S5_HALF] = s_im
            zn_re = zs_ref[rows_n, 0:S5_HALF]
            zn_im = zs_ref[rows_n, S5_HALF:2 * S5_HALF]
            s_re, s_im = al_re * s_re - al_im * s_im + zn_re, al_re * s_im + al_im * s_re + zn_im
            if n % 4 == 3:
                yield
        ss5_ref[0] = s_re
        ss5_ref[1] = s_im
        sp_re = sp_ref[:, 0:S5_HALF][:, None, :]
        sp_im = sp_ref[:, S5_HALF:2 * S5_HALF][:, None, :]
        t_re = stre_ref[...][None]
        t_im = stim_ref[...][None]

        def batch_major(v):
            tiles = [v[(n * nb + b) * S5_LAGS:(n * nb + b + 1) * S5_LAGS] for b in batches for n in range(n_s5)]
            return jnp.concatenate(tiles, axis=0)

        v_re = batch_major((t_re * sp_re - t_im * sp_im).reshape(rr, S5_HALF))
        yield
        v_im = batch_major((t_re * sp_im + t_im * sp_re).reshape(rr, S5_HALF))
        yield
        vv = _bf(jnp.concatenate([v_re, v_im], axis=1))
        y_s5 = y_s5 + jnp.concatenate([_dot(vv[rb(b)], scbd_ref[...]) for b in batches], axis=0)
        yield
        c0 = math.sqrt(2.0 / math.pi)
        y_s5 = 0.5 * y_s5 * (1.0 + jnp.tanh(c0 * (y_s5 + 0.044715 * (y_s5 * y_s5 * y_s5))))
        y_s5 = y_s5 * _sigmoid(_dot(_bf(y_s5), swglu_ref[...]) + sbglu_ref[...])
        y_ref[:, 2 * BRANCH_W:3 * BRANCH_W] = _bf(y_s5 * _silu(ps_ref[:, S_G:S_G + BRANCH_W]))
        yield
        out_project(2)

    def gla():
        z = _dot(_bf(pg_ref[:, L_CODE:L_CODE + LANE]), gwgk_ref[...]) + gb_ref[...]
        gk = -_softplus(-z) * (1.0 / GLA_GATE_TAU)
        yield
        cum = jnp.concatenate([_dot_m2(btri_ref[...], gk[rb(b)]) for b in batches], axis=0)
        yield
        q = pg_ref[:, L_Q:L_Q + GLA_QK]
        k = pg_ref[:, L_K:L_K + GLA_QK]
        v = pg_ref[:, L_V:L_V + BRANCH_W]
        qt = _bf(q * jnp.exp(cum) * (GLA_DK ** -0.5))
        kt = k * jnp.exp(-cum)
        clast = [cum[sl][CHUNK - 1:CHUNK, :] for _, _, sl in units]
        yield
        s_all = []
        for b, _, sl in units:
            s_all.append(_dot_nt(qt[sl], _stack_heads(kt[sl], hmg_ref)))
            if b == nb - 1:
                yield
        o_intra = []
        for u, (b, _, sl) in enumerate(units):
            o_intra.append(_dot(_bf(s_all[u] * tril_ref[...]), _stack_heads(v[sl], hm_ref)))
            if b == nb - 1:
                yield
        kv = []
        for u, (b, _, sl) in enumerate(units):
            kv.append(bdg_ref[...] * _dot_tn(_bf(v[sl]), _bf(k[sl] * jnp.exp(clast[u] - cum[sl]))))
            if b == nb - 1:
                yield
        st = [sgla_ref[b] for b in batches]
        o_parts = {}
        for u, (b, c, sl) in enumerate(units):
            o_parts[(b, c)] = o_intra[u] + _dot_nt(qt[sl], _bf(st[b]))
            st[b] = st[b] * jnp.exp(clast[u]) + kv[u]
            if b == nb - 1:
                yield
        for b in batches:
            sgla_ref[b] = st[b]
        o = jnp.concatenate([o_parts[(b, c)] for b in batches for c in range(n_chunks)], axis=0)
        y_ref[:, 3 * BRANCH_W:4 * BRANCH_W] = _bf(
            _head_rmsnorm_gate(o, gnorm_ref[...], pg_ref[:, L_G:L_G + BRANCH_W], hs))
        yield
        out_project(3)

    _interleave(projections(), {"s5": s5(), "deltanet": deltanet(), "retention": retention(), "gla": gla()})

    o = o_partial[0] + o_partial[1] + o_partial[2] + o_partial[3]
    o = o * lax.rsqrt(jnp.mean(o * o, axis=-1, keepdims=True) + EPS) * gpost_ref[...]
    for b in batches:
        out_ref[b] = x_ref[b] + o[rb(b)]


def _in_proj_operands(w_in):
    dp, dm, _ = w_in.shape
    main = _bf(w_in)

    def deinterleave(o):
        t = w_in[:, :, o:o + 256].reshape(dp, dm, N_HEADS, HEAD_DIM // 2, 2)
        return jnp.transpose(t, (0, 1, 4, 2, 3)).reshape(dp, dm, 256)

    rot = _bf(jnp.concatenate([deinterleave(O_RQ), deinterleave(O_RK)], axis=2))
    return dict(wrot=rot, wmain=main, wtail=main[:, :, O_DG:O_DG + TAIL_COLS], wgg=main[:, :, O_GG:O_GG + 256])


@functools.lru_cache(maxsize=None)
def _constant_tables(seq_len, tb, nb):
    c = CHUNK
    lane = np.arange(256)
    head_std = lane // 64
    head_qk = (lane % 128) // 32
    head_g = np.arange(128) // 32
    i = np.arange(c)[:, None]
    j = np.arange(256)[None, :] % 64
    t = {}
    heads = np.arange(4)[:, None, None]
    t["hm"] = np.broadcast_to(head_std[None, None, :] == heads, (4, tb, 256)).astype(np.float32)
    t["hmqk"] = np.broadcast_to(head_qk[None, None, :] == heads, (4, tb, 256)).astype(np.float32)
    t["hmg"] = np.broadcast_to(head_g[None, None, :] == heads, (4, c, 128)).astype(np.float32)
    t["hs"] = (head_std[:, None] == head_std[None, :]).astype(np.float32)
    t["bd"] = t["hs"]
    t["bdqk"] = (head_std[:, None] == head_qk[None, :]).astype(np.float32)
    t["bdg"] = (head_std[:, None] == head_g[None, :]).astype(np.float32)
    t["tril"] = (i >= j).astype(np.float32)
    t["strict"] = (i > j).astype(np.float32)
    t["eye"] = (i == j).astype(np.float32)
    r = np.arange(tb)
    same_chunk = (r[:, None] // c) == (r[None, :] // c)
    t["btri"] = (same_chunk & (r[:, None] >= r[None, :])).astype(np.float32)
    t["selc"] = ((r[None, :] // c) == np.arange(16)[:, None]).astype(np.float32)
    n_s5 = tb // S5_LAGS
    jj, nn, bb = np.meshgrid(np.arange(S5_LAGS), np.arange(n_s5), np.arange(nb), indexing="ij")
    old = (bb * tb + nn * S5_LAGS + jj).reshape(-1)
    sperm = np.zeros((nb * tb, nb * tb), np.float32)
    sperm[np.arange(nb * tb), old] = 1.0
    t["sperm"] = sperm
    t["eyer"] = np.tile(t["eye"], (tb // c, 1))
    rc = min(RET_CHUNK, tb)
    lg = np.log(1.0 - 2.0 ** (-5.0 - np.arange(4, dtype=np.float64)))
    ri = np.arange(rc)[:, None]
    rj = np.arange(4 * rc)[None, :] % rc
    lg_cols = lg[np.arange(4 * rc) // rc][None, :]
    t["rdall"] = np.where(ri >= rj, np.exp(lg_cols * np.where(ri >= rj, ri - rj, 0)), 0.0).astype(np.float32)
    lg_qk = lg[head_qk][None, :]
    t["rqdec"] = np.exp(lg_qk * (ri + 1.0)).astype(np.float32)
    t["rkdec"] = np.exp(lg_qk * (rc - 1.0 - ri)).astype(np.float32)
    t["rcd"] = np.exp(lg_qk * rc).astype(np.float32)
    inv = ROPE_BASE ** (-np.arange(0, HEAD_DIM, 2, dtype=np.float64) / HEAD_DIM)
    ang = np.arange(seq_len, dtype=np.float64)[:, None] * inv[None, :]
    t["cos"] = np.tile(np.cos(ang), (1, 4)).astype(np.float32)
    t["sin"] = np.tile(np.sin(ang), (1, 4)).astype(np.float32)
    t["tile16"] = np.tile(np.eye(S5_GROUP, dtype=np.float32), (1, S5_GROUPS))
    t["tile64"] = np.tile(np.eye(S5_STATE, dtype=np.float32), (1, S5_GROUPS))
    return t


def _s5_tables(lam_re, lam_im, b_re, b_im, c_re, c_im, d, log_dt, tabs):
    hp = lax.Precision.HIGHEST
    g, p, hc = S5_GROUPS, S5_STATE, S5_GROUP
    nl = lam_re.shape[0]
    lam_re, lam_im = lam_re.astype(F32), lam_im.astype(F32)
    dt = jnp.exp(log_dt.astype(F32))[..., None]
    mag = jnp.exp(lam_re * dt)
    ang = lam_im * dt
    a_re, a_im = mag * jnp.cos(ang), mag * jnp.sin(ang)
    den = lam_re * lam_re + lam_im * lam_im
    nr, ni = a_re - 1.0, a_im
    coef_re = (nr * lam_re + ni * lam_im) / den
    coef_im = (ni * lam_re - nr * lam_im) / den
    b_re, b_im = b_re.astype(F32), b_im.astype(F32)
    bb_re = coef_re[..., None] * b_re - coef_im[..., None] * b_im
    bb_im = coef_re[..., None] * b_im + coef_im[..., None] * b_re

    def apow(n):
        n = jnp.asarray(n, F32)[None, :, None, None]
        m = jnp.exp((lam_re * dt)[:, None] * n)
        return m * jnp.cos(ang[:, None] * n), m * jnp.sin(ang[:, None] * n)

    lags = np.arange(S5_LAGS)
    p_re, p_im = apow(lags)
    c_re, c_im = c_re.astype(F32), c_im.astype(F32)
    ab_re = p_re[..., None] * bb_re[:, None] - p_im[..., None] * bb_im[:, None]
    ab_im = p_re[..., None] * bb_im[:, None] + p_im[..., None] * bb_re[:, None]
    kk = (jnp.einsum('ntgpi,ngop->ntgio', ab_re, c_re, precision=hp)
          - jnp.einsum('ntgpi,ngop->ntgio', ab_im, c_im, precision=hp))
    kk = kk.at[:, 0].add(d.astype(F32)[..., None] * jnp.eye(hc, dtype=F32))
    grp256 = np.arange(g * hc) // hc
    grp1024 = np.arange(g * p) // p
    tile16 = jnp.asarray(tabs["tile16"])
    tile64 = jnp.asarray(tabs["tile64"])
    kcat = jnp.einsum('nro,oc->nrc', kk.reshape(nl, S5_LAGS * g * hc, hc), tile16, precision=hp)
    kcat = jnp.where(jnp.asarray(np.tile(grp256, S5_LAGS)[:, None] == grp256[None, :]), kcat, 0.0)

    def rows_to_state(bb):
        m = jnp.einsum('nrp,pc->nrc', jnp.transpose(bb, (0, 1, 3, 2)).reshape(nl, g * hc, p), tile64,
                       precision=hp)
        return jnp.where(jnp.asarray(grp256[:, None] == grp1024[None, :]), m, 0.0)

    def state_to_rows(cc):
        m = jnp.einsum('nro,oc->nrc', jnp.transpose(cc, (0, 1, 3, 2)).reshape(nl, g * p, hc), tile16,
                       precision=hp)
        return jnp.where(jnp.asarray(grp1024[:, None] == grp256[None, :]), m, 0.0)

    bbar = jnp.concatenate([rows_to_state(bb_re), rows_to_state(bb_im)], axis=2)
    cbd = jnp.concatenate([state_to_rows(c_re), -state_to_rows(c_im)], axis=1)
    t_re, t_im = apow(lags + 1)
    a1_re, a1_im = apow(np.array([1]))
    al_re, al_im = apow(np.array([S5_LAGS]))
    flat = lambda z: z.reshape(nl, z.shape[1], g * p)
    return dict(skcat=_bf(kcat), sbbar=_bf(bbar), scbd=_bf(cbd),
                stre=flat(t_re), stim=flat(t_im),
                sa=jnp.concatenate([flat(a1_re), flat(a1_im), flat(al_re), flat(al_im)], axis=1))


def _const_spec(arr):
    nd = arr.ndim
    return pl.BlockSpec(arr.shape, lambda b, t, _nd=nd: (0,) * _nd)


def _layer_spec(arr, layer):
    nd = arr.ndim - 1
    return pl.BlockSpec((None,) + arr.shape[1:], lambda b, t, _nd=nd, _l=layer: (_l,) + (0,) * _nd)


def _layer_call(x, layer, stacked, const_inputs, cos, sin, tb, nb):
    bsz, seq, _ = x.shape
    grid = (bsz // nb, seq // tb)
    rr = nb * tb
    in_specs = [pl.BlockSpec((nb, tb, D_MODEL), lambda b, t: (b, t, 0)),
                pl.BlockSpec((tb, LANE), lambda b, t: (t, 0)),
                pl.BlockSpec((tb, LANE), lambda b, t: (t, 0))]
    operands = [x, cos, sin]
    order = ["gpre", "gpost", "wrot", "wmain", "wtail", "wgg", "wout",
             "hm", "hmqk", "hmg", "hs", "bd", "bdqk", "bdg", "tril", "strict", "eye", "eyer", "btri", "selc", "sperm",
             "rdall", "rqdec", "rkdec", "rcd", "rnorm",
             "dconv", "darow", "dbias", "debeta", "deg", "dnorm",
             "skcat", "sbbar", "scbd", "stre", "stim", "sa", "swglu", "sbglu",
             "gwgk", "gb", "gnorm"]
    for name in order:
        if name in stacked:
            operands.append(stacked[name])
            in_specs.append(_layer_spec(stacked[name], layer))
        else:
            operands.append(const_inputs[name])
            in_specs.append(_const_spec(const_inputs[name]))
    scratch = [
        pltpu.VMEM((rr, D_MODEL), BF16),
        pltpu.VMEM((rr, G_RET), F32),
        pltpu.VMEM((rr, G_DN), F32),
        pltpu.VMEM((rr, G_S5), F32),
        pltpu.VMEM((rr, G_GLA), F32),
        pltpu.VMEM((rr, D_MODEL), BF16),
        pltpu.VMEM((nb, tb + 8, 768), F32),
        pltpu.VMEM((rr, 768), F32),
        pltpu.VMEM((nb, tb + S5_LAGS, BRANCH_W), F32),
        pltpu.VMEM((rr, S5_LAGS * BRANCH_W), BF16),
        pltpu.VMEM((rr // S5_LAGS, 2 * S5_HALF), F32),
        pltpu.VMEM((rr // S5_LAGS, 2 * S5_HALF), F32),
        pltpu.VMEM((nb, BRANCH_W, BRANCH_W), F32),
        pltpu.VMEM((nb, BRANCH_W, BRANCH_W), F32),
        pltpu.VMEM((nb, BRANCH_W, GLA_QK), F32),
        pltpu.VMEM((2, nb, S5_HALF), F32),
    ]
    return pl.pallas_call(
        functools.partial(_layer_kernel, tb=tb, nb=nb),
        grid=grid,
        in_specs=in_specs,
        out_specs=pl.BlockSpec((nb, tb, D_MODEL), lambda b, t: (b, t, 0)),
        out_shape=jax.ShapeDtypeStruct(x.shape, x.dtype),
        scratch_shapes=scratch,
        compiler_params=pltpu.CompilerParams(
            dimension_semantics=("arbitrary", "arbitrary"),
            vmem_limit_bytes=VMEM_LIMIT_BYTES),
        name="hybrid_layer",
    )(*operands)


def _pick_tiles(bsz, seq):
    nb = 2 if bsz % 2 == 0 else 1
    for tb in (256, 128, 64):
        if seq % tb == 0:
            return tb, nb
    raise ValueError(f"sequence length {seq} must be a multiple of {CHUNK}")


def kernel(x, norm_pre, norm_post, w_in, w_out, ret_norm, dn_conv, dn_a_log, dn_dt_bias, dn_norm,
           s5_lam_re, s5_lam_im, s5_b_re, s5_b_im, s5_c_re, s5_c_im, s5_d, s5_log_dt, s5_w_glu, s5_b_glu,
           gla_w_gk, gla_b_gk, gla_norm):
    bsz, seq, dm = x.shape
    depth = w_in.shape[0]
    assert dm == D_MODEL and x.dtype == F32
    tb, nb = _pick_tiles(bsz, seq)
    tabs = _constant_tables(seq, tb, nb)
    bf_names = ("hm", "hmqk", "hmg", "hs", "bd", "btri", "selc", "sperm")
    skip = ("cos", "sin", "tile16", "tile64")
    const_inputs = {k: jnp.asarray(v, BF16 if k in bf_names else F32)
                    for k, v in tabs.items() if k not in skip}
    cos = jnp.asarray(tabs["cos"])
    sin = jnp.asarray(tabs["sin"])

    w_in_ops = _in_proj_operands(w_in)
    w_out_b = _bf(w_out)

    e_beta = np.zeros((LANE, 256), np.float32)
    e_g = np.zeros((LANE, 256), np.float32)
    for hh in range(N_HEADS):
        e_beta[SM_BETA + hh, 64 * hh:64 * hh + 64] = 1.0
        e_g[SM_A + hh, 64 * hh:64 * hh + 64] = 1.0
    e_beta = jnp.asarray(e_beta, BF16)
    e_g = jnp.asarray(e_g, BF16)

    a_rows = jnp.zeros((depth, 1, LANE), F32).at[:, 0, SM_A:SM_A + 4].set(jnp.exp(dn_a_log.astype(F32)))
    b_rows = jnp.zeros((depth, 1, LANE), F32).at[:, 0, SM_A:SM_A + 4].set(dn_dt_bias.astype(F32))
    wgk = jnp.zeros((depth, LANE, GLA_QK), BF16).at[:, 0:GLA_GATE_RANK, :].set(_bf(gla_w_gk))
    s5 = _s5_tables(s5_lam_re, s5_lam_im, s5_b_re, s5_b_im, s5_c_re, s5_c_im, s5_d, s5_log_dt, tabs)
    tile4 = lambda g: jnp.tile(g.astype(F32), (1, N_HEADS)).reshape(depth, 1, 256)
    rnorm, dnorm, gnorm = tile4(ret_norm), tile4(dn_norm), tile4(gla_norm)
    swglu = _bf(s5_w_glu)

    stacked = dict(
        gpre=norm_pre.reshape(depth, 1, dm).astype(F32),
        gpost=norm_post.reshape(depth, 1, dm).astype(F32),
        wout=w_out_b,
        rnorm=rnorm,
        dconv=dn_conv.astype(F32),
        darow=a_rows, dbias=b_rows,
        dnorm=dnorm,
        swglu=swglu, sbglu=s5_b_glu.reshape(depth, 1, 256).astype(F32),
        gwgk=wgk, gb=gla_b_gk.reshape(depth, 1, GLA_QK).astype(F32),
        gnorm=gnorm,
    )
    stacked.update(s5)
    stacked.update(w_in_ops)
    const_inputs.update(debeta=e_beta, deg=e_g)
    for i in range(depth):
        x = _layer_call(x, i, stacked, const_inputs, cos, sin, tb, nb)
    return x
```

```python
import functools
import math

import jax
import jax.numpy as jnp
import numpy as np
from jax import lax
from jax.experimental import pallas as pl
from jax.experimental.pallas import tpu as pltpu

F32 = jnp.float32
BF16 = jnp.bfloat16

D_MODEL = 1024
BRANCH_W = 256
N_HEADS = 4
HEAD_DIM = 64
EPS = 1e-6
ROPE_BASE = 10000.0
DN_CONV = 4
S5_GROUP = 16
S5_GROUPS = 16
S5_STATE = 64
S5_HALF = S5_GROUPS * S5_STATE
GLA_DK = 32
GLA_QK = 128
GLA_GATE_RANK = 16
GLA_GATE_TAU = 16.0
IN_SPLITS = [256, 256, 256, 256, 768, 4, 4, 256, 256, 256, 128, 128, 256, 16, 256]

CHUNK = 64
RET_CHUNK = 128
S5_LAGS = 8
LANE = 128
SUBLANES = 8
PROJ_COLS = 256
VMEM_LIMIT_BYTES = 56 * 1024 * 1024

_OFF = np.concatenate([[0], np.cumsum(IN_SPLITS)])
(O_RQ, O_RK, O_RV, O_RG, O_DQKV, O_DBETA, O_DA, O_DG, O_SU, O_SG,
 O_GQ, O_GK, O_GV, O_GCODE, O_GG) = (int(o) for o in _OFF[:-1])
assert O_DBETA % LANE == 0 and O_DA == O_DBETA + 4 and O_DG == O_DA + 4
_T = lambda o: o - O_DG
PROJ_GROUPS = {
    "retention": [("rot", 0, 256), ("rot", 256, 256), ("main", O_RV, 256), ("main", O_RG, 256)],
    "deltanet": [("main", O_DQKV, 256), ("main", O_DQKV + 256, 256), ("main", O_DQKV + 512, 256),
                 ("main", O_DBETA, LANE), ("tail", _T(O_DG), 256)],
    "s5": [("tail", _T(O_SU), 256), ("tail", _T(O_SG), 256)],
    "gla": [("tail", _T(O_GQ), 256), ("tail", _T(O_GV), 256), ("tail", _T(O_GCODE), LANE), ("gg", 0, 256)],
}
G_RET, G_DN, G_S5, G_GLA = (sum(w for _, _, w in PROJ_GROUPS[k]) for k in ("retention", "deltanet", "s5", "gla"))
TAIL_COLS = _T(O_GCODE) + LANE
R_Q, R_K, R_V, R_G = 0, 256, 512, 768
D_QKV, D_SMALL, D_G = 0, 768, 896
S_U, S_G = 0, 256
L_Q, L_K, L_V, L_CODE, L_G = 0, 128, 256, 512, 640
SM_BETA, SM_A = 0, 4


def _bf(x):
    return x.astype(BF16)


def _dot(a, b):
    return jnp.dot(a, b, preferred_element_type=F32)


def _dot_nt(a, b):
    return lax.dot_general(a, b, (((1,), (1,)), ((), ())), preferred_element_type=F32)


def _dot_tn(a, b):
    return lax.dot_general(a, b, (((0,), (0,)), ((), ())), preferred_element_type=F32)


def _split2(x):
    x1 = _bf(x)
    x2 = _bf(x - x1.astype(F32))
    return x1, x2


def _dot_x2(x, m):
    x1, x2 = _split2(x)
    return _dot(x1, m) + _dot(x2, m)


def _dot_m2(m, x):
    x1, x2 = _split2(x)
    return _dot(m, x1) + _dot(m, x2)


def _sigmoid(x):
    return 1.0 / (1.0 + jnp.exp(-x))


def _silu(x):
    return x * _sigmoid(x)


def _softplus(x):
    return jnp.maximum(x, 0.0) + jnp.log(1.0 + jnp.exp(-jnp.abs(x)))


def _stack_heads(x, hm_ref):
    xb = _bf(x)
    rows = x.shape[0]
    return jnp.concatenate([xb * hm_ref[h, 0:rows, :] for h in range(N_HEADS)], axis=0)


def _head_rmsnorm_gate(o, g_row, gate, hs):
    ms = _dot(_bf(o * o), hs) * (1.0 / HEAD_DIM)
    return o * lax.rsqrt(ms + EPS) * g_row * _silu(gate)


def _s5_groups(tb, nb):
    n_grp = max(1, SUBLANES // nb)
    n_s5 = tb // S5_LAGS
    assert n_s5 % n_grp == 0
    return n_grp, n_s5 // n_grp


def _interleave(feeder, consumers):
    live = []
    feeding = True
    while feeding or live:
        if feeding:
            try:
                tag = next(feeder)
                while tag is not None:
                    live.append(consumers.pop(tag))
                    tag = next(feeder)
            except StopIteration:
                feeding = False
        for g in list(live):
            try:
                next(g)
            except StopIteration:
                live.remove(g)
    assert not consumers


def _layer_kernel(
        x_ref, cos_ref, sin_ref,
        gpost_ref, wrot_ref, wmain_ref, wtail_ref, wgg_ref, wout_ref,
        hm_ref, hmqk_ref, hmg_ref, hs_ref, bd_ref, bdqk_ref, bdg_ref,
        tril_ref, strict_ref, eye_ref, eyer_ref, btri_ref, selc_ref, sperm_ref,
        rdall_ref, rqdec_ref, rkdec_ref, rcd_ref, rnorm_ref,
        dconv_ref, darow_ref, dbias_ref, debeta_ref, deg_ref, dnorm_ref,
        skcat_ref, sbbar_ref, scbd_ref, stre_ref, stim_ref, sa_ref, scar_ref,
        swglu_ref, sbglu_ref,
        gwgk_ref, gb_ref, gnorm_ref,
        out_ref,
        h_ref, pr_ref, pd_ref, ps_ref, pg_ref, y_ref, xc_ref, qkv_ref, us_ref, ucat_ref,
        sret_ref, sdn_ref, sgla_ref, ss5_ref,
        *, tb, nb):
    t_idx = pl.program_id(1)
    n_chunks = tb // CHUNK
    n_s5 = tb // S5_LAGS
    n_grp, n_stp = _s5_groups(tb, nb)
    batches = range(nb)
    rr = nb * tb

    def rb(b):
        return slice(b * tb, (b + 1) * tb)

    units = [(b, c, slice(b * tb + c * CHUNK, b * tb + (c + 1) * CHUNK))
             for c in range(n_chunks) for b in batches]
    n_units = len(units)

    @pl.when(t_idx == 0)
    def _reset():
        sret_ref[...] = jnp.zeros_like(sret_ref)
        sdn_ref[...] = jnp.zeros_like(sdn_ref)
        sgla_ref[...] = jnp.zeros_like(sgla_ref)
        ss5_ref[...] = jnp.zeros_like(ss5_ref)
        for b in batches:
            xc_ref[b, 0:8, :] = jnp.zeros((8, 768), F32)
            us_ref[b, 0:S5_LAGS, :] = jnp.zeros((S5_LAGS, BRANCH_W), F32)

    for b in batches:
        x = x_ref[b]
        h_ref[rb(b), :] = _bf(x * lax.rsqrt(jnp.mean(x * x, axis=-1, keepdims=True) + EPS))

    def projections():
        weights = {"rot": wrot_ref, "main": wmain_ref, "tail": wtail_ref, "gg": wgg_ref}
        for name, dst_ref in (("deltanet", pd_ref), ("s5", ps_ref), ("gla", pg_ref), ("retention", pr_ref)):
            c0 = 0
            for operand, first, width in PROJ_GROUPS[name]:
                dst_ref[:, c0:c0 + width] = _dot(h_ref[...], weights[operand][:, first:first + width])
                c0 += width
                yield None
            yield name

    o_partial = []

    def out_project(branch):
        rows = slice(branch * BRANCH_W, (branch + 1) * BRANCH_W)
        o_partial.append(_dot(y_ref[:, rows], wout_ref[rows, :]))

    hs = hs_ref[...]
    bd = bd_ref[...]
    bd32 = bd.astype(F32)

    def retention():
        rc = min(RET_CHUNK, tb)
        n_rc = tb // rc
        runits = [(b, c, slice(b * tb + c * rc, b * tb + (c + 1) * rc)) for c in range(n_rc) for b in batches]
        ret_q, ret_k, ret_v = {}, {}, {}
        for b, c, sl in runits:
            cs = cos_ref[c * rc:(c + 1) * rc, :]
            sn = sin_ref[c * rc:(c + 1) * rc, :]
            qa = pr_ref[sl, R_Q:R_Q + LANE]
            qb = pr_ref[sl, R_Q + LANE:R_Q + 2 * LANE]
            ka = pr_ref[sl, R_K:R_K + LANE]
            kb = pr_ref[sl, R_K + LANE:R_K + 2 * LANE]
            ret_q[b, c] = jnp.concatenate([qa * cs - qb * sn, qa * sn + qb * cs], axis=1)
            ret_k[b, c] = jnp.concatenate([ka * cs - kb * sn, ka * sn + kb * cs], axis=1)
            ret_v[b, c] = pr_ref[sl, R_V:R_V + BRANCH_W]
            if b == nb - 1:
                yield
        o_intra, kv = {}, {}
        for b, c, sl in runits:
            s_all = _dot_nt(_bf(ret_q[b, c]), _stack_heads(ret_k[b, c], hmqk_ref))
            o_intra[b, c] = _dot(_bf(s_all * rdall_ref[...]), _stack_heads(ret_v[b, c], hm_ref))
            kv[b, c] = bdqk_ref[...] * _dot_tn(_bf(ret_v[b, c]), _bf(ret_k[b, c] * rkdec_ref[...]))
            if b == nb - 1:
                yield
        st = [sret_ref[b] for b in batches]
        o_ret = {}
        for b, c, sl in runits:
            o_ret[b, c] = o_intra[b, c] + _dot_nt(_bf(ret_q[b, c] * rqdec_ref[...]), _bf(st[b]))
            st[b] = st[b] * rcd_ref[...] + kv[b, c]
            if b == nb - 1:
                yield
        for b in batches:
            sret_ref[b] = st[b]
        o = jnp.concatenate([o_ret[b, c] for b in batches for c in range(n_rc)], axis=0)
        y_ref[:, 0:BRANCH_W] = _bf(_head_rmsnorm_gate(o, rnorm_ref[...], pr_ref[:, R_G:R_G + BRANCH_W], hs))
        yield
        out_project(0)

    def deltanet():
        for b in batches:
            xc_ref[b, 8:8 + tb, :] = pd_ref[rb(b), D_QKV:D_QKV + 768]
            conv = dconv_ref[DN_CONV - 1:DN_CONV, :] * xc_ref[b, 8:8 + tb, :]
            for i in range(DN_CONV - 1):
                conv = conv + dconv_ref[i:i + 1, :] * xc_ref[b, 5 + i:5 + i + tb, :]
            qkv_ref[rb(b), :] = _silu(conv)
            xc_ref[b, 0:8, :] = xc_ref[b, tb:tb + 8, :]
            yield

        q = qkv_ref[:, 0:256]
        k = qkv_ref[:, 256:512]
        v = qkv_ref[:, 512:768]
        ss = _dot(_bf(jnp.concatenate([q * q, k * k], axis=0)), hs)
        qn = q * lax.rsqrt(ss[0:rr] + EPS) * (HEAD_DIM ** -0.5)
        kn = k * lax.rsqrt(ss[rr:2 * rr] + EPS)
        yield
        small = pd_ref[:, D_SMALL:D_SMALL + LANE]
        beta = _dot(_bf(_sigmoid(small)), debeta_ref[...])
        g_s = -darow_ref[...] * _softplus(small + dbias_ref[...])
        gcum_s = jnp.concatenate([_dot_m2(btri_ref[...], g_s[rb(b)]) for b in batches], axis=0)
        yield
        gcum = _dot_x2(gcum_s, deg_ref[...])
        grows = [_dot_m2(selc_ref[...], gcum[rb(b)] * eyer_ref[...]) for b in batches]
        yield
        egc = jnp.exp(gcum)
        kbeta = kn * beta
        vbeta = v * beta
        kbe = kbeta * egc
        qg = qn * egc
        yield

        dec, glast, kgt = [], [], []
        for b, c, sl in units:
            gc = gcum[sl]
            dec.append(jnp.exp(jnp.minimum(gc - grows[b][c:c + 1, :], 0.0)))
            glast.append(gc[CHUNK - 1:CHUNK, :])
            kgt.append(_bf((kn[sl] * jnp.exp(glast[-1] - gc)).T))
            if b == nb - 1:
                yield
        aa = []
        for b, _, sl in units:
            aa.append(_dot_nt(_bf(jnp.concatenate([kbeta[sl], qn[sl]], axis=0)), _stack_heads(kn[sl], hm_ref)))
            if b == nb - 1:
                yield
        attn = [aa[u][CHUNK:2 * CHUNK] * dec[u] * tril_ref[...] for u in range(n_units)]

        def blockdiag(m):
            mb = _bf(m)
            return jnp.concatenate([mb, mb, mb, mb], axis=0) * bd

        pw = [-(aa[u][0:CHUNK] * dec[u] * strict_ref[...]) for u in range(n_units)]
        t_all = [eye_ref[...] + pw[u] for u in range(n_units)]
        yield
        pw = [_dot(_bf(pw[u]), blockdiag(pw[u])) for u in range(n_units)]
        yield
        for level in range(1, 5):
            both = [_dot(_bf(jnp.concatenate([pw[u], t_all[u]], axis=0)), blockdiag(pw[u]))
                    for u in range(n_units)]
            pw = [both[u][0:CHUNK] for u in range(n_units)]
            t_all = [t_all[u] + both[u][CHUNK:2 * CHUNK] for u in range(n_units)]
            yield
        t_all = [t_all[u] + _dot(_bf(t_all[u]), blockdiag(pw[u])) for u in range(n_units)]
        yield
        uw = []
        for u, (b, _, sl) in enumerate(units):
            uw.append(_dot(_bf(t_all[u]), jnp.concatenate([_stack_heads(vbeta[sl], hm_ref),
                                                           _stack_heads(kbe[sl], hm_ref)], axis=1)))
            if b == nb - 1:
                yield

        st = [sdn_ref[b] for b in batches]
        o_parts = {}
        for u, (b, c, sl) in enumerate(units):
            wq = _dot(_bf(jnp.concatenate([uw[u][:, 256:512], qg[sl]], axis=0)), _bf(st[b]))
            v_new = uw[u][:, 0:256] - wq[0:CHUNK]
            o_parts[(b, c)] = wq[CHUNK:2 * CHUNK] + _dot(_bf(attn[u]), _stack_heads(v_new, hm_ref))
            st[b] = st[b] * jnp.exp(glast[u]) + bd32 * _dot(kgt[u], _bf(v_new))
            if b == nb - 1:
                yield
        for b in batches:
            sdn_ref[b] = st[b]
        o = jnp.concatenate([o_parts[(b, c)] for b in batches for c in range(n_chunks)], axis=0)
        y_ref[:, BRANCH_W:2 * BRANCH_W] = _bf(
            _head_rmsnorm_gate(o, dnorm_ref[...], pd_ref[:, D_G:D_G + BRANCH_W], hs))
        yield
        out_project(1)

    def s5():
        u_in = ps_ref[:, S_U:S_U + BRANCH_W]
        row_in_chunk = lax.broadcasted_iota(jnp.int32, (tb, BRANCH_W), 0) % S5_LAGS
        ucat_ref[:, 0:BRANCH_W] = _bf(u_in)
        for b in batches:
            us_ref[b, S5_LAGS:S5_LAGS + tb, :] = u_in[rb(b)]
            for tau in range(1, S5_LAGS):
                shifted = us_ref[b, S5_LAGS - tau:S5_LAGS - tau + tb, :]
                ucat_ref[rb(b), tau * BRANCH_W:(tau + 1) * BRANCH_W] = _bf(
                    jnp.where(row_in_chunk >= tau, shifted, 0.0))
            us_ref[b, 0:S5_LAGS, :] = us_ref[b, tb:tb + S5_LAGS, :]
            yield
        u_perm = _bf(_dot(sperm_ref[...], _bf(u_in)))
        xx = _dot(u_perm, sbbar_ref[...])
        yield
        a1_re = sa_ref[0:1, :]
        a1_im = sa_ref[1:2, :]
        n_cb = nb * n_s5
        z_re = xx[0:n_cb, 0:S5_HALF]
        z_im = xx[0:n_cb, S5_HALF:2 * S5_HALF]
        for j in range(1, S5_LAGS):
            xj_re = xx[j * n_cb:(j + 1) * n_cb, 0:S5_HALF]
            xj_im = xx[j * n_cb:(j + 1) * n_cb, S5_HALF:2 * S5_HALF]
            z_re, z_im = a1_re * z_re - a1_im * z_im + xj_re, a1_re * z_im + a1_im * z_re + xj_im
            if j % 2 == 1:
                yield
        yield
        y_s5 = jnp.concatenate([_dot(ucat_ref[rb(b), :], skcat_ref[...]) for b in batches], axis=0)
        yield
        grp_rows = n_grp * nb
        al_re = sa_ref[2:3, :]
        al_im = sa_ref[3:4, :]
        s_re = jnp.zeros((grp_rows, S5_HALF), F32)
        s_im = jnp.zeros((grp_rows, S5_HALF), F32)
        loc_re, loc_im = [], []
        for m in range(n_stp):
            loc_re.append(s_re)
            loc_im.append(s_im)
            zm_re = z_re[m * grp_rows:(m + 1) * grp_rows]
            zm_im = z_im[m * grp_rows:(m + 1) * grp_rows]
            s_re, s_im = al_re * s_re - al_im * s_im + zm_re, al_re * s_im + al_im * s_re + zm_im
            if m % 2 == 1:
                yield
        ag_re = scar_ref[2 * n_stp:2 * n_stp + 1, :]
        ag_im = scar_ref[2 * n_stp + 1:2 * n_stp + 2, :]
        c_re = ss5_ref[0]
        c_im = ss5_ref[1]
        car_re, car_im = [], []
        for g in range(n_grp):
            car_re.append(c_re)
            car_im.append(c_im)
            e_re = s_re[g * nb:(g + 1) * nb]
            e_im = s_im[g * nb:(g + 1) * nb]
            c_re, c_im = ag_re * c_re - ag_im * c_im + e_re, ag_re * c_im + ag_im * c_re + e_im
        ss5_ref[0] = c_re
        ss5_ref[1] = c_im
        yield
        car_re = jnp.concatenate(car_re, axis=0)
        car_im = jnp.concatenate(car_im, axis=0)
        sp_re, sp_im = [], []
        for m in range(n_stp):
            p_re = scar_ref[m:m + 1, :]
            p_im = scar_ref[n_stp + m:n_stp + m + 1, :]
            sp_re.append(loc_re[m] + p_re * car_re - p_im * car_im)
            sp_im.append(loc_im[m] + p_re * car_im + p_im * car_re)
        yield
        sp_re = jnp.concatenate(sp_re, axis=0)[:, None, :]
        sp_im = jnp.concatenate(sp_im, axis=0)[:, None, :]
        t_re = stre_ref[...][None]
        t_im = stim_ref[...][None]

        def batch_major(v):
            tiles = []
            for b in batches:
                for n in range(n_s5):
                    g, m = divmod(n, n_stp)
                    src = (m * n_grp + g) * nb + b
                    tiles.append(v[src * S5_LAGS:(src + 1) * S5_LAGS])
            return jnp.concatenate(tiles, axis=0)

        v_re = batch_major((t_re * sp_re - t_im * sp_im).reshape(rr, S5_HALF))
        yield
        v_im = batch_major((t_re * sp_im + t_im * sp_re).reshape(rr, S5_HALF))
        yield
        vv = _bf(jnp.concatenate([v_re, v_im], axis=1))
        y_s5 = y_s5 + jnp.concatenate([_dot(vv[rb(b)], scbd_ref[...]) for b in batches], axis=0)
        yield
        c0 = math.sqrt(2.0 / math.pi)
        y_s5 = 0.5 * y_s5 * (1.0 + jnp.tanh(c0 * (y_s5 + 0.044715 * (y_s5 * y_s5 * y_s5))))
        y_s5 = y_s5 * _sigmoid(_dot(_bf(y_s5), swglu_ref[...]) + sbglu_ref[...])
        y_ref[:, 2 * BRANCH_W:3 * BRANCH_W] = _bf(y_s5 * _silu(ps_ref[:, S_G:S_G + BRANCH_W]))
        yield
        out_project(2)

    def gla():
        z = _dot(_bf(pg_ref[:, L_CODE:L_CODE + LANE]), gwgk_ref[...]) + gb_ref[...]
        gk = -_softplus(-z) * (1.0 / GLA_GATE_TAU)
        yield
        cum = jnp.concatenate([_dot_m2(btri_ref[...], gk[rb(b)]) for b in batches], axis=0)
        yield
        q = pg_ref[:, L_Q:L_Q + GLA_QK]
        k = pg_ref[:, L_K:L_K + GLA_QK]
        v = pg_ref[:, L_V:L_V + BRANCH_W]
        qt = _bf(q * jnp.exp(cum) * (GLA_DK ** -0.5))
        kt = k * jnp.exp(-cum)
        clast = [cum[sl][CHUNK - 1:CHUNK, :] for _, _, sl in units]
        yield
        s_all = []
        for b, _, sl in units:
            s_all.append(_dot_nt(qt[sl], _stack_heads(kt[sl], hmg_ref)))
            if b == nb - 1:
                yield
        o_intra = []
        for u, (b, _, sl) in enumerate(units):
            o_intra.append(_dot(_bf(s_all[u] * tril_ref[...]), _stack_heads(v[sl], hm_ref)))
            if b == nb - 1:
                yield
        kv = []
        for u, (b, _, sl) in enumerate(units):
            kv.append(bdg_ref[...] * _dot_tn(_bf(v[sl]), _bf(k[sl] * jnp.exp(clast[u] - cum[sl]))))
            if b == nb - 1:
                yield
        st = [sgla_ref[b] for b in batches]
        o_parts = {}
        for u, (b, c, sl) in enumerate(units):
            o_parts[(b, c)] = o_intra[u] + _dot_nt(qt[sl], _bf(st[b]))
            st[b] = st[b] * jnp.exp(clast[u]) + kv[u]
            if b == nb - 1:
                yield
        for b in batches:
            sgla_ref[b] = st[b]
        o = jnp.concatenate([o_parts[(b, c)] for b in batches for c in range(n_chunks)], axis=0)
        y_ref[:, 3 * BRANCH_W:4 * BRANCH_W] = _bf(
            _head_rmsnorm_gate(o, gnorm_ref[...], pg_ref[:, L_G:L_G + BRANCH_W], hs))
        yield
        out_project(3)

    _interleave(projections(), {"s5": s5(), "deltanet": deltanet(), "retention": retention(), "gla": gla()})

    o = o_partial[0] + o_partial[1] + o_partial[2] + o_partial[3]
    o = o * lax.rsqrt(jnp.mean(o * o, axis=-1, keepdims=True) + EPS) * gpost_ref[...]
    for b in batches:
        out_ref[b] = x_ref[b] + o[rb(b)]


def _in_proj_operands(w_in, norm_pre):
    dp, dm, _ = w_in.shape
    w_in = w_in.astype(F32) * norm_pre.astype(F32)[:, :, None]
    main = _bf(w_in)

    def deinterleave(o):
        t = w_in[:, :, o:o + 256].reshape(dp, dm, N_HEADS, HEAD_DIM // 2, 2)
        return jnp.transpose(t, (0, 1, 4, 2, 3)).reshape(dp, dm, 256)

    rot = _bf(jnp.concatenate([deinterleave(O_RQ) * (HEAD_DIM ** -0.5), deinterleave(O_RK)], axis=2))
    return dict(wrot=rot, wmain=main, wtail=main[:, :, O_DG:O_DG + TAIL_COLS], wgg=main[:, :, O_GG:O_GG + 256])


@functools.lru_cache(maxsize=None)
def _constant_tables(seq_len, tb, nb):
    c = CHUNK
    lane = np.arange(256)
    head_std = lane // 64
    head_qk = (lane % 128) // 32
    head_g = np.arange(128) // 32
    i = np.arange(c)[:, None]
    j = np.arange(256)[None, :] % 64
    t = {}
    heads = np.arange(4)[:, None, None]
    t["hm"] = np.broadcast_to(head_std[None, None, :] == heads, (4, tb, 256)).astype(np.float32)
    t["hmqk"] = np.broadcast_to(head_qk[None, None, :] == heads, (4, tb, 256)).astype(np.float32)
    t["hmg"] = np.broadcast_to(head_g[None, None, :] == heads, (4, c, 128)).astype(np.float32)
    t["hs"] = (head_std[:, None] == head_std[None, :]).astype(np.float32)
    t["bd"] = t["hs"]
    t["bdqk"] = (head_std[:, None] == head_qk[None, :]).astype(np.float32)
    t["bdg"] = (head_std[:, None] == head_g[None, :]).astype(np.float32)
    t["tril"] = (i >= j).astype(np.float32)
    t["strict"] = (i > j).astype(np.float32)
    t["eye"] = (i == j).astype(np.float32)
    r = np.arange(tb)
    same_chunk = (r[:, None] // c) == (r[None, :] // c)
    t["btri"] = (same_chunk & (r[:, None] >= r[None, :])).astype(np.float32)
    t["selc"] = ((r[None, :] // c) == np.arange(16)[:, None]).astype(np.float32)
    n_grp, n_stp = _s5_groups(tb, nb)
    jj, mm, gg, bb = np.meshgrid(np.arange(S5_LAGS), np.arange(n_stp), np.arange(n_grp), np.arange(nb),
                                 indexing="ij")
    old = (bb * tb + (gg * n_stp + mm) * S5_LAGS + jj).reshape(-1)
    sperm = np.zeros((nb * tb, nb * tb), np.float32)
    sperm[np.arange(nb * tb), old] = 1.0
    t["sperm"] = sperm
    t["eyer"] = np.tile(t["eye"], (tb // c, 1))
    rc = min(RET_CHUNK, tb)
    lg = np.log(1.0 - 2.0 ** (-5.0 - np.arange(4, dtype=np.float64)))
    ri = np.arange(rc)[:, None]
    rj = np.arange(4 * rc)[None, :] % rc
    lg_cols = lg[np.arange(4 * rc) // rc][None, :]
    t["rdall"] = np.where(ri >= rj, np.exp(lg_cols * np.where(ri >= rj, ri - rj, 0)), 0.0).astype(np.float32)
    lg_qk = lg[head_qk][None, :]
    t["rqdec"] = np.exp(lg_qk * (ri + 1.0)).astype(np.float32)
    t["rkdec"] = np.exp(lg_qk * (rc - 1.0 - ri)).astype(np.float32)
    t["rcd"] = np.exp(lg_qk * rc).astype(np.float32)
    inv = ROPE_BASE ** (-np.arange(0, HEAD_DIM, 2, dtype=np.float64) / HEAD_DIM)
    ang = np.arange(seq_len, dtype=np.float64)[:, None] * inv[None, :]
    t["cos"] = np.tile(np.cos(ang), (1, 4)).astype(np.float32)
    t["sin"] = np.tile(np.sin(ang), (1, 4)).astype(np.float32)
    t["tile16"] = np.tile(np.eye(S5_GROUP, dtype=np.float32), (1, S5_GROUPS))
    t["tile64"] = np.tile(np.eye(S5_STATE, dtype=np.float32), (1, S5_GROUPS))
    return t


def _s5_tables(lam_re, lam_im, b_re, b_im, c_re, c_im, d, log_dt, tabs, n_stp):
    hp = lax.Precision.HIGHEST
    g, p, hc = S5_GROUPS, S5_STATE, S5_GROUP
    nl = lam_re.shape[0]
    lam_re, lam_im = lam_re.astype(F32), lam_im.astype(F32)
    dt = jnp.exp(log_dt.astype(F32))[..., None]
    mag = jnp.exp(lam_re * dt)
    ang = lam_im * dt
    a_re, a_im = mag * jnp.cos(ang), mag * jnp.sin(ang)
    den = lam_re * lam_re + lam_im * lam_im
    nr, ni = a_re - 1.0, a_im
    coef_re = (nr * lam_re + ni * lam_im) / den
    coef_im = (ni * lam_re - nr * lam_im) / den
    b_re, b_im = b_re.astype(F32), b_im.astype(F32)
    bb_re = coef_re[..., None] * b_re - coef_im[..., None] * b_im
    bb_im = coef_re[..., None] * b_im + coef_im[..., None] * b_re

    def apow(n):
        n = jnp.asarray(n, F32)[None, :, None, None]
        m = jnp.exp((lam_re * dt)[:, None] * n)
        return m * jnp.cos(ang[:, None] * n), m * jnp.sin(ang[:, None] * n)

    lags = np.arange(S5_LAGS)
    p_re, p_im = apow(lags)
    c_re, c_im = c_re.astype(F32), c_im.astype(F32)
    ab_re = p_re[..., None] * bb_re[:, None] - p_im[..., None] * bb_im[:, None]
    ab_im = p_re[..., None] * bb_im[:, None] + p_im[..., None] * bb_re[:, None]
    kk = (jnp.einsum('ntgpi,ngop->ntgio', ab_re, c_re, precision=hp)
          - jnp.einsum('ntgpi,ngop->ntgio', ab_im, c_im, precision=hp))
    kk = kk.at[:, 0].add(d.astype(F32)[..., None] * jnp.eye(hc, dtype=F32))
    grp256 = np.arange(g * hc) // hc
    grp1024 = np.arange(g * p) // p
    tile16 = jnp.asarray(tabs["tile16"])
    tile64 = jnp.asarray(tabs["tile64"])
    kcat = jnp.einsum('nro,oc->nrc', kk.reshape(nl, S5_LAGS * g * hc, hc), tile16, precision=hp)
    kcat = jnp.where(jnp.asarray(np.tile(grp256, S5_LAGS)[:, None] == grp256[None, :]), kcat, 0.0)

    def rows_to_state(bb):
        m = jnp.einsum('nrp,pc->nrc', jnp.transpose(bb, (0, 1, 3, 2)).reshape(nl, g * hc, p), tile64,
                       precision=hp)
        return jnp.where(jnp.asarray(grp256[:, None] == grp1024[None, :]), m, 0.0)

    def state_to_rows(cc):
        m = jnp.einsum('nro,oc->nrc', jnp.transpose(cc, (0, 1, 3, 2)).reshape(nl, g * p, hc), tile16,
                       precision=hp)
        return jnp.where(jnp.asarray(grp1024[:, None] == grp256[None, :]), m, 0.0)

    bbar = jnp.concatenate([rows_to_state(bb_re), rows_to_state(bb_im)], axis=2)
    cbd = jnp.concatenate([state_to_rows(c_re), -state_to_rows(c_im)], axis=1)
    t_re, t_im = apow(lags + 1)
    a1_re, a1_im = apow(np.array([1]))
    al_re, al_im = apow(np.array([S5_LAGS]))
    pm_re, pm_im = apow(S5_LAGS * np.arange(n_stp))
    ag_re, ag_im = apow(np.array([S5_LAGS * n_stp]))
    flat = lambda z: z.reshape(nl, z.shape[1], g * p)
    return dict(skcat=_bf(kcat), sbbar=_bf(bbar), scbd=_bf(cbd),
                stre=flat(t_re), stim=flat(t_im),
                sa=jnp.concatenate([flat(a1_re), flat(a1_im), flat(al_re), flat(al_im)], axis=1),
                scar=jnp.concatenate([flat(pm_re), flat(pm_im), flat(ag_re), flat(ag_im)], axis=1))


def _const_spec(arr):
    nd = arr.ndim
    return pl.BlockSpec(arr.shape, lambda b, t, _nd=nd: (0,) * _nd)


def _layer_spec(arr, layer):
    nd = arr.ndim - 1
    return pl.BlockSpec((None,) + arr.shape[1:], lambda b, t, _nd=nd, _l=layer: (_l,) + (0,) * _nd)


def _layer_call(x, layer, stacked, const_inputs, cos, sin, tb, nb):
    bsz, seq, _ = x.shape
    grid = (bsz // nb, seq // tb)
    rr = nb * tb
    in_specs = [pl.BlockSpec((nb, tb, D_MODEL), lambda b, t: (b, t, 0)),
                pl.BlockSpec((tb, LANE), lambda b, t: (t, 0)),
                pl.BlockSpec((tb, LANE), lambda b, t: (t, 0))]
    operands = [x, cos, sin]
    order = ["gpost", "wrot", "wmain", "wtail", "wgg", "wout",
             "hm", "hmqk", "hmg", "hs", "bd", "bdqk", "bdg", "tril", "strict", "eye", "eyer", "btri", "selc", "sperm",
             "rdall", "rqdec", "rkdec", "rcd", "rnorm",
             "dconv", "darow", "dbias", "debeta", "deg", "dnorm",
             "skcat", "sbbar", "scbd", "stre", "stim", "sa", "scar", "swglu", "sbglu",
             "gwgk", "gb", "gnorm"]
    for name in order:
        if name in stacked:
            operands.append(stacked[name])
            in_specs.append(_layer_spec(stacked[name], layer))
        else:
            operands.append(const_inputs[name])
            in_specs.append(_const_spec(const_inputs[name]))
    scratch = [
        pltpu.VMEM((rr, D_MODEL), BF16),
        pltpu.VMEM((rr, G_RET), F32),
        pltpu.VMEM((rr, G_DN), F32),
        pltpu.VMEM((rr, G_S5), F32),
        pltpu.VMEM((rr, G_GLA), F32),
        pltpu.VMEM((rr, D_MODEL), BF16),
        pltpu.VMEM((nb, tb + 8, 768), F32),
        pltpu.VMEM((rr, 768), F32),
        pltpu.VMEM((nb, tb + S5_LAGS, BRANCH_W), F32),
        pltpu.VMEM((rr, S5_LAGS * BRANCH_W), BF16),
        pltpu.VMEM((nb, BRANCH_W, BRANCH_W), F32),
        pltpu.VMEM((nb, BRANCH_W, BRANCH_W), F32),
        pltpu.VMEM((nb, BRANCH_W, GLA_QK), F32),
        pltpu.VMEM((2, nb, S5_HALF), F32),
    ]
    return pl.pallas_call(
        functools.partial(_layer_kernel, tb=tb, nb=nb),
        grid=grid,
        in_specs=in_specs,
        out_specs=pl.BlockSpec((nb, tb, D_MODEL), lambda b, t: (b, t, 0)),
        out_shape=jax.ShapeDtypeStruct(x.shape, x.dtype),
        scratch_shapes=scratch,
        compiler_params=pltpu.CompilerParams(
            dimension_semantics=("arbitrary", "arbitrary"),
            vmem_limit_bytes=VMEM_LIMIT_BYTES),
        name="hybrid_layer",
    )(*operands)


def _pick_tiles(bsz, seq):
    nb = 2 if bsz % 2 == 0 else 1
    for tb in (256, 128, 64):
        if seq % tb == 0:
            return tb, nb
    raise ValueError(f"sequence length {seq} must be a multiple of {CHUNK}")


def kernel(x, norm_pre, norm_post, w_in, w_out, ret_norm, dn_conv, dn_a_log, dn_dt_bias, dn_norm,
           s5_lam_re, s5_lam_im, s5_b_re, s5_b_im, s5_c_re, s5_c_im, s5_d, s5_log_dt, s5_w_glu, s5_b_glu,
           gla_w_gk, gla_b_gk, gla_norm):
    bsz, seq, dm = x.shape
    depth = w_in.shape[0]
    assert dm == D_MODEL and x.dtype == F32
    tb, nb = _pick_tiles(bsz, seq)
    tabs = _constant_tables(seq, tb, nb)
    bf_names = ("hm", "hmqk", "hmg", "hs", "bd", "btri", "selc", "sperm")
    skip = ("cos", "sin", "tile16", "tile64")
    const_inputs = {k: jnp.asarray(v, BF16 if k in bf_names else F32)
                    for k, v in tabs.items() if k not in skip}
    cos = jnp.asarray(tabs["cos"])
    sin = jnp.asarray(tabs["sin"])

    w_in_ops = _in_proj_operands(w_in, norm_pre)
    w_out_b = _bf(w_out)

    e_beta = np.zeros((LANE, 256), np.float32)
    e_g = np.zeros((LANE, 256), np.float32)
    for hh in range(N_HEADS):
        e_beta[SM_BETA + hh, 64 * hh:64 * hh + 64] = 1.0
        e_g[SM_A + hh, 64 * hh:64 * hh + 64] = 1.0
    e_beta = jnp.asarray(e_beta, BF16)
    e_g = jnp.asarray(e_g, BF16)

    a_rows = jnp.zeros((depth, 1, LANE), F32).at[:, 0, SM_A:SM_A + 4].set(jnp.exp(dn_a_log.astype(F32)))
    b_rows = jnp.zeros((depth, 1, LANE), F32).at[:, 0, SM_A:SM_A + 4].set(dn_dt_bias.astype(F32))
    wgk = jnp.zeros((depth, LANE, GLA_QK), BF16).at[:, 0:GLA_GATE_RANK, :].set(_bf(gla_w_gk))
    s5 = _s5_tables(s5_lam_re, s5_lam_im, s5_b_re, s5_b_im, s5_c_re, s5_c_im, s5_d, s5_log_dt, tabs,
                    _s5_groups(tb, nb)[1])
    tile4 = lambda g: jnp.tile(g.astype(F32), (1, N_HEADS)).reshape(depth, 1, 256)
    rnorm, dnorm, gnorm = tile4(ret_norm), tile4(dn_norm), tile4(gla_norm)
    swglu = _bf(s5_w_glu)

    stacked = dict(
        gpost=norm_post.reshape(depth, 1, dm).astype(F32),
        wout=w_out_b,
        rnorm=rnorm,
        dconv=dn_conv.astype(F32),
        darow=a_rows, dbias=b_rows,
        dnorm=dnorm,
        swglu=swglu, sbglu=s5_b_glu.reshape(depth, 1, 256).astype(F32),
        gwgk=wgk, gb=gla_b_gk.reshape(depth, 1, GLA_QK).astype(F32),
        gnorm=gnorm,
    )
    stacked.update(s5)
    stacked.update(w_in_ops)
    const_inputs.update(debeta=e_beta, deg=e_g)
    for i in range(depth):
        x = _layer_call(x, i, stacked, const_inputs, cos, sin, tb, nb)
    return x
```

```python
import functools
import math

import jax
import jax.numpy as jnp
import numpy as np
from jax import lax
from jax.experimental import pallas as pl
from jax.experimental.pallas import tpu as pltpu

F32 = jnp.float32
BF16 = jnp.bfloat16

D_MODEL = 1024
BRANCH_W = 256
N_HEADS = 4
HEAD_DIM = 64
EPS = 1e-6
ROPE_BASE = 10000.0
DN_CONV = 4
S5_GROUP = 16
S5_GROUPS = 16
S5_STATE = 64
S5_HALF = S5_GROUPS * S5_STATE
GLA_DK = 32
GLA_QK = 128
GLA_GATE_RANK = 16
GLA_GATE_TAU = 16.0
IN_SPLITS = [256, 256, 256, 256, 768, 4, 4, 256, 256, 256, 128, 128, 256, 16, 256]

CHUNK = 64
RET_CHUNK = 128
S5_LAGS = 8
LANE = 128
SUBLANES = 8
PROJ_COLS = 256
VMEM_LIMIT_BYTES = 56 * 1024 * 1024

_OFF = np.concatenate([[0], np.cumsum(IN_SPLITS)])
(O_RQ, O_RK, O_RV, O_RG, O_DQKV, O_DBETA, O_DA, O_DG, O_SU, O_SG,
 O_GQ, O_GK, O_GV, O_GCODE, O_GG) = (int(o) for o in _OFF[:-1])
assert O_DBETA % LANE == 0 and O_DA == O_DBETA + 4 and O_DG == O_DA + 4
_T = lambda o: o - O_DG
PROJ_GROUPS = {
    "retention": [("rot", 0, 256), ("rot", 256, 256), ("main", O_RV, 256), ("main", O_RG, 256)],
    "deltanet": [("main", O_DQKV, 256), ("main", O_DQKV + 256, 256), ("main", O_DQKV + 512, 256),
                 ("main", O_DBETA, LANE), ("tail", _T(O_DG), 256)],
    "s5": [("tail", _T(O_SU), 256), ("tail", _T(O_SG), 256)],
    "gla": [("tail", _T(O_GQ), 256), ("tail", _T(O_GV), 256), ("tail", _T(O_GCODE), LANE), ("gg", 0, 256)],
}
G_RET, G_DN, G_S5, G_GLA = (sum(w for _, _, w in PROJ_GROUPS[k]) for k in ("retention", "deltanet", "s5", "gla"))
TAIL_COLS = _T(O_GCODE) + LANE
R_Q, R_K, R_V, R_G = 0, 256, 512, 768
D_QKV, D_SMALL, D_G = 0, 768, 896
S_U, S_G = 0, 256
L_Q, L_K, L_V, L_CODE, L_G = 0, 128, 256, 512, 640
SM_BETA, SM_A = 0, 4


def _bf(x):
    return x.astype(BF16)


def _dot(a, b):
    return jnp.dot(a, b, preferred_element_type=F32)


def _dot_nt(a, b):
    return lax.dot_general(a, b, (((1,), (1,)), ((), ())), preferred_element_type=F32)


def _dot_tn(a, b):
    return lax.dot_general(a, b, (((0,), (0,)), ((), ())), preferred_element_type=F32)


def _split2(x):
    x1 = _bf(x)
    x2 = _bf(x - x1.astype(F32))
    return x1, x2


def _dot_x2(x, m):
    x1, x2 = _split2(x)
    return _dot(x1, m) + _dot(x2, m)


def _dot_m2(m, x):
    x1, x2 = _split2(x)
    return _dot(m, x1) + _dot(m, x2)


def _sigmoid(x):
    return 1.0 / (1.0 + jnp.exp(-x))


def _silu(x):
    return x * _sigmoid(x)


def _softplus(x):
    return jnp.maximum(x, 0.0) + jnp.log(1.0 + jnp.exp(-jnp.abs(x)))


def _stack_heads(x, hm_ref):
    xb = _bf(x)
    rows = x.shape[0]
    return jnp.concatenate([xb * hm_ref[h, 0:rows, :] for h in range(N_HEADS)], axis=0)


def _head_rmsnorm_gate(o, g_row, gate, hs):
    ms = _dot(_bf(o * o), hs) * (1.0 / HEAD_DIM)
    return o * lax.rsqrt(ms + EPS) * g_row * _silu(gate)


def _s5_groups(tb, nb):
    n_grp = max(1, SUBLANES // nb)
    n_s5 = tb // S5_LAGS
    assert n_s5 % n_grp == 0
    return n_grp, n_s5 // n_grp


def _interleave(feeder, consumers):
    live = []
    feeding = True
    while feeding or live:
        if feeding:
            try:
                tag = next(feeder)
                while tag is not None:
                    live.append(consumers.pop(tag))
                    tag = next(feeder)
            except StopIteration:
                feeding = False
        for g in list(live):
            try:
                next(g)
            except StopIteration:
                live.remove(g)
    assert not consumers


def _layer_kernel(
        x_ref, cos_ref, sin_ref,
        gpre_ref, gpost_ref, wrot_ref, wmain_ref, wtail_ref, wgg_ref, wout_ref,
        hm_ref, hmqk_ref, hmg_ref, hs_ref, bd_ref, bdqk_ref, bdg_ref,
        tril_ref, strict_ref, eye_ref, eyer_ref, btri_ref, selc_ref, sperm_ref,
        rdall_ref, rqdec_ref, rkdec_ref, rcd_ref, rnorm_ref,
        dconv_ref, darow_ref, dbias_ref, debeta_ref, deg_ref, dnorm_ref,
        skcat_ref, sbbar_ref, scbd_ref, stre_ref, stim_ref, sa_ref, scar_ref,
        swglu_ref, sbglu_ref,
        gwgk_ref, gb_ref, gnorm_ref,
        out_ref,
        h_ref, pr_ref, pd_ref, ps_ref, pg_ref, y_ref, xc_ref, qkv_ref, us_ref, ucat_ref,
        sret_ref, sdn_ref, sgla_ref, ss5_ref,
        *, tb, nb):
    t_idx = pl.program_id(1)
    n_chunks = tb // CHUNK
    n_s5 = tb // S5_LAGS
    n_grp, n_stp = _s5_groups(tb, nb)
    batches = range(nb)
    rr = nb * tb

    def rb(b):
        return slice(b * tb, (b + 1) * tb)

    units = [(b, c, slice(b * tb + c * CHUNK, b * tb + (c + 1) * CHUNK))
             for c in range(n_chunks) for b in batches]
    n_units = len(units)

    @pl.when(t_idx == 0)
    def _reset():
        sret_ref[...] = jnp.zeros_like(sret_ref)
        sdn_ref[...] = jnp.zeros_like(sdn_ref)
        sgla_ref[...] = jnp.zeros_like(sgla_ref)
        ss5_ref[...] = jnp.zeros_like(ss5_ref)
        for b in batches:
            xc_ref[b, 0:8, :] = jnp.zeros((8, 768), F32)
            us_ref[b, 0:S5_LAGS, :] = jnp.zeros((S5_LAGS, BRANCH_W), F32)

    for b in batches:
        x = x_ref[b]
        h_ref[rb(b), :] = _bf(x * lax.rsqrt(jnp.mean(x * x, axis=-1, keepdims=True) + EPS) * gpre_ref[...])

    def projections():
        weights = {"rot": wrot_ref, "main": wmain_ref, "tail": wtail_ref, "gg": wgg_ref}
        for name, dst_ref in (("deltanet", pd_ref), ("s5", ps_ref), ("gla", pg_ref), ("retention", pr_ref)):
            c0 = 0
            for operand, first, width in PROJ_GROUPS[name]:
                dst_ref[:, c0:c0 + width] = _dot(h_ref[...], weights[operand][:, first:first + width])
                c0 += width
                yield None
            yield name

    o_partial = []

    def out_project(branch):
        rows = slice(branch * BRANCH_W, (branch + 1) * BRANCH_W)
        o_partial.append(_dot(y_ref[:, rows], wout_ref[rows, :]))

    hs = hs_ref[...]
    bd = bd_ref[...]
    bd32 = bd.astype(F32)

    def retention():
        rc = min(RET_CHUNK, tb)
        n_rc = tb // rc
        runits = [(b, c, slice(b * tb + c * rc, b * tb + (c + 1) * rc)) for c in range(n_rc) for b in batches]
        ret_q, ret_k, ret_v = {}, {}, {}
        for b, c, sl in runits:
            cs = cos_ref[c * rc:(c + 1) * rc, :]
            sn = sin_ref[c * rc:(c + 1) * rc, :]
            qa = pr_ref[sl, R_Q:R_Q + LANE]
            qb = pr_ref[sl, R_Q + LANE:R_Q + 2 * LANE]
            ka = pr_ref[sl, R_K:R_K + LANE]
            kb = pr_ref[sl, R_K + LANE:R_K + 2 * LANE]
            ret_q[b, c] = jnp.concatenate([qa * cs - qb * sn, qa * sn + qb * cs], axis=1)
            ret_k[b, c] = jnp.concatenate([ka * cs - kb * sn, ka * sn + kb * cs], axis=1)
            ret_v[b, c] = pr_ref[sl, R_V:R_V + BRANCH_W]
            if b == nb - 1:
                yield
        o_intra, kv = {}, {}
        for b, c, sl in runits:
            s_all = _dot_nt(_bf(ret_q[b, c]), _stack_heads(ret_k[b, c], hmqk_ref))
            o_intra[b, c] = _dot(_bf(s_all * rdall_ref[...]), _stack_heads(ret_v[b, c], hm_ref))
            kv[b, c] = bdqk_ref[...] * _dot_tn(_bf(ret_v[b, c]), _bf(ret_k[b, c] * rkdec_ref[...]))
            if b == nb - 1:
                yield
        st = [sret_ref[b] for b in batches]
        o_ret = {}
        for b, c, sl in runits:
            o_ret[b, c] = o_intra[b, c] + _dot_nt(_bf(ret_q[b, c] * rqdec_ref[...]), _bf(st[b]))
            st[b] = st[b] * rcd_ref[...] + kv[b, c]
            if b == nb - 1:
                yield
        for b in batches:
            sret_ref[b] = st[b]
        o = jnp.concatenate([o_ret[b, c] for b in batches for c in range(n_rc)], axis=0)
        y_ref[:, 0:BRANCH_W] = _bf(_head_rmsnorm_gate(o, rnorm_ref[...], pr_ref[:, R_G:R_G + BRANCH_W], hs))
        yield
        out_project(0)

    def deltanet():
        for b in batches:
            xc_ref[b, 8:8 + tb, :] = pd_ref[rb(b), D_QKV:D_QKV + 768]
            conv = dconv_ref[DN_CONV - 1:DN_CONV, :] * xc_ref[b, 8:8 + tb, :]
            for i in range(DN_CONV - 1):
                conv = conv + dconv_ref[i:i + 1, :] * xc_ref[b, 5 + i:5 + i + tb, :]
            qkv_ref[rb(b), :] = _silu(conv)
            xc_ref[b, 0:8, :] = xc_ref[b, tb:tb + 8, :]
            yield

        q = qkv_ref[:, 0:256]
        k = qkv_ref[:, 256:512]
        v = qkv_ref[:, 512:768]
        ss = _dot(_bf(jnp.concatenate([q * q, k * k], axis=0)), hs)
        qn = q * lax.rsqrt(ss[0:rr] + EPS) * (HEAD_DIM ** -0.5)
        kn = k * lax.rsqrt(ss[rr:2 * rr] + EPS)
        yield
        small = pd_ref[:, D_SMALL:D_SMALL + LANE]
        beta = _dot(_bf(_sigmoid(small)), debeta_ref[...])
        g_s = -darow_ref[...] * _softplus(small + dbias_ref[...])
        gcum_s = jnp.concatenate([_dot_m2(btri_ref[...], g_s[rb(b)]) for b in batches], axis=0)
        yield
        gcum = _dot_x2(gcum_s, deg_ref[...])
        grows = [_dot_m2(selc_ref[...], gcum[rb(b)] * eyer_ref[...]) for b in batches]
        yield
        egc = jnp.exp(gcum)
        kbeta = kn * beta
        vbeta = v * beta
        kbe = kbeta * egc
        qg = qn * egc
        yield

        dec, glast, kgt = [], [], []
        for b, c, sl in units:
            gc = gcum[sl]
            dec.append(jnp.exp(jnp.minimum(gc - grows[b][c:c + 1, :], 0.0)))
            glast.append(gc[CHUNK - 1:CHUNK, :])
            kgt.append(_bf((kn[sl] * jnp.exp(glast[-1] - gc)).T))
            if b == nb - 1:
                yield
        aa = []
        for b, _, sl in units:
            aa.append(_dot_nt(_bf(jnp.concatenate([kbeta[sl], qn[sl]], axis=0)), _stack_heads(kn[sl], hm_ref)))
            if b == nb - 1:
                yield
        attn = [aa[u][CHUNK:2 * CHUNK] * dec[u] * tril_ref[...] for u in range(n_units)]

        def blockdiag(m):
            mb = _bf(m)
            return jnp.concatenate([mb, mb, mb, mb], axis=0) * bd

        pw = [-(aa[u][0:CHUNK] * dec[u] * strict_ref[...]) for u in range(n_units)]
        t_all = [eye_ref[...] + pw[u] for u in range(n_units)]
        yield
        pw = [_dot(_bf(pw[u]), blockdiag(pw[u])) for u in range(n_units)]
        yield
        for level in range(1, 5):
            both = [_dot(_bf(jnp.concatenate([pw[u], t_all[u]], axis=0)), blockdiag(pw[u]))
                    for u in range(n_units)]
            pw = [both[u][0:CHUNK] for u in range(n_units)]
            t_all = [t_all[u] + both[u][CHUNK:2 * CHUNK] for u in range(n_units)]
            yield
        t_all = [t_all[u] + _dot(_bf(t_all[u]), blockdiag(pw[u])) for u in range(n_units)]
        yield
        uw = []
        for u, (b, _, sl) in enumerate(units):
            uw.append(_dot(_bf(t_all[u]), jnp.concatenate([_stack_heads(vbeta[sl], hm_ref),
                                                           _stack_heads(kbe[sl], hm_ref)], axis=1)))
            if b == nb - 1:
                yield

        st = [sdn_ref[b] for b in batches]
        o_parts = {}
        for u, (b, c, sl) in enumerate(units):
            wq = _dot(_bf(jnp.concatenate([uw[u][:, 256:512], qg[sl]], axis=0)), _bf(st[b]))
            v_new = uw[u][:, 0:256] - wq[0:CHUNK]
            o_parts[(b, c)] = wq[CHUNK:2 * CHUNK] + _dot(_bf(attn[u]), _stack_heads(v_new, hm_ref))
            st[b] = st[b] * jnp.exp(glast[u]) + bd32 * _dot(kgt[u], _bf(v_new))
            if b == nb - 1:
                yield
        for b in batches:
            sdn_ref[b] = st[b]
        o = jnp.concatenate([o_parts[(b, c)] for b in batches for c in range(n_chunks)], axis=0)
        y_ref[:, BRANCH_W:2 * BRANCH_W] = _bf(
            _head_rmsnorm_gate(o, dnorm_ref[...], pd_ref[:, D_G:D_G + BRANCH_W], hs))
        yield
        out_project(1)

    def s5():
        u_in = ps_ref[:, S_U:S_U + BRANCH_W]
        row_in_chunk = lax.broadcasted_iota(jnp.int32, (tb, BRANCH_W), 0) % S5_LAGS
        ucat_ref[:, 0:BRANCH_W] = _bf(u_in)
        for b in batches:
            us_ref[b, S5_LAGS:S5_LAGS + tb, :] = u_in[rb(b)]
            for tau in range(1, S5_LAGS):
                shifted = us_ref[b, S5_LAGS - tau:S5_LAGS - tau + tb, :]
                ucat_ref[rb(b), tau * BRANCH_W:(tau + 1) * BRANCH_W] = _bf(
                    jnp.where(row_in_chunk >= tau, shifted, 0.0))
            us_ref[b, 0:S5_LAGS, :] = us_ref[b, tb:tb + S5_LAGS, :]
            yield
        u_perm = _bf(_dot(sperm_ref[...], _bf(u_in)))
        xx = _dot(u_perm, sbbar_ref[...])
        yield
        a1_re = sa_ref[0:1, :]
        a1_im = sa_ref[1:2, :]
        n_cb = nb * n_s5
        z_re = xx[0:n_cb, 0:S5_HALF]
        z_im = xx[0:n_cb, S5_HALF:2 * S5_HALF]
        for j in range(1, S5_LAGS):
            xj_re = xx[j * n_cb:(j + 1) * n_cb, 0:S5_HALF]
            xj_im = xx[j * n_cb:(j + 1) * n_cb, S5_HALF:2 * S5_HALF]
            z_re, z_im = a1_re * z_re - a1_im * z_im + xj_re, a1_re * z_im + a1_im * z_re + xj_im
            if j % 2 == 1:
                yield
        yield
        y_s5 = jnp.concatenate([_dot(ucat_ref[rb(b), :], skcat_ref[...]) for b in batches], axis=0)
        yield
        grp_rows = n_grp * nb
        al_re = sa_ref[2:3, :]
        al_im = sa_ref[3:4, :]
        s_re = jnp.zeros((grp_rows, S5_HALF), F32)
        s_im = jnp.zeros((grp_rows, S5_HALF), F32)
        loc_re, loc_im = [], []
        for m in range(n_stp):
            loc_re.append(s_re)
            loc_im.append(s_im)
            zm_re = z_re[m * grp_rows:(m + 1) * grp_rows]
            zm_im = z_im[m * grp_rows:(m + 1) * grp_rows]
            s_re, s_im = al_re * s_re - al_im * s_im + zm_re, al_re * s_im + al_im * s_re + zm_im
            if m % 2 == 1:
                yield
        ag_re = scar_ref[2 * n_stp:2 * n_stp + 1, :]
        ag_im = scar_ref[2 * n_stp + 1:2 * n_stp + 2, :]
        c_re = ss5_ref[0]
        c_im = ss5_ref[1]
        car_re, car_im = [], []
        for g in range(n_grp):
            car_re.append(c_re)
            car_im.append(c_im)
            e_re = s_re[g * nb:(g + 1) * nb]
            e_im = s_im[g * nb:(g + 1) * nb]
            c_re, c_im = ag_re * c_re - ag_im * c_im + e_re, ag_re * c_im + ag_im * c_re + e_im
        ss5_ref[0] = c_re
        ss5_ref[1] = c_im
        yield
        car_re = jnp.concatenate(car_re, axis=0)
        car_im = jnp.concatenate(car_im, axis=0)
        sp_re, sp_im = [], []
        for m in range(n_stp):
            p_re = scar_ref[m:m + 1, :]
            p_im = scar_ref[n_stp + m:n_stp + m + 1, :]
            sp_re.append(loc_re[m] + p_re * car_re - p_im * car_im)
            sp_im.append(loc_im[m] + p_re * car_im + p_im * car_re)
        yield
        sp_re = jnp.concatenate(sp_re, axis=0)[:, None, :]
        sp_im = jnp.concatenate(sp_im, axis=0)[:, None, :]
        t_re = stre_ref[...][None]
        t_im = stim_ref[...][None]

        def batch_major(v):
            tiles = []
            for b in batches:
                for n in range(n_s5):
                    g, m = divmod(n, n_stp)
                    src = (m * n_grp + g) * nb + b
                    tiles.append(v[src * S5_LAGS:(src + 1) * S5_LAGS])
            return jnp.concatenate(tiles, axis=0)

        v_re = batch_major((t_re * sp_re - t_im * sp_im).reshape(rr, S5_HALF))
        yield
        v_im = batch_major((t_re * sp_im + t_im * sp_re).reshape(rr, S5_HALF))
        yield
        vv = _bf(jnp.concatenate([v_re, v_im], axis=1))
        y_s5 = y_s5 + jnp.concatenate([_dot(vv[rb(b)], scbd_ref[...]) for b in batches], axis=0)
        yield
        c0 = math.sqrt(2.0 / math.pi)
        y_s5 = 0.5 * y_s5 * (1.0 + jnp.tanh(c0 * (y_s5 + 0.044715 * (y_s5 * y_s5 * y_s5))))
        y_s5 = y_s5 * _sigmoid(_dot(_bf(y_s5), swglu_ref[...]) + sbglu_ref[...])
        y_ref[:, 2 * BRANCH_W:3 * BRANCH_W] = _bf(y_s5 * _silu(ps_ref[:, S_G:S_G + BRANCH_W]))
        yield
        out_project(2)

    def gla():
        z = _dot(_bf(pg_ref[:, L_CODE:L_CODE + LANE]), gwgk_ref[...]) + gb_ref[...]
        gk = -_softplus(-z) * (1.0 / GLA_GATE_TAU)
        yield
        cum = jnp.concatenate([_dot_m2(btri_ref[...], gk[rb(b)]) for b in batches], axis=0)
        yield
        q = pg_ref[:, L_Q:L_Q + GLA_QK]
        k = pg_ref[:, L_K:L_K + GLA_QK]
        v = pg_ref[:, L_V:L_V + BRANCH_W]
        qt = _bf(q * jnp.exp(cum) * (GLA_DK ** -0.5))
        kt = k * jnp.exp(-cum)
        clast = [cum[sl][CHUNK - 1:CHUNK, :] for _, _, sl in units]
        yield
        s_all = []
        for b, _, sl in units:
            s_all.append(_dot_nt(qt[sl], _stack_heads(kt[sl], hmg_ref)))
            if b == nb - 1:
                yield
        o_intra = []
        for u, (b, _, sl) in enumerate(units):
            o_intra.append(_dot(_bf(s_all[u] * tril_ref[...]), _stack_heads(v[sl], hm_ref)))
            if b == nb - 1:
                yield
        kv = []
        for u, (b, _, sl) in enumerate(units):
            kv.append(bdg_ref[...] * _dot_tn(_bf(v[sl]), _bf(k[sl] * jnp.exp(clast[u] - cum[sl]))))
            if b == nb - 1:
                yield
        st = [sgla_ref[b] for b in batches]
        o_parts = {}
        for u, (b, c, sl) in enumerate(units):
            o_parts[(b, c)] = o_intra[u] + _dot_nt(qt[sl], _bf(st[b]))
            st[b] = st[b] * jnp.exp(clast[u]) + kv[u]
            if b == nb - 1:
                yield
        for b in batches:
            sgla_ref[b] = st[b]
        o = jnp.concatenate([o_parts[(b, c)] for b in batches for c in range(n_chunks)], axis=0)
        y_ref[:, 3 * BRANCH_W:4 * BRANCH_W] = _bf(
            _head_rmsnorm_gate(o, gnorm_ref[...], pg_ref[:, L_G:L_G + BRANCH_W], hs))
        yield
        out_project(3)

    _interleave(projections(), {"s5": s5(), "deltanet": deltanet(), "retention": retention(), "gla": gla()})

    o = o_partial[0] + o_partial[1] + o_partial[2] + o_partial[3]
    o = o * lax.rsqrt(jnp.mean(o * o, axis=-1, keepdims=True) + EPS) * gpost_ref[...]
    for b in batches:
        out_ref[b] = x_ref[b] + o[rb(b)]


def _in_proj_operands(w_in):
    dp, dm, _ = w_in.shape
    main = _bf(w_in)

    def deinterleave(o):
        t = w_in[:, :, o:o + 256].reshape(dp, dm, N_HEADS, HEAD_DIM // 2, 2)
        return jnp.transpose(t, (0, 1, 4, 2, 3)).reshape(dp, dm, 256)

    rot = _bf(jnp.concatenate([deinterleave(O_RQ) * (HEAD_DIM ** -0.5), deinterleave(O_RK)], axis=2))
    return dict(wrot=rot, wmain=main, wtail=main[:, :, O_DG:O_DG + TAIL_COLS], wgg=main[:, :, O_GG:O_GG + 256])


@functools.lru_cache(maxsize=None)
def _constant_tables(seq_len, tb, nb):
    c = CHUNK
    lane = np.arange(256)
    head_std = lane // 64
    head_qk = (lane % 128) // 32
    head_g = np.arange(128) // 32
    i = np.arange(c)[:, None]
    j = np.arange(256)[None, :] % 64
    t = {}
    heads = np.arange(4)[:, None, None]
    t["hm"] = np.broadcast_to(head_std[None, None, :] == heads, (4, tb, 256)).astype(np.float32)
    t["hmqk"] = np.broadcast_to(head_qk[None, None, :] == heads, (4, tb, 256)).astype(np.float32)
    t["hmg"] = np.broadcast_to(head_g[None, None, :] == heads, (4, c, 128)).astype(np.float32)
    t["hs"] = (head_std[:, None] == head_std[None, :]).astype(np.float32)
    t["bd"] = t["hs"]
    t["bdqk"] = (head_std[:, None] == head_qk[None, :]).astype(np.float32)
    t["bdg"] = (head_std[:, None] == head_g[None, :]).astype(np.float32)
    t["tril"] = (i >= j).astype(np.float32)
    t["strict"] = (i > j).astype(np.float32)
    t["eye"] = (i == j).astype(np.float32)
    r = np.arange(tb)
    same_chunk = (r[:, None] // c) == (r[None, :] // c)
    t["btri"] = (same_chunk & (r[:, None] >= r[None, :])).astype(np.float32)
    t["selc"] = ((r[None, :] // c) == np.arange(16)[:, None]).astype(np.float32)
    n_grp, n_stp = _s5_groups(tb, nb)
    jj, mm, gg, bb = np.meshgrid(np.arange(S5_LAGS), np.arange(n_stp), np.arange(n_grp), np.arange(nb),
                                 indexing="ij")
    old = (bb * tb + (gg * n_stp + mm) * S5_LAGS + jj).reshape(-1)
    sperm = np.zeros((nb * tb, nb * tb), np.float32)
    sperm[np.arange(nb * tb), old] = 1.0
    t["sperm"] = sperm
    t["eyer"] = np.tile(t["eye"], (tb // c, 1))
    rc = min(RET_CHUNK, tb)
    lg = np.log(1.0 - 2.0 ** (-5.0 - np.arange(4, dtype=np.float64)))
    ri = np.arange(rc)[:, None]
    rj = np.arange(4 * rc)[None, :] % rc
    lg_cols = lg[np.arange(4 * rc) // rc][None, :]
    t["rdall"] = np.where(ri >= rj, np.exp(lg_cols * np.where(ri >= rj, ri - rj, 0)), 0.0).astype(np.float32)
    lg_qk = lg[head_qk][None, :]
    t["rqdec"] = np.exp(lg_qk * (ri + 1.0)).astype(np.float32)
    t["rkdec"] = np.exp(lg_qk * (rc - 1.0 - ri)).astype(np.float32)
    t["rcd"] = np.exp(lg_qk * rc).astype(np.float32)
    inv = ROPE_BASE ** (-np.arange(0, HEAD_DIM, 2, dtype=np.float64) / HEAD_DIM)
    ang = np.arange(seq_len, dtype=np.float64)[:, None] * inv[None, :]
    t["cos"] = np.tile(np.cos(ang), (1, 4)).astype(np.float32)
    t["sin"] = np.tile(np.sin(ang), (1, 4)).astype(np.float32)
    t["tile16"] = np.tile(np.eye(S5_GROUP, dtype=np.float32), (1, S5_GROUPS))
    t["tile64"] = np.tile(np.eye(S5_STATE, dtype=np.float32), (1, S5_GROUPS))
    return t


def _s5_tables(lam_re, lam_im, b_re, b_im, c_re, c_im, d, log_dt, tabs, n_stp):
    hp = lax.Precision.HIGHEST
    g, p, hc = S5_GROUPS, S5_STATE, S5_GROUP
    nl = lam_re.shape[0]
    lam_re, lam_im = lam_re.astype(F32), lam_im.astype(F32)
    dt = jnp.exp(log_dt.astype(F32))[..., None]
    mag = jnp.exp(lam_re * dt)
    ang = lam_im * dt
    a_re, a_im = mag * jnp.cos(ang), mag * jnp.sin(ang)
    den = lam_re * lam_re + lam_im * lam_im
    nr, ni = a_re - 1.0, a_im
    coef_re = (nr * lam_re + ni * lam_im) / den
    coef_im = (ni * lam_re - nr * lam_im) / den
    b_re, b_im = b_re.astype(F32), b_im.astype(F32)
    bb_re = coef_re[..., None] * b_re - coef_im[..., None] * b_im
    bb_im = coef_re[..., None] * b_im + coef_im[..., None] * b_re

    def apow(n):
        n = jnp.asarray(n, F32)[None, :, None, None]
        m = jnp.exp((lam_re * dt)[:, None] * n)
        return m * jnp.cos(ang[:, None] * n), m * jnp.sin(ang[:, None] * n)

    lags = np.arange(S5_LAGS)
    p_re, p_im = apow(lags)
    c_re, c_im = c_re.astype(F32), c_im.astype(F32)
    ab_re = p_re[..., None] * bb_re[:, None] - p_im[..., None] * bb_im[:, None]
    ab_im = p_re[..., None] * bb_im[:, None] + p_im[..., None] * bb_re[:, None]
    kk = (jnp.einsum('ntgpi,ngop->ntgio', ab_re, c_re, precision=hp)
          - jnp.einsum('ntgpi,ngop->ntgio', ab_im, c_im, precision=hp))
    kk = kk.at[:, 0].add(d.astype(F32)[..., None] * jnp.eye(hc, dtype=F32))
    grp256 = np.arange(g * hc) // hc
    grp1024 = np.arange(g * p) // p
    tile16 = jnp.asarray(tabs["tile16"])
    tile64 = jnp.asarray(tabs["tile64"])
    kcat = jnp.einsum('nro,oc->nrc', kk.reshape(nl, S5_LAGS * g * hc, hc), tile16, precision=hp)
    kcat = jnp.where(jnp.asarray(np.tile(grp256, S5_LAGS)[:, None] == grp256[None, :]), kcat, 0.0)

    def rows_to_state(bb):
        m = jnp.einsum('nrp,pc->nrc', jnp.transpose(bb, (0, 1, 3, 2)).reshape(nl, g * hc, p), tile64,
                       precision=hp)
        return jnp.where(jnp.asarray(grp256[:, None] == grp1024[None, :]), m, 0.0)

    def state_to_rows(cc):
        m = jnp.einsum('nro,oc->nrc', jnp.transpose(cc, (0, 1, 3, 2)).reshape(nl, g * p, hc), tile16,
                       precision=hp)
        return jnp.where(jnp.asarray(grp1024[:, None] == grp256[None, :]), m, 0.0)

    bbar = jnp.concatenate([rows_to_state(bb_re), rows_to_state(bb_im)], axis=2)
    cbd = jnp.concatenate([state_to_rows(c_re), -state_to_rows(c_im)], axis=1)
    t_re, t_im = apow(lags + 1)
    a1_re, a1_im = apow(np.array([1]))
    al_re, al_im = apow(np.array([S5_LAGS]))
    pm_re, pm_im = apow(S5_LAGS * np.arange(n_stp))
    ag_re, ag_im = apow(np.array([S5_LAGS * n_stp]))
    flat = lambda z: z.reshape(nl, z.shape[1], g * p)
    return dict(skcat=_bf(kcat), sbbar=_bf(bbar), scbd=_bf(cbd),
                stre=flat(t_re), stim=flat(t_im),
                sa=jnp.concatenate([flat(a1_re), flat(a1_im), flat(al_re), flat(al_im)], axis=1),
                scar=jnp.concatenate([flat(pm_re), flat(pm_im), flat(ag_re), flat(ag_im)], axis=1))


def _const_spec(arr):
    nd = arr.ndim
    return pl.BlockSpec(arr.shape, lambda b, t, _nd=nd: (0,) * _nd)


def _layer_spec(arr, layer):
    nd = arr.ndim - 1
    return pl.BlockSpec((None,) + arr.shape[1:], lambda b, t, _nd=nd, _l=layer: (_l,) + (0,) * _nd)


def _layer_call(x, layer, stacked, const_inputs, cos, sin, tb, nb):
    bsz, seq, _ = x.shape
    grid = (bsz // nb, seq // tb)
    rr = nb * tb
    in_specs = [pl.BlockSpec((nb, tb, D_MODEL), lambda b, t: (b, t, 0)),
                pl.BlockSpec((tb, LANE), lambda b, t: (t, 0)),
                pl.BlockSpec((tb, LANE), lambda b, t: (t, 0))]
    operands = [x, cos, sin]
    order = ["gpre", "gpost", "wrot", "wmain", "wtail", "wgg", "wout",
             "hm", "hmqk", "hmg", "hs", "bd", "bdqk", "bdg", "tril", "strict", "eye", "eyer", "btri", "selc", "sperm",
             "rdall", "rqdec", "rkdec", "rcd", "rnorm",
             "dconv", "darow", "dbias", "debeta", "deg", "dnorm",
             "skcat", "sbbar", "scbd", "stre", "stim", "sa", "scar", "swglu", "sbglu",
             "gwgk", "gb", "gnorm"]
    for name in order:
        if name in stacked:
            operands.append(stacked[name])
            in_specs.append(_layer_spec(stacked[name], layer))
        else:
            operands.append(const_inputs[name])
            in_specs.append(_const_spec(const_inputs[name]))
    scratch = [
        pltpu.VMEM((rr, D_MODEL), BF16),
        pltpu.VMEM((rr, G_RET), F32),
        pltpu.VMEM((rr, G_DN), F32),
        pltpu.VMEM((rr, G_S5), F32),
        pltpu.VMEM((rr, G_GLA), F32),
        pltpu.VMEM((rr, D_MODEL), BF16),
        pltpu.VMEM((nb, tb + 8, 768), F32),
        pltpu.VMEM((rr, 768), F32),
        pltpu.VMEM((nb, tb + S5_LAGS, BRANCH_W), F32),
        pltpu.VMEM((rr, S5_LAGS * BRANCH_W), BF16),
        pltpu.VMEM((nb, BRANCH_W, BRANCH_W), F32),
        pltpu.VMEM((nb, BRANCH_W, BRANCH_W), F32),
        pltpu.VMEM((nb, BRANCH_W, GLA_QK), F32),
        pltpu.VMEM((2, nb, S5_HALF), F32),
    ]
    return pl.pallas_call(
        functools.partial(_layer_kernel, tb=tb, nb=nb),
        grid=grid,
        in_specs=in_specs,
        out_specs=pl.BlockSpec((nb, tb, D_MODEL), lambda b, t: (b, t, 0)),
        out_shape=jax.ShapeDtypeStruct(x.shape, x.dtype),
        scratch_shapes=scratch,
        compiler_params=pltpu.CompilerParams(
            dimension_semantics=("arbitrary", "arbitrary"),
            vmem_limit_bytes=VMEM_LIMIT_BYTES),
        name="hybrid_layer",
    )(*operands)


def _pick_tiles(bsz, seq):
    nb = 2 if bsz % 2 == 0 else 1
    for tb in (256, 128, 64):
        if seq % tb == 0:
            return tb, nb
    raise ValueError(f"sequence length {seq} must be a multiple of {CHUNK}")


def kernel(x, norm_pre, norm_post, w_in, w_out, ret_norm, dn_conv, dn_a_log, dn_dt_bias, dn_norm,
           s5_lam_re, s5_lam_im, s5_b_re, s5_b_im, s5_c_re, s5_c_im, s5_d, s5_log_dt, s5_w_glu, s5_b_glu,
           gla_w_gk, gla_b_gk, gla_norm):
    bsz, seq, dm = x.shape
    depth = w_in.shape[0]
    assert dm == D_MODEL and x.dtype == F32
    tb, nb = _pick_tiles(bsz, seq)
    tabs = _constant_tables(seq, tb, nb)
    bf_names = ("hm", "hmqk", "hmg", "hs", "bd", "btri", "selc", "sperm")
    skip = ("cos", "sin", "tile16", "tile64")
    const_inputs = {k: jnp.asarray(v, BF16 if k in bf_names else F32)
                    for k, v in tabs.items() if k not in skip}
    cos = jnp.asarray(tabs["cos"])
    sin = jnp.asarray(tabs["sin"])

    w_in_ops = _in_proj_operands(w_in)
    w_out_b = _bf(w_out)

    e_beta = np.zeros((LANE, 256), np.float32)
    e_g = np.zeros((LANE, 256), np.float32)
    for hh in range(N_HEADS):
        e_beta[SM_BETA + hh, 64 * hh:64 * hh + 64] = 1.0
        e_g[SM_A + hh, 64 * hh:64 * hh + 64] = 1.0
    e_beta = jnp.asarray(e_beta, BF16)
    e_g = jnp.asarray(e_g, BF16)

    a_rows = jnp.zeros((depth, 1, LANE), F32).at[:, 0, SM_A:SM_A + 4].set(jnp.exp(dn_a_log.astype(F32)))
    b_rows = jnp.zeros((depth, 1, LANE), F32).at[:, 0, SM_A:SM_A + 4].set(dn_dt_bias.astype(F32))
    wgk = jnp.zeros((depth, LANE, GLA_QK), BF16).at[:, 0:GLA_GATE_RANK, :].set(_bf(gla_w_gk))
    s5 = _s5_tables(s5_lam_re, s5_lam_im, s5_b_re, s5_b_im, s5_c_re, s5_c_im, s5_d, s5_log_dt, tabs,
                    _s5_groups(tb, nb)[1])
    tile4 = lambda g: jnp.tile(g.astype(F32), (1, N_HEADS)).reshape(depth, 1, 256)
    rnorm, dnorm, gnorm = tile4(ret_norm), tile4(dn_norm), tile4(gla_norm)
    swglu = _bf(s5_w_glu)

    stacked = dict(
        gpre=norm_pre.reshape(depth, 1, dm).astype(F32),
        gpost=norm_post.reshape(depth, 1, dm).astype(F32),
        wout=w_out_b,
        rnorm=rnorm,
        dconv=dn_conv.astype(F32),
        darow=a_rows, dbias=b_rows,
        dnorm=dnorm,
        swglu=swglu, sbglu=s5_b_glu.reshape(depth, 1, 256).astype(F32),
        gwgk=wgk, gb=gla_b_gk.reshape(depth, 1, GLA_QK).astype(F32),
        gnorm=gnorm,
    )
    stacked.update(s5)
    stacked.update(w_in_ops)
    const_inputs.update(debeta=e_beta, deg=e_g)
    for i in range(depth):
        x = _layer_call(x, i, stacked, const_inputs, cos, sin, tb, nb)
    return x
```

```python
import functools
import math

import jax
import jax.numpy as jnp
import numpy as np
from jax import lax
from jax.experimental import pallas as pl
from jax.experimental.pallas import tpu as pltpu

F32 = jnp.float32
BF16 = jnp.bfloat16

D_MODEL = 1024
BRANCH_W = 256
N_HEADS = 4
HEAD_DIM = 64
EPS = 1e-6
ROPE_BASE = 10000.0
DN_CONV = 4
S5_GROUP = 16
S5_GROUPS = 16
S5_STATE = 64
S5_HALF = S5_GROUPS * S5_STATE
GLA_DK = 32
GLA_QK = 128
GLA_GATE_RANK = 16
GLA_GATE_TAU = 16.0
IN_SPLITS = [256, 256, 256, 256, 768, 4, 4, 256, 256, 256, 128, 128, 256, 16, 256]

CHUNK = 64
RET_CHUNK = 128
S5_LAGS = 8
LANE = 128
SUBLANES = 8
VMEM_LIMIT_BYTES = 56 * 1024 * 1024

_OFF = np.concatenate([[0], np.cumsum(IN_SPLITS)])
(O_RQ, O_RK, O_RV, O_RG, O_DQKV, O_DBETA, O_DA, O_DG, O_SU, O_SG,
 O_GQ, O_GK, O_GV, O_GCODE, O_GG) = (int(o) for o in _OFF[:-1])
assert O_DBETA % LANE == 0 and O_DA == O_DBETA + 4 and O_DG == O_DA + 4
_T = lambda o: o - O_DG
PROJ_GROUPS = {
    "retention": [("rot", 0, 256), ("rot", 256, 256), ("main", O_RV, 256), ("main", O_RG, 256)],
    "deltanet": [("main", O_DQKV, 256), ("main", O_DQKV + 256, 256), ("main", O_DQKV + 512, 256),
                 ("main", O_DBETA, LANE), ("tail", _T(O_DG), 256)],
    "s5": [("tail", _T(O_SU), 256), ("tail", _T(O_SG), 256)],
    "gla": [("tail", _T(O_GQ), 256), ("tail", _T(O_GV), 256), ("tail", _T(O_GCODE), LANE), ("gg", 0, 256)],
}
G_RET, G_DN, G_S5, G_GLA = (sum(w for _, _, w in PROJ_GROUPS[k]) for k in ("retention", "deltanet", "s5", "gla"))
TAIL_COLS = _T(O_GCODE) + LANE
R_Q, R_K, R_V, R_G = 0, 256, 512, 768
D_QKV, D_SMALL, D_G = 0, 768, 896
S_U, S_G = 0, 256
L_Q, L_K, L_V, L_CODE, L_G = 0, 128, 256, 512, 640
SM_BETA, SM_A = 0, 4


def _bf(x):
    return x.astype(BF16)


def _dot(a, b):
    return jnp.dot(a, b, preferred_element_type=F32)


def _dot_nt(a, b):
    return lax.dot_general(a, b, (((1,), (1,)), ((), ())), preferred_element_type=F32)


def _dot_tn(a, b):
    return lax.dot_general(a, b, (((0,), (0,)), ((), ())), preferred_element_type=F32)


def _split2(x):
    x1 = _bf(x)
    x2 = _bf(x - x1.astype(F32))
    return x1, x2


def _dot_x2(x, m):
    x1, x2 = _split2(x)
    return _dot(x1, m) + _dot(x2, m)


def _dot_m2(m, x):
    x1, x2 = _split2(x)
    return _dot(m, x1) + _dot(m, x2)


def _sigmoid(x):
    return 1.0 / (1.0 + jnp.exp(-x))


def _silu(x):
    return x * _sigmoid(x)


def _softplus(x):
    return jnp.maximum(x, 0.0) + jnp.log(1.0 + jnp.exp(-jnp.abs(x)))


def _stack_heads(x, hm_ref):
    xb = _bf(x)
    rows = x.shape[0]
    return jnp.concatenate([xb * hm_ref[h, 0:rows, :] for h in range(N_HEADS)], axis=0)


def _head_rmsnorm_gate(o, g_row, gate, hs):
    ms = _dot(_bf(o * o), hs) * (1.0 / HEAD_DIM)
    return o * lax.rsqrt(ms + EPS) * g_row * _silu(gate)


def _s5_groups(tb, nb):
    n_grp = max(1, SUBLANES // nb)
    n_s5 = tb // S5_LAGS
    assert n_s5 % n_grp == 0
    return n_grp, n_s5 // n_grp


def _interleave(feeder, consumers):
    live = []
    feeding = True
    while feeding or live:
        if feeding:
            try:
                tag = next(feeder)
                while tag is not None:
                    live.append(consumers.pop(tag))
                    tag = next(feeder)
            except StopIteration:
                feeding = False
        for g in list(live):
            try:
                next(g)
            except StopIteration:
                live.remove(g)
    assert not consumers


def _layer_kernel(
        x_ref, cos_ref, sin_ref,
        gpre_ref, gpost_ref, wrot_ref, wmain_ref, wtail_ref, wgg_ref, wout_ref,
        hm_ref, hmqk_ref, hmg_ref, hs_ref, bd_ref, bdqk_ref, bdg_ref,
        tril_ref, strict_ref, eye_ref, eyer_ref, btri_ref, selc_ref, sperm_ref,
        rdall_ref, rqdec_ref, rkdec_ref, rcd_ref, rnorm_ref,
        dconv_ref, darow_ref, dbias_ref, debeta_ref, deg_ref, dnorm_ref,
        skcat_ref, sbbar_ref, scbd_ref, stre_ref, stim_ref, sa_ref, scar_ref,
        swglu_ref, sbglu_ref,
        gwgk_ref, gb_ref, gnorm_ref,
        out_ref,
        h_ref, pr_ref, pd_ref, ps_ref, pg_ref, y_ref, xc_ref, qkv_ref, ucat_ref,
        sret_ref, sdn_ref, sgla_ref, ss5_ref,
        *, tb, nb):
    t_idx = pl.program_id(1)
    n_chunks = tb // CHUNK
    n_s5 = tb // S5_LAGS
    n_grp, n_stp = _s5_groups(tb, nb)
    batches = range(nb)
    rr = nb * tb

    def rb(b):
        return slice(b * tb, (b + 1) * tb)

    units = [(b, c, slice(b * tb + c * CHUNK, b * tb + (c + 1) * CHUNK))
             for c in range(n_chunks) for b in batches]
    n_units = len(units)

    @pl.when(t_idx == 0)
    def _reset():
        sret_ref[...] = jnp.zeros_like(sret_ref)
        sdn_ref[...] = jnp.zeros_like(sdn_ref)
        sgla_ref[...] = jnp.zeros_like(sgla_ref)
        ss5_ref[...] = jnp.zeros_like(ss5_ref)
        for b in batches:
            xc_ref[b] = jnp.zeros((SUBLANES, 768), F32)

    for b in batches:
        x = x_ref[b]
        h_ref[rb(b), :] = _bf(x * lax.rsqrt(jnp.mean(x * x, axis=-1, keepdims=True) + EPS) * gpre_ref[...])

    def projections():
        weights = {"rot": wrot_ref, "main": wmain_ref, "tail": wtail_ref, "gg": wgg_ref}
        for name, dst_ref in (("deltanet", pd_ref), ("s5", ps_ref), ("gla", pg_ref), ("retention", pr_ref)):
            c0 = 0
            for operand, first, width in PROJ_GROUPS[name]:
                dst_ref[:, c0:c0 + width] = _dot(h_ref[...], weights[operand][:, first:first + width])
                c0 += width
                yield None
            yield name

    o_partial = []

    def out_project(branch):
        rows = slice(branch * BRANCH_W, (branch + 1) * BRANCH_W)
        o_partial.append(_dot(y_ref[:, rows], wout_ref[rows, :]))

    hs = hs_ref[...]
    bd = bd_ref[...]
    bd32 = bd.astype(F32)

    def retention():
        rc = min(RET_CHUNK, tb)
        n_rc = tb // rc
        runits = [(b, c, slice(b * tb + c * rc, b * tb + (c + 1) * rc)) for c in range(n_rc) for b in batches]
        ret_q, ret_k, ret_v = {}, {}, {}
        for b, c, sl in runits:
            cs = cos_ref[c * rc:(c + 1) * rc, :]
            sn = sin_ref[c * rc:(c + 1) * rc, :]
            qa = pr_ref[sl, R_Q:R_Q + LANE]
            qb = pr_ref[sl, R_Q + LANE:R_Q + 2 * LANE]
            ka = pr_ref[sl, R_K:R_K + LANE]
            kb = pr_ref[sl, R_K + LANE:R_K + 2 * LANE]
            ret_q[b, c] = jnp.concatenate([qa * cs - qb * sn, qa * sn + qb * cs], axis=1)
            ret_k[b, c] = jnp.concatenate([ka * cs - kb * sn, ka * sn + kb * cs], axis=1)
            ret_v[b, c] = pr_ref[sl, R_V:R_V + BRANCH_W]
            if b == nb - 1:
                yield
        o_intra, kv = {}, {}
        for b, c, sl in runits:
            s_all = _dot_nt(_bf(ret_q[b, c]), _stack_heads(ret_k[b, c], hmqk_ref))
            o_intra[b, c] = _dot(_bf(s_all * rdall_ref[...]), _stack_heads(ret_v[b, c], hm_ref))
            kv[b, c] = bdqk_ref[...] * _dot_tn(_bf(ret_v[b, c]), _bf(ret_k[b, c] * rkdec_ref[...]))
            if b == nb - 1:
                yield
        st = [sret_ref[b] for b in batches]
        o_ret = {}
        for b, c, sl in runits:
            o_ret[b, c] = o_intra[b, c] + _dot_nt(_bf(ret_q[b, c] * rqdec_ref[...]), _bf(st[b]))
            st[b] = st[b] * rcd_ref[...] + kv[b, c]
            if b == nb - 1:
                yield
        for b in batches:
            sret_ref[b] = st[b]
        o = jnp.concatenate([o_ret[b, c] for b in batches for c in range(n_rc)], axis=0)
        y_ref[:, 0:BRANCH_W] = _bf(_head_rmsnorm_gate(o, rnorm_ref[...], pr_ref[:, R_G:R_G + BRANCH_W], hs))
        yield
        out_project(0)

    def deltanet():
        sub = lax.broadcasted_iota(jnp.int32, (tb, 768), 0) % SUBLANES
        for b in batches:
            xb = pd_ref[rb(b), D_QKV:D_QKV + 768]
            xfull = jnp.concatenate([xc_ref[b], xb], axis=0)
            conv = dconv_ref[DN_CONV - 1:DN_CONV, :] * xb
            for s in range(1, DN_CONV):
                rolled = pltpu.roll(xfull.reshape(tb // SUBLANES + 1, SUBLANES, 768), s, axis=1)
                rolled = rolled.reshape(tb + SUBLANES, 768)
                shifted = jnp.where(sub >= s, rolled[SUBLANES:], rolled[0:tb])
                conv = conv + dconv_ref[DN_CONV - 1 - s:DN_CONV - s, :] * shifted
            qkv_ref[rb(b), :] = _silu(conv)
            xc_ref[b] = xb[tb - SUBLANES:tb]
            yield

        q = qkv_ref[:, 0:256]
        k = qkv_ref[:, 256:512]
        v = qkv_ref[:, 512:768]
        ss = _dot(_bf(jnp.concatenate([q * q, k * k], axis=0)), hs)
        qn = q * lax.rsqrt(ss[0:rr] + EPS) * (HEAD_DIM ** -0.5)
        kn = k * lax.rsqrt(ss[rr:2 * rr] + EPS)
        yield
        small = pd_ref[:, D_SMALL:D_SMALL + LANE]
        beta = _dot(_bf(_sigmoid(small)), debeta_ref[...])
        g_s = -darow_ref[...] * _softplus(small + dbias_ref[...])
        gcum_s = jnp.concatenate([_dot_m2(btri_ref[...], g_s[rb(b)]) for b in batches], axis=0)
        yield
        gcum = _dot_x2(gcum_s, deg_ref[...])
        grows = [_dot_m2(selc_ref[...], gcum[rb(b)] * eyer_ref[...]) for b in batches]
        yield
        egc = jnp.exp(gcum)
        kbeta = kn * beta
        vbeta = v * beta
        kbe = kbeta * egc
        qg = qn * egc
        yield

        dec, glast, kgt = [], [], []
        for b, c, sl in units:
            gc = gcum[sl]
            dec.append(jnp.exp(jnp.minimum(gc - grows[b][c:c + 1, :], 0.0)))
            glast.append(gc[CHUNK - 1:CHUNK, :])
            kgt.append(_bf((kn[sl] * jnp.exp(glast[-1] - gc)).T))
            if b == nb - 1:
                yield
        aa = []
        for b, _, sl in units:
            aa.append(_dot_nt(_bf(jnp.concatenate([kbeta[sl], qn[sl]], axis=0)), _stack_heads(kn[sl], hm_ref)))
            if b == nb - 1:
                yield
        attn = [aa[u][CHUNK:2 * CHUNK] * dec[u] * tril_ref[...] for u in range(n_units)]

        def blockdiag(m):
            mb = _bf(m)
            return jnp.concatenate([mb, mb, mb, mb], axis=0) * bd

        pw = [-(aa[u][0:CHUNK] * dec[u] * strict_ref[...]) for u in range(n_units)]
        t_all = [eye_ref[...] + pw[u] for u in range(n_units)]
        yield
        pw = [_dot(_bf(pw[u]), blockdiag(pw[u])) for u in range(n_units)]
        yield
        for level in range(1, 5):
            both = [_dot(_bf(jnp.concatenate([pw[u], t_all[u]], axis=0)), blockdiag(pw[u]))
                    for u in range(n_units)]
            pw = [both[u][0:CHUNK] for u in range(n_units)]
            t_all = [t_all[u] + both[u][CHUNK:2 * CHUNK] for u in range(n_units)]
            yield
        t_all = [t_all[u] + _dot(_bf(t_all[u]), blockdiag(pw[u])) for u in range(n_units)]
        yield
        uw = []
        for u, (b, _, sl) in enumerate(units):
            uw.append(_dot(_bf(t_all[u]), jnp.concatenate([_stack_heads(vbeta[sl], hm_ref),
                                                           _stack_heads(kbe[sl], hm_ref)], axis=1)))
            if b == nb - 1:
                yield

        st = [sdn_ref[b] for b in batches]
        o_parts = {}
        for u, (b, c, sl) in enumerate(units):
            wq = _dot(_bf(jnp.concatenate([uw[u][:, 256:512], qg[sl]], axis=0)), _bf(st[b]))
            v_new = uw[u][:, 0:256] - wq[0:CHUNK]
            o_parts[(b, c)] = wq[CHUNK:2 * CHUNK] + _dot(_bf(attn[u]), _stack_heads(v_new, hm_ref))
            st[b] = st[b] * jnp.exp(glast[u]) + bd32 * _dot(kgt[u], _bf(v_new))
            if b == nb - 1:
                yield
        for b in batches:
            sdn_ref[b] = st[b]
        o = jnp.concatenate([o_parts[(b, c)] for b in batches for c in range(n_chunks)], axis=0)
        y_ref[:, BRANCH_W:2 * BRANCH_W] = _bf(
            _head_rmsnorm_gate(o, dnorm_ref[...], pd_ref[:, D_G:D_G + BRANCH_W], hs))
        yield
        out_project(1)

    def s5():
        u_in = ps_ref[:, S_U:S_U + BRANCH_W]
        row_in_chunk = lax.broadcasted_iota(jnp.int32, (rr, BRANCH_W), 0) % S5_LAGS
        ucat_ref[:, 0:BRANCH_W] = _bf(u_in)
        for tau in range(1, S5_LAGS):
            shifted = pltpu.roll(u_in.reshape(rr // S5_LAGS, S5_LAGS, BRANCH_W), tau, axis=1).reshape(rr, BRANCH_W)
            ucat_ref[:, tau * BRANCH_W:(tau + 1) * BRANCH_W] = _bf(jnp.where(row_in_chunk >= tau, shifted, 0.0))
            if tau % 4 == 0:
                yield
        yield
        u_perm = _bf(_dot(sperm_ref[...], _bf(u_in)))
        xx = _dot(u_perm, sbbar_ref[...])
        yield
        a1_re = sa_ref[0:1, :]
        a1_im = sa_ref[1:2, :]
        n_cb = nb * n_s5
        z_re = xx[0:n_cb, 0:S5_HALF]
        z_im = xx[0:n_cb, S5_HALF:2 * S5_HALF]
        for j in range(1, S5_LAGS):
            xj_re = xx[j * n_cb:(j + 1) * n_cb, 0:S5_HALF]
            xj_im = xx[j * n_cb:(j + 1) * n_cb, S5_HALF:2 * S5_HALF]
            z_re, z_im = a1_re * z_re - a1_im * z_im + xj_re, a1_re * z_im + a1_im * z_re + xj_im
            if j % 2 == 1:
                yield
        yield
        y_s5 = jnp.concatenate([_dot(ucat_ref[rb(b), :], skcat_ref[...]) for b in batches], axis=0)
        yield
        grp_rows = n_grp * nb
        al_re = sa_ref[2:3, :]
        al_im = sa_ref[3:4, :]
        s_re = jnp.zeros((grp_rows, S5_HALF), F32)
        s_im = jnp.zeros((grp_rows, S5_HALF), F32)
        loc_re, loc_im = [], []
        for m in range(n_stp):
            loc_re.append(s_re)
            loc_im.append(s_im)
            zm_re = z_re[m * grp_rows:(m + 1) * grp_rows]
            zm_im = z_im[m * grp_rows:(m + 1) * grp_rows]
            s_re, s_im = al_re * s_re - al_im * s_im + zm_re, al_re * s_im + al_im * s_re + zm_im
            if m % 2 == 1:
                yield
        ag_re = scar_ref[2 * n_stp:2 * n_stp + 1, :]
        ag_im = scar_ref[2 * n_stp + 1:2 * n_stp + 2, :]
        c_re = ss5_ref[0]
        c_im = ss5_ref[1]
        car_re, car_im = [], []
        for g in range(n_grp):
            car_re.append(c_re)
            car_im.append(c_im)
            e_re = s_re[g * nb:(g + 1) * nb]
            e_im = s_im[g * nb:(g + 1) * nb]
            c_re, c_im = ag_re * c_re - ag_im * c_im + e_re, ag_re * c_im + ag_im * c_re + e_im
        ss5_ref[0] = c_re
        ss5_ref[1] = c_im
        yield
        car_re = jnp.concatenate(car_re, axis=0)
        car_im = jnp.concatenate(car_im, axis=0)
        sp_re, sp_im = [], []
        for m in range(n_stp):
            p_re = scar_ref[m:m + 1, :]
            p_im = scar_ref[n_stp + m:n_stp + m + 1, :]
            sp_re.append(loc_re[m] + p_re * car_re - p_im * car_im)
            sp_im.append(loc_im[m] + p_re * car_im + p_im * car_re)
        yield
        sp_re = jnp.concatenate(sp_re, axis=0)[:, None, :]
        sp_im = jnp.concatenate(sp_im, axis=0)[:, None, :]
        t_re = stre_ref[...][None]
        t_im = stim_ref[...][None]

        def batch_major(v):
            tiles = []
            for b in batches:
                for n in range(n_s5):
                    g, m = divmod(n, n_stp)
                    src = (m * n_grp + g) * nb + b
                    tiles.append(v[src * S5_LAGS:(src + 1) * S5_LAGS])
            return jnp.concatenate(tiles, axis=0)

        v_re = batch_major((t_re * sp_re - t_im * sp_im).reshape(rr, S5_HALF))
        yield
        v_im = batch_major((t_re * sp_im + t_im * sp_re).reshape(rr, S5_HALF))
        yield
        vv = _bf(jnp.concatenate([v_re, v_im], axis=1))
        y_s5 = y_s5 + jnp.concatenate([_dot(vv[rb(b)], scbd_ref[...]) for b in batches], axis=0)
        yield
        c0 = math.sqrt(2.0 / math.pi)
        y_s5 = 0.5 * y_s5 * (1.0 + jnp.tanh(c0 * (y_s5 + 0.044715 * (y_s5 * y_s5 * y_s5))))
        y_s5 = y_s5 * _sigmoid(_dot(_bf(y_s5), swglu_ref[...]) + sbglu_ref[...])
        y_ref[:, 2 * BRANCH_W:3 * BRANCH_W] = _bf(y_s5 * _silu(ps_ref[:, S_G:S_G + BRANCH_W]))
        yield
        out_project(2)

    def gla():
        z = _dot(_bf(pg_ref[:, L_CODE:L_CODE + LANE]), gwgk_ref[...]) + gb_ref[...]
        gk = -_softplus(-z) * (1.0 / GLA_GATE_TAU)
        yield
        cum = jnp.concatenate([_dot_m2(btri_ref[...], gk[rb(b)]) for b in batches], axis=0)
        yield
        q = pg_ref[:, L_Q:L_Q + GLA_QK]
        k = pg_ref[:, L_K:L_K + GLA_QK]
        v = pg_ref[:, L_V:L_V + BRANCH_W]
        qt = _bf(q * jnp.exp(cum) * (GLA_DK ** -0.5))
        kt = k * jnp.exp(-cum)
        clast = [cum[sl][CHUNK - 1:CHUNK, :] for _, _, sl in units]
        yield
        s_all = []
        for b, _, sl in units:
            s_all.append(_dot_nt(qt[sl], _stack_heads(kt[sl], hmg_ref)))
            if b == nb - 1:
                yield
        o_intra = []
        for u, (b, _, sl) in enumerate(units):
            o_intra.append(_dot(_bf(s_all[u] * tril_ref[...]), _stack_heads(v[sl], hm_ref)))
            if b == nb - 1:
                yield
        kv = []
        for u, (b, _, sl) in enumerate(units):
            kv.append(bdg_ref[...] * _dot_tn(_bf(v[sl]), _bf(k[sl] * jnp.exp(clast[u] - cum[sl]))))
            if b == nb - 1:
                yield
        st = [sgla_ref[b] for b in batches]
        o_parts = {}
        for u, (b, c, sl) in enumerate(units):
            o_parts[(b, c)] = o_intra[u] + _dot_nt(qt[sl], _bf(st[b]))
            st[b] = st[b] * jnp.exp(clast[u]) + kv[u]
            if b == nb - 1:
                yield
        for b in batches:
            sgla_ref[b] = st[b]
        o = jnp.concatenate([o_parts[(b, c)] for b in batches for c in range(n_chunks)], axis=0)
        y_ref[:, 3 * BRANCH_W:4 * BRANCH_W] = _bf(
            _head_rmsnorm_gate(o, gnorm_ref[...], pg_ref[:, L_G:L_G + BRANCH_W], hs))
        yield
        out_project(3)

    _interleave(projections(), {"s5": s5(), "deltanet": deltanet(), "retention": retention(), "gla": gla()})

    o = o_partial[0] + o_partial[1] + o_partial[2] + o_partial[3]
    o = o * lax.rsqrt(jnp.mean(o * o, axis=-1, keepdims=True) + EPS) * gpost_ref[...]
    for b in batches:
        out_ref[b] = x_ref[b] + o[rb(b)]


def _in_proj_operands(w_in):
    dp, dm, _ = w_in.shape
    main = _bf(w_in)

    def deinterleave(o):
        t = w_in[:, :, o:o + 256].reshape(dp, dm, N_HEADS, HEAD_DIM // 2, 2)
        return jnp.transpose(t, (0, 1, 4, 2, 3)).reshape(dp, dm, 256)

    rot = _bf(jnp.concatenate([deinterleave(O_RQ) * (HEAD_DIM ** -0.5), deinterleave(O_RK)], axis=2))
    return dict(wrot=rot, wmain=main, wtail=main[:, :, O_DG:O_DG + TAIL_COLS], wgg=main[:, :, O_GG:O_GG + 256])


@functools.lru_cache(maxsize=None)
def _constant_tables(seq_len, tb, nb):
    c = CHUNK
    lane = np.arange(256)
    head_std = lane // 64
    head_qk = (lane % 128) // 32
    head_g = np.arange(128) // 32
    i = np.arange(c)[:, None]
    j = np.arange(256)[None, :] % 64
    t = {}
    heads = np.arange(4)[:, None, None]
    t["hm"] = np.broadcast_to(head_std[None, None, :] == heads, (4, tb, 256)).astype(np.float32)
    t["hmqk"] = np.broadcast_to(head_qk[None, None, :] == heads, (4, tb, 256)).astype(np.float32)
    t["hmg"] = np.broadcast_to(head_g[None, None, :] == heads, (4, c, 128)).astype(np.float32)
    t["hs"] = (head_std[:, None] == head_std[None, :]).astype(np.float32)
    t["bd"] = t["hs"]
    t["bdqk"] = (head_std[:, None] == head_qk[None, :]).astype(np.float32)
    t["bdg"] = (head_std[:, None] == head_g[None, :]).astype(np.float32)
    t["tril"] = (i >= j).astype(np.float32)
    t["strict"] = (i > j).astype(np.float32)
    t["eye"] = (i == j).astype(np.float32)
    r = np.arange(tb)
    same_chunk = (r[:, None] // c) == (r[None, :] // c)
    t["btri"] = (same_chunk & (r[:, None] >= r[None, :])).astype(np.float32)
    t["selc"] = ((r[None, :] // c) == np.arange(16)[:, None]).astype(np.float32)
    n_grp, n_stp = _s5_groups(tb, nb)
    jj, mm, gg, bb = np.meshgrid(np.arange(S5_LAGS), np.arange(n_stp), np.arange(n_grp), np.arange(nb),
                                 indexing="ij")
    old = (bb * tb + (gg * n_stp + mm) * S5_LAGS + jj).reshape(-1)
    sperm = np.zeros((nb * tb, nb * tb), np.float32)
    sperm[np.arange(nb * tb), old] = 1.0
    t["sperm"] = sperm
    t["eyer"] = np.tile(t["eye"], (tb // c, 1))
    rc = min(RET_CHUNK, tb)
    lg = np.log(1.0 - 2.0 ** (-5.0 - np.arange(4, dtype=np.float64)))
    ri = np.arange(rc)[:, None]
    rj = np.arange(4 * rc)[None, :] % rc
    lg_cols = lg[np.arange(4 * rc) // rc][None, :]
    t["rdall"] = np.where(ri >= rj, np.exp(lg_cols * np.where(ri >= rj, ri - rj, 0)), 0.0).astype(np.float32)
    lg_qk = lg[head_qk][None, :]
    t["rqdec"] = np.exp(lg_qk * (ri + 1.0)).astype(np.float32)
    t["rkdec"] = np.exp(lg_qk * (rc - 1.0 - ri)).astype(np.float32)
    t["rcd"] = np.exp(lg_qk * rc).astype(np.float32)
    inv = ROPE_BASE ** (-np.arange(0, HEAD_DIM, 2, dtype=np.float64) / HEAD_DIM)
    ang = np.arange(seq_len, dtype=np.float64)[:, None] * inv[None, :]
    t["cos"] = np.tile(np.cos(ang), (1, 4)).astype(np.float32)
    t["sin"] = np.tile(np.sin(ang), (1, 4)).astype(np.float32)
    t["tile16"] = np.tile(np.eye(S5_GROUP, dtype=np.float32), (1, S5_GROUPS))
    t["tile64"] = np.tile(np.eye(S5_STATE, dtype=np.float32), (1, S5_GROUPS))
    return t


def _s5_tables(lam_re, lam_im, b_re, b_im, c_re, c_im, d, log_dt, tabs, n_stp):
    hp = lax.Precision.HIGHEST
    g, p, hc = S5_GROUPS, S5_STATE, S5_GROUP
    nl = lam_re.shape[0]
    lam_re, lam_im = lam_re.astype(F32), lam_im.astype(F32)
    dt = jnp.exp(log_dt.astype(F32))[..., None]
    mag = jnp.exp(lam_re * dt)
    ang = lam_im * dt
    a_re, a_im = mag * jnp.cos(ang), mag * jnp.sin(ang)
    den = lam_re * lam_re + lam_im * lam_im
    nr, ni = a_re - 1.0, a_im
    coef_re = (nr * lam_re + ni * lam_im) / den
    coef_im = (ni * lam_re - nr * lam_im) / den
    b_re, b_im = b_re.astype(F32), b_im.astype(F32)
    bb_re = coef_re[..., None] * b_re - coef_im[..., None] * b_im
    bb_im = coef_re[..., None] * b_im + coef_im[..., None] * b_re

    def apow(n):
        n = jnp.asarray(n, F32)[None, :, None, None]
        m = jnp.exp((lam_re * dt)[:, None] * n)
        return m * jnp.cos(ang[:, None] * n), m * jnp.sin(ang[:, None] * n)

    lags = np.arange(S5_LAGS)
    p_re, p_im = apow(lags)
    c_re, c_im = c_re.astype(F32), c_im.astype(F32)
    ab_re = p_re[..., None] * bb_re[:, None] - p_im[..., None] * bb_im[:, None]
    ab_im = p_re[..., None] * bb_im[:, None] + p_im[..., None] * bb_re[:, None]
    kk = (jnp.einsum('ntgpi,ngop->ntgio', ab_re, c_re, precision=hp)
          - jnp.einsum('ntgpi,ngop->ntgio', ab_im, c_im, precision=hp))
    kk = kk.at[:, 0].add(d.astype(F32)[..., None] * jnp.eye(hc, dtype=F32))
    grp256 = np.arange(g * hc) // hc
    grp1024 = np.arange(g * p) // p
    tile16 = jnp.asarray(tabs["tile16"])
    tile64 = jnp.asarray(tabs["tile64"])
    kcat = jnp.einsum('nro,oc->nrc', kk.reshape(nl, S5_LAGS * g * hc, hc), tile16, precision=hp)
    kcat = jnp.where(jnp.asarray(np.tile(grp256, S5_LAGS)[:, None] == grp256[None, :]), kcat, 0.0)

    def rows_to_state(bb):
        m = jnp.einsum('nrp,pc->nrc', jnp.transpose(bb, (0, 1, 3, 2)).reshape(nl, g * hc, p), tile64,
                       precision=hp)
        return jnp.where(jnp.asarray(grp256[:, None] == grp1024[None, :]), m, 0.0)

    def state_to_rows(cc):
        m = jnp.einsum('nro,oc->nrc', jnp.transpose(cc, (0, 1, 3, 2)).reshape(nl, g * p, hc), tile16,
                       precision=hp)
        return jnp.where(jnp.asarray(grp1024[:, None] == grp256[None, :]), m, 0.0)

    bbar = jnp.concatenate([rows_to_state(bb_re), rows_to_state(bb_im)], axis=2)
    cbd = jnp.concatenate([state_to_rows(c_re), -state_to_rows(c_im)], axis=1)
    t_re, t_im = apow(lags + 1)
    a1_re, a1_im = apow(np.array([1]))
    al_re, al_im = apow(np.array([S5_LAGS]))
    pm_re, pm_im = apow(S5_LAGS * np.arange(n_stp))
    ag_re, ag_im = apow(np.array([S5_LAGS * n_stp]))
    flat = lambda z: z.reshape(nl, z.shape[1], g * p)
    return dict(skcat=_bf(kcat), sbbar=_bf(bbar), scbd=_bf(cbd),
                stre=flat(t_re), stim=flat(t_im),
                sa=jnp.concatenate([flat(a1_re), flat(a1_im), flat(al_re), flat(al_im)], axis=1),
                scar=jnp.concatenate([flat(pm_re), flat(pm_im), flat(ag_re), flat(ag_im)], axis=1))


def _const_spec(arr):
    nd = arr.ndim
    return pl.BlockSpec(arr.shape, lambda b, t, _nd=nd: (0,) * _nd)


def _layer_spec(arr, layer):
    nd = arr.ndim - 1
    return pl.BlockSpec((None,) + arr.shape[1:], lambda b, t, _nd=nd, _l=layer: (_l,) + (0,) * _nd)


def _layer_call(x, layer, stacked, const_inputs, cos, sin, tb, nb):
    bsz, seq, _ = x.shape
    grid = (bsz // nb, seq // tb)
    rr = nb * tb
    in_specs = [pl.BlockSpec((nb, tb, D_MODEL), lambda b, t: (b, t, 0)),
                pl.BlockSpec((tb, LANE), lambda b, t: (t, 0)),
                pl.BlockSpec((tb, LANE), lambda b, t: (t, 0))]
    operands = [x, cos, sin]
    order = ["gpre", "gpost", "wrot", "wmain", "wtail", "wgg", "wout",
             "hm", "hmqk", "hmg", "hs", "bd", "bdqk", "bdg", "tril", "strict", "eye", "eyer", "btri", "selc", "sperm",
             "rdall", "rqdec", "rkdec", "rcd", "rnorm",
             "dconv", "darow", "dbias", "debeta", "deg", "dnorm",
             "skcat", "sbbar", "scbd", "stre", "stim", "sa", "scar", "swglu", "sbglu",
             "gwgk", "gb", "gnorm"]
    for name in order:
        if name in stacked:
            operands.append(stacked[name])
            in_specs.append(_layer_spec(stacked[name], layer))
        else:
            operands.append(const_inputs[name])
            in_specs.append(_const_spec(const_inputs[name]))
    scratch = [
        pltpu.VMEM((rr, D_MODEL), BF16),
        pltpu.VMEM((rr, G_RET), F32),
        pltpu.VMEM((rr, G_DN), F32),
        pltpu.VMEM((rr, G_S5), F32),
        pltpu.VMEM((rr, G_GLA), F32),
        pltpu.VMEM((rr, D_MODEL), BF16),
        pltpu.VMEM((nb, SUBLANES, 768), F32),
        pltpu.VMEM((rr, 768), F32),
        pltpu.VMEM((rr, S5_LAGS * BRANCH_W), BF16),
        pltpu.VMEM((nb, BRANCH_W, BRANCH_W), F32),
        pltpu.VMEM((nb, BRANCH_W, BRANCH_W), F32),
        pltpu.VMEM((nb, BRANCH_W, GLA_QK), F32),
        pltpu.VMEM((2, nb, S5_HALF), F32),
    ]
    return pl.pallas_call(
        functools.partial(_layer_kernel, tb=tb, nb=nb),
        grid=grid,
        in_specs=in_specs,
        out_specs=pl.BlockSpec((nb, tb, D_MODEL), lambda b, t: (b, t, 0)),
        out_shape=jax.ShapeDtypeStruct(x.shape, x.dtype),
        scratch_shapes=scratch,
        compiler_params=pltpu.CompilerParams(
            dimension_semantics=("arbitrary", "arbitrary"),
            vmem_limit_bytes=VMEM_LIMIT_BYTES),
        name="hybrid_layer",
    )(*operands)


def _pick_tiles(bsz, seq):
    nb = 2 if bsz % 2 == 0 else 1
    for tb in (256, 128, 64):
        if seq % tb == 0:
            return tb, nb
    raise ValueError(f"sequence length {seq} must be a multiple of {CHUNK}")


def kernel(x, norm_pre, norm_post, w_in, w_out, ret_norm, dn_conv, dn_a_log, dn_dt_bias, dn_norm,
           s5_lam_re, s5_lam_im, s5_b_re, s5_b_im, s5_c_re, s5_c_im, s5_d, s5_log_dt, s5_w_glu, s5_b_glu,
           gla_w_gk, gla_b_gk, gla_norm):
    bsz, seq, dm = x.shape
    depth = w_in.shape[0]
    assert dm == D_MODEL and x.dtype == F32
    tb, nb = _pick_tiles(bsz, seq)
    tabs = _constant_tables(seq, tb, nb)
    bf_names = ("hm", "hmqk", "hmg", "hs", "bd", "btri", "selc", "sperm")
    skip = ("cos", "sin", "tile16", "tile64")
    const_inputs = {k: jnp.asarray(v, BF16 if k in bf_names else F32)
                    for k, v in tabs.items() if k not in skip}
    cos = jnp.asarray(tabs["cos"])
    sin = jnp.asarray(tabs["sin"])

    w_in_ops = _in_proj_operands(w_in)
    w_out_b = _bf(w_out)

    e_beta = np.zeros((LANE, 256), np.float32)
    e_g = np.zeros((LANE, 256), np.float32)
    for hh in range(N_HEADS):
        e_beta[SM_BETA + hh, 64 * hh:64 * hh + 64] = 1.0
        e_g[SM_A + hh, 64 * hh:64 * hh + 64] = 1.0
    e_beta = jnp.asarray(e_beta, BF16)
    e_g = jnp.asarray(e_g, BF16)

    a_rows = jnp.zeros((depth, 1, LANE), F32).at[:, 0, SM_A:SM_A + 4].set(jnp.exp(dn_a_log.astype(F32)))
    b_rows = jnp.zeros((depth, 1, LANE), F32).at[:, 0, SM_A:SM_A + 4].set(dn_dt_bias.astype(F32))
    wgk = jnp.zeros((depth, LANE, GLA_QK), BF16).at[:, 0:GLA_GATE_RANK, :].set(_bf(gla_w_gk))
    s5 = _s5_tables(s5_lam_re, s5_lam_im, s5_b_re, s5_b_im, s5_c_re, s5_c_im, s5_d, s5_log_dt, tabs,
                    _s5_groups(tb, nb)[1])
    tile4 = lambda g: jnp.tile(g.astype(F32), (1, N_HEADS)).reshape(depth, 1, 256)
    rnorm, dnorm, gnorm = tile4(ret_norm), tile4(dn_norm), tile4(gla_norm)
    swglu = _bf(s5_w_glu)

    stacked = dict(
        gpre=norm_pre.reshape(depth, 1, dm).astype(F32),
        gpost=norm_post.reshape(depth, 1, dm).astype(F32),
        wout=w_out_b,
        rnorm=rnorm,
        dconv=dn_conv.astype(F32),
        darow=a_rows, dbias=b_rows,
        dnorm=dnorm,
        swglu=swglu, sbglu=s5_b_glu.reshape(depth, 1, 256).astype(F32),
        gwgk=wgk, gb=gla_b_gk.reshape(depth, 1, GLA_QK).astype(F32),
        gnorm=gnorm,
    )
    stacked.update(s5)
    stacked.update(w_in_ops)
    const_inputs.update(debeta=e_beta, deg=e_g)
    for i in range(depth):
        x = _layer_call(x, i, stacked, const_inputs, cos, sin, tb, nb)
    return x
```

```python
import functools
import math

import jax
import jax.numpy as jnp
import numpy as np
from jax import lax
from jax.experimental import pallas as pl
from jax.experimental.pallas import tpu as pltpu

F32 = jnp.float32
BF16 = jnp.bfloat16

D_MODEL = 1024
BRANCH_W = 256
N_HEADS = 4
HEAD_DIM = 64
EPS = 1e-6
ROPE_BASE = 10000.0
DN_CONV = 4
S5_GROUP = 16
S5_GROUPS = 16
S5_STATE = 64
S5_HALF = S5_GROUPS * S5_STATE
GLA_DK = 32
GLA_QK = 128
GLA_GATE_RANK = 16
GLA_GATE_TAU = 16.0
IN_SPLITS = [256, 256, 256, 256, 768, 4, 4, 256, 256, 256, 128, 128, 256, 16, 256]

CHUNK = 64
RET_CHUNK = 128
S5_LAGS = 8
LANE = 128
SUBLANES = 8
VMEM_LIMIT_BYTES = 56 * 1024 * 1024

_OFF = np.concatenate([[0], np.cumsum(IN_SPLITS)])
(O_RQ, O_RK, O_RV, O_RG, O_DQKV, O_DBETA, O_DA, O_DG, O_SU, O_SG,
 O_GQ, O_GK, O_GV, O_GCODE, O_GG) = (int(o) for o in _OFF[:-1])
assert O_DBETA % LANE == 0 and O_DA == O_DBETA + 4 and O_DG == O_DA + 4
_T = lambda o: o - O_DG
PROJ_GROUPS = {
    "retention": [("rot", 0, 256), ("rot", 256, 256), ("main", O_RV, 256), ("main", O_RG, 256)],
    "deltanet": [("main", O_DQKV, 256), ("main", O_DQKV + 256, 256), ("main", O_DQKV + 512, 256),
                 ("main", O_DBETA, LANE), ("tail", _T(O_DG), 256)],
    "s5": [("tail", _T(O_SU), 256), ("tail", _T(O_SG), 256)],
    "gla": [("tail", _T(O_GQ), 256), ("tail", _T(O_GV), 256), ("tail", _T(O_GCODE), LANE), ("gg", 0, 256)],
}
G_RET, G_DN, G_S5, G_GLA = (sum(w for _, _, w in PROJ_GROUPS[k]) for k in ("retention", "deltanet", "s5", "gla"))
TAIL_COLS = _T(O_GCODE) + LANE
R_Q, R_K, R_V, R_G = 0, 256, 512, 768
D_QKV, D_SMALL, D_G = 0, 768, 896
S_U, S_G = 0, 256
L_Q, L_K, L_V, L_CODE, L_G = 0, 128, 256, 512, 640
SM_BETA, SM_A = 0, 4


def _bf(x):
    return x.astype(BF16)


def _dot(a, b):
    return jnp.dot(a, b, preferred_element_type=F32)


def _dot_nt(a, b):
    return lax.dot_general(a, b, (((1,), (1,)), ((), ())), preferred_element_type=F32)


def _dot_tn(a, b):
    return lax.dot_general(a, b, (((0,), (0,)), ((), ())), preferred_element_type=F32)


def _split2(x):
    x1 = _bf(x)
    x2 = _bf(x - x1.astype(F32))
    return x1, x2


def _dot_x2(x, m):
    x1, x2 = _split2(x)
    return _dot(x1, m) + _dot(x2, m)


def _dot_m2(m, x):
    x1, x2 = _split2(x)
    return _dot(m, x1) + _dot(m, x2)


def _sigmoid(x):
    return 0.5 * jnp.tanh(0.5 * x) + 0.5


def _silu(x):
    h = 0.5 * x
    return h * jnp.tanh(h) + h


def _softplus(x):
    return jnp.maximum(x, 0.0) + jnp.log(1.0 + jnp.exp(-jnp.abs(x)))


def _stack_heads(x, hm_ref):
    xb = _bf(x)
    rows = x.shape[0]
    return jnp.concatenate([xb * hm_ref[h, 0:rows, :] for h in range(N_HEADS)], axis=0)


def _head_rmsnorm_gate(o, g_row, gate, hs):
    ms = _dot(_bf(o * o), hs) * (1.0 / HEAD_DIM)
    return o * lax.rsqrt(ms + EPS) * g_row * _silu(gate)


def _s5_groups(tb, nb):
    n_grp = max(1, SUBLANES // nb)
    n_s5 = tb // S5_LAGS
    assert n_s5 % n_grp == 0
    return n_grp, n_s5 // n_grp


def _interleave(feeder, consumers):
    live = []
    feeding = True
    while feeding or live:
        if feeding:
            try:
                tag = next(feeder)
                while tag is not None:
                    live.append(consumers.pop(tag))
                    tag = next(feeder)
            except StopIteration:
                feeding = False
        for g in list(live):
            try:
                next(g)
            except StopIteration:
                live.remove(g)
    assert not consumers


def _layer_kernel(
        x_ref, cos_ref, sin_ref,
        gpre_ref, gpost_ref, wrot_ref, wmain_ref, wtail_ref, wgg_ref, wout_ref,
        hm_ref, hmqk_ref, hmg_ref, hs_ref, bd_ref, bdqk_ref, bdg_ref,
        tril_ref, strict_ref, eye_ref, eyer_ref, btri_ref, selc_ref, sperm_ref,
        rdall_ref, rqdec_ref, rkdec_ref, rcd_ref, rnorm_ref,
        dconv_ref, darow_ref, dbias_ref, debeta_ref, deg_ref, dnorm_ref,
        skcat_ref, sbbar_ref, scbd_ref, stre_ref, stim_ref, sa_ref, scar_ref,
        swglu_ref, sbglu_ref,
        gwgk_ref, gb_ref, gnorm_ref,
        out_ref,
        h_ref, pr_ref, pd_ref, ps_ref, pg_ref, y_ref, xc_ref, qkv_ref, ucat_ref,
        sret_ref, sdn_ref, sgla_ref, ss5_ref,
        *, tb, nb):
    t_idx = pl.program_id(1)
    n_chunks = tb // CHUNK
    n_s5 = tb // S5_LAGS
    n_grp, n_stp = _s5_groups(tb, nb)
    batches = range(nb)
    rr = nb * tb

    def rb(b):
        return slice(b * tb, (b + 1) * tb)

    units = [(b, c, slice(b * tb + c * CHUNK, b * tb + (c + 1) * CHUNK))
             for c in range(n_chunks) for b in batches]
    n_units = len(units)

    @pl.when(t_idx == 0)
    def _reset():
        sret_ref[...] = jnp.zeros_like(sret_ref)
        sdn_ref[...] = jnp.zeros_like(sdn_ref)
        sgla_ref[...] = jnp.zeros_like(sgla_ref)
        ss5_ref[...] = jnp.zeros_like(ss5_ref)
        for b in batches:
            xc_ref[b] = jnp.zeros((SUBLANES, 768), F32)

    for b in batches:
        x = x_ref[b]
        h_ref[rb(b), :] = _bf(x * lax.rsqrt(jnp.mean(x * x, axis=-1, keepdims=True) + EPS) * gpre_ref[...])

    def projections():
        weights = {"rot": wrot_ref, "main": wmain_ref, "tail": wtail_ref, "gg": wgg_ref}
        for name, dst_ref in (("deltanet", pd_ref), ("s5", ps_ref), ("gla", pg_ref), ("retention", pr_ref)):
            c0 = 0
            for operand, first, width in PROJ_GROUPS[name]:
                dst_ref[:, c0:c0 + width] = _dot(h_ref[...], weights[operand][:, first:first + width])
                c0 += width
                yield None
            yield name

    o_partial = []

    def out_project(branch):
        rows = slice(branch * BRANCH_W, (branch + 1) * BRANCH_W)
        o_partial.append(_dot(y_ref[:, rows], wout_ref[rows, :]))

    hs = hs_ref[...]
    bd = bd_ref[...]
    bd32 = bd.astype(F32)

    def retention():
        rc = min(RET_CHUNK, tb)
        n_rc = tb // rc
        runits = [(b, c, slice(b * tb + c * rc, b * tb + (c + 1) * rc)) for c in range(n_rc) for b in batches]
        ret_q, ret_k, ret_v = {}, {}, {}
        for b, c, sl in runits:
            cs = cos_ref[c * rc:(c + 1) * rc, :]
            sn = sin_ref[c * rc:(c + 1) * rc, :]
            qa = pr_ref[sl, R_Q:R_Q + LANE]
            qb = pr_ref[sl, R_Q + LANE:R_Q + 2 * LANE]
            ka = pr_ref[sl, R_K:R_K + LANE]
            kb = pr_ref[sl, R_K + LANE:R_K + 2 * LANE]
            ret_q[b, c] = jnp.concatenate([qa * cs - qb * sn, qa * sn + qb * cs], axis=1)
            ret_k[b, c] = jnp.concatenate([ka * cs - kb * sn, ka * sn + kb * cs], axis=1)
            ret_v[b, c] = pr_ref[sl, R_V:R_V + BRANCH_W]
            if b == nb - 1:
                yield
        o_intra, kv = {}, {}
        for b, c, sl in runits:
            s_all = _dot_nt(_bf(ret_q[b, c]), _stack_heads(ret_k[b, c], hmqk_ref))
            o_intra[b, c] = _dot(_bf(s_all * rdall_ref[...]), _stack_heads(ret_v[b, c], hm_ref))
            kv[b, c] = bdqk_ref[...] * _dot_tn(_bf(ret_v[b, c]), _bf(ret_k[b, c] * rkdec_ref[...]))
            if b == nb - 1:
                yield
        st = [sret_ref[b] for b in batches]
        o_ret = {}
        for b, c, sl in runits:
            o_ret[b, c] = o_intra[b, c] + _dot_nt(_bf(ret_q[b, c] * rqdec_ref[...]), _bf(st[b]))
            st[b] = st[b] * rcd_ref[...] + kv[b, c]
            if b == nb - 1:
                yield
        for b in batches:
            sret_ref[b] = st[b]
        o = jnp.concatenate([o_ret[b, c] for b in batches for c in range(n_rc)], axis=0)
        y_ref[:, 0:BRANCH_W] = _bf(_head_rmsnorm_gate(o, rnorm_ref[...], pr_ref[:, R_G:R_G + BRANCH_W], hs))
        yield
        out_project(0)

    def deltanet():
        sub = lax.broadcasted_iota(jnp.int32, (tb, 768), 0) % SUBLANES
        for b in batches:
            xb = pd_ref[rb(b), D_QKV:D_QKV + 768]
            xfull = jnp.concatenate([xc_ref[b], xb], axis=0)
            conv = dconv_ref[DN_CONV - 1:DN_CONV, :] * xb
            for s in range(1, DN_CONV):
                rolled = pltpu.roll(xfull.reshape(tb // SUBLANES + 1, SUBLANES, 768), s, axis=1)
                rolled = rolled.reshape(tb + SUBLANES, 768)
                shifted = jnp.where(sub >= s, rolled[SUBLANES:], rolled[0:tb])
                conv = conv + dconv_ref[DN_CONV - 1 - s:DN_CONV - s, :] * shifted
            qkv_ref[rb(b), :] = _silu(conv)
            xc_ref[b] = xb[tb - SUBLANES:tb]
            yield

        q = qkv_ref[:, 0:256]
        k = qkv_ref[:, 256:512]
        v = qkv_ref[:, 512:768]
        ss = _dot(_bf(jnp.concatenate([q * q, k * k], axis=0)), hs)
        qn = q * lax.rsqrt(ss[0:rr] + EPS) * (HEAD_DIM ** -0.5)
        kn = k * lax.rsqrt(ss[rr:2 * rr] + EPS)
        yield
        small = pd_ref[:, D_SMALL:D_SMALL + LANE]
        beta = _dot(_bf(_sigmoid(small)), debeta_ref[...])
        g_s = -darow_ref[...] * _softplus(small + dbias_ref[...])
        gcum_s = jnp.concatenate([_dot_m2(btri_ref[...], g_s[rb(b)]) for b in batches], axis=0)
        yield
        gcum = _dot_x2(gcum_s, deg_ref[...])
        grows = [_dot_m2(selc_ref[...], gcum[rb(b)] * eyer_ref[...]) for b in batches]
        yield
        egc = jnp.exp(gcum)
        kbeta = kn * beta
        vbeta = v * beta
        kbe = kbeta * egc
        qg = qn * egc
        yield

        dec, glast, kgt = [], [], []
        for b, c, sl in units:
            gc = gcum[sl]
            dec.append(jnp.exp(jnp.minimum(gc - grows[b][c:c + 1, :], 0.0)))
            glast.append(gc[CHUNK - 1:CHUNK, :])
            kgt.append(_bf((kn[sl] * jnp.exp(glast[-1] - gc)).T))
            if b == nb - 1:
                yield
        aa = []
        for b, _, sl in units:
            aa.append(_dot_nt(_bf(jnp.concatenate([kbeta[sl], qn[sl]], axis=0)), _stack_heads(kn[sl], hm_ref)))
            if b == nb - 1:
                yield
        attn = [aa[u][CHUNK:2 * CHUNK] * dec[u] * tril_ref[...] for u in range(n_units)]

        def blockdiag(m):
            mb = _bf(m)
            return jnp.concatenate([mb, mb, mb, mb], axis=0) * bd

        pw = [-(aa[u][0:CHUNK] * dec[u] * strict_ref[...]) for u in range(n_units)]
        t_all = [eye_ref[...] + pw[u] for u in range(n_units)]
        yield
        pw = [_dot(_bf(pw[u]), blockdiag(pw[u])) for u in range(n_units)]
        yield
        for level in range(1, 5):
            both = [_dot(_bf(jnp.concatenate([pw[u], t_all[u]], axis=0)), blockdiag(pw[u]))
                    for u in range(n_units)]
            pw = [both[u][0:CHUNK] for u in range(n_units)]
            t_all = [t_all[u] + both[u][CHUNK:2 * CHUNK] for u in range(n_units)]
            yield
        t_all = [t_all[u] + _dot(_bf(t_all[u]), blockdiag(pw[u])) for u in range(n_units)]
        yield
        uw = []
        for u, (b, _, sl) in enumerate(units):
            uw.append(_dot(_bf(t_all[u]), jnp.concatenate([_stack_heads(vbeta[sl], hm_ref),
                                                           _stack_heads(kbe[sl], hm_ref)], axis=1)))
            if b == nb - 1:
                yield

        st = [sdn_ref[b] for b in batches]
        o_parts = {}
        for u, (b, c, sl) in enumerate(units):
            wq = _dot(_bf(jnp.concatenate([uw[u][:, 256:512], qg[sl]], axis=0)), _bf(st[b]))
            v_new = uw[u][:, 0:256] - wq[0:CHUNK]
            o_parts[(b, c)] = wq[CHUNK:2 * CHUNK] + _dot(_bf(attn[u]), _stack_heads(v_new, hm_ref))
            st[b] = st[b] * jnp.exp(glast[u]) + bd32 * _dot(kgt[u], _bf(v_new))
            if b == nb - 1:
                yield
        for b in batches:
            sdn_ref[b] = st[b]
        o = jnp.concatenate([o_parts[(b, c)] for b in batches for c in range(n_chunks)], axis=0)
        y_ref[:, BRANCH_W:2 * BRANCH_W] = _bf(
            _head_rmsnorm_gate(o, dnorm_ref[...], pd_ref[:, D_G:D_G + BRANCH_W], hs))
        yield
        out_project(1)

    def s5():
        u_in = ps_ref[:, S_U:S_U + BRANCH_W]
        row_in_chunk = lax.broadcasted_iota(jnp.int32, (rr, BRANCH_W), 0) % S5_LAGS
        ucat_ref[:, 0:BRANCH_W] = _bf(u_in)
        for tau in range(1, S5_LAGS):
            shifted = pltpu.roll(u_in.reshape(rr // S5_LAGS, S5_LAGS, BRANCH_W), tau, axis=1).reshape(rr, BRANCH_W)
            ucat_ref[:, tau * BRANCH_W:(tau + 1) * BRANCH_W] = _bf(jnp.where(row_in_chunk >= tau, shifted, 0.0))
            if tau % 4 == 0:
                yield
        yield
        u_perm = _bf(_dot(sperm_ref[...], _bf(u_in)))
        xx = _dot(u_perm, sbbar_ref[...])
        yield
        a1_re = sa_ref[0:1, :]
        a1_im = sa_ref[1:2, :]
        n_cb = nb * n_s5
        z_re = xx[0:n_cb, 0:S5_HALF]
        z_im = xx[0:n_cb, S5_HALF:2 * S5_HALF]
        for j in range(1, S5_LAGS):
            xj_re = xx[j * n_cb:(j + 1) * n_cb, 0:S5_HALF]
            xj_im = xx[j * n_cb:(j + 1) * n_cb, S5_HALF:2 * S5_HALF]
            z_re, z_im = a1_re * z_re - a1_im * z_im + xj_re, a1_re * z_im + a1_im * z_re + xj_im
            if j % 2 == 1:
                yield
        yield
        y_s5 = jnp.concatenate([_dot(ucat_ref[rb(b), :], skcat_ref[...]) for b in batches], axis=0)
        yield
        grp_rows = n_grp * nb
        al_re = sa_ref[2:3, :]
        al_im = sa_ref[3:4, :]
        s_re = jnp.zeros((grp_rows, S5_HALF), F32)
        s_im = jnp.zeros((grp_rows, S5_HALF), F32)
        loc_re, loc_im = [], []
        for m in range(n_stp):
            loc_re.append(s_re)
            loc_im.append(s_im)
            zm_re = z_re[m * grp_rows:(m + 1) * grp_rows]
            zm_im = z_im[m * grp_rows:(m + 1) * grp_rows]
            s_re, s_im = al_re * s_re - al_im * s_im + zm_re, al_re * s_im + al_im * s_re + zm_im
            if m % 2 == 1:
                yield
        ag_re = scar_ref[2 * n_stp:2 * n_stp + 1, :]
        ag_im = scar_ref[2 * n_stp + 1:2 * n_stp + 2, :]
        c_re = ss5_ref[0]
        c_im = ss5_ref[1]
        car_re, car_im = [], []
        for g in range(n_grp):
            car_re.append(c_re)
            car_im.append(c_im)
            e_re = s_re[g * nb:(g + 1) * nb]
            e_im = s_im[g * nb:(g + 1) * nb]
            c_re, c_im = ag_re * c_re - ag_im * c_im + e_re, ag_re * c_im + ag_im * c_re + e_im
        ss5_ref[0] = c_re
        ss5_ref[1] = c_im
        yield
        car_re = jnp.concatenate(car_re, axis=0)
        car_im = jnp.concatenate(car_im, axis=0)
        sp_re, sp_im = [], []
        for m in range(n_stp):
            p_re = scar_ref[m:m + 1, :]
            p_im = scar_ref[n_stp + m:n_stp + m + 1, :]
            sp_re.append(loc_re[m] + p_re * car_re - p_im * car_im)
            sp_im.append(loc_im[m] + p_re * car_im + p_im * car_re)
        yield
        sp_re = jnp.concatenate(sp_re, axis=0)[:, None, :]
        sp_im = jnp.concatenate(sp_im, axis=0)[:, None, :]
        t_re = stre_ref[...][None]
        t_im = stim_ref[...][None]

        def batch_major(v):
            tiles = []
            for b in batches:
                for n in range(n_s5):
                    g, m = divmod(n, n_stp)
                    src = (m * n_grp + g) * nb + b
                    tiles.append(v[src * S5_LAGS:(src + 1) * S5_LAGS])
            return jnp.concatenate(tiles, axis=0)

        v_re = batch_major((t_re * sp_re - t_im * sp_im).reshape(rr, S5_HALF))
        yield
        v_im = batch_major((t_re * sp_im + t_im * sp_re).reshape(rr, S5_HALF))
        yield
        vv = _bf(jnp.concatenate([v_re, v_im], axis=1))
        y_s5 = y_s5 + jnp.concatenate([_dot(vv[rb(b)], scbd_ref[...]) for b in batches], axis=0)
        yield
        c0 = math.sqrt(2.0 / math.pi)
        y_s5 = 0.5 * y_s5 * (1.0 + jnp.tanh(c0 * (y_s5 + 0.044715 * (y_s5 * y_s5 * y_s5))))
        y_s5 = y_s5 * _sigmoid(_dot(_bf(y_s5), swglu_ref[...]) + sbglu_ref[...])
        y_ref[:, 2 * BRANCH_W:3 * BRANCH_W] = _bf(y_s5 * _silu(ps_ref[:, S_G:S_G + BRANCH_W]))
        yield
        out_project(2)

    def gla():
        z = _dot(_bf(pg_ref[:, L_CODE:L_CODE + LANE]), gwgk_ref[...]) + gb_ref[...]
        gk = -_softplus(-z) * (1.0 / GLA_GATE_TAU)
        yield
        cum = jnp.concatenate([_dot_m2(btri_ref[...], gk[rb(b)]) for b in batches], axis=0)
        yield
        q = pg_ref[:, L_Q:L_Q + GLA_QK]
        k = pg_ref[:, L_K:L_K + GLA_QK]
        v = pg_ref[:, L_V:L_V + BRANCH_W]
        qt = _bf(q * jnp.exp(cum) * (GLA_DK ** -0.5))
        kt = k * jnp.exp(-cum)
        clast = [cum[sl][CHUNK - 1:CHUNK, :] for _, _, sl in units]
        yield
        s_all = []
        for b, _, sl in units:
            s_all.append(_dot_nt(qt[sl], _stack_heads(kt[sl], hmg_ref)))
            if b == nb - 1:
                yield
        o_intra = []
        for u, (b, _, sl) in enumerate(units):
            o_intra.append(_dot(_bf(s_all[u] * tril_ref[...]), _stack_heads(v[sl], hm_ref)))
            if b == nb - 1:
                yield
        kv = []
        for u, (b, _, sl) in enumerate(units):
            kv.append(bdg_ref[...] * _dot_tn(_bf(v[sl]), _bf(k[sl] * jnp.exp(clast[u] - cum[sl]))))
            if b == nb - 1:
                yield
        st = [sgla_ref[b] for b in batches]
        o_parts = {}
        for u, (b, c, sl) in enumerate(units):
            o_parts[(b, c)] = o_intra[u] + _dot_nt(qt[sl], _bf(st[b]))
            st[b] = st[b] * jnp.exp(clast[u]) + kv[u]
            if b == nb - 1:
                yield
        for b in batches:
            sgla_ref[b] = st[b]
        o = jnp.concatenate([o_parts[(b, c)] for b in batches for c in range(n_chunks)], axis=0)
        y_ref[:, 3 * BRANCH_W:4 * BRANCH_W] = _bf(
            _head_rmsnorm_gate(o, gnorm_ref[...], pg_ref[:, L_G:L_G + BRANCH_W], hs))
        yield
        out_project(3)

    _interleave(projections(), {"s5": s5(), "deltanet": deltanet(), "retention": retention(), "gla": gla()})

    o = o_partial[0] + o_partial[1] + o_partial[2] + o_partial[3]
    o = o * lax.rsqrt(jnp.mean(o * o, axis=-1, keepdims=True) + EPS) * gpost_ref[...]
    for b in batches:
        out_ref[b] = x_ref[b] + o[rb(b)]


def _in_proj_operands(w_in):
    dp, dm, _ = w_in.shape
    main = _bf(w_in)

    def deinterleave(o):
        t = w_in[:, :, o:o + 256].reshape(dp, dm, N_HEADS, HEAD_DIM // 2, 2)
        return jnp.transpose(t, (0, 1, 4, 2, 3)).reshape(dp, dm, 256)

    rot = _bf(jnp.concatenate([deinterleave(O_RQ) * (HEAD_DIM ** -0.5), deinterleave(O_RK)], axis=2))
    return dict(wrot=rot, wmain=main, wtail=main[:, :, O_DG:O_DG + TAIL_COLS], wgg=main[:, :, O_GG:O_GG + 256])


@functools.lru_cache(maxsize=None)
def _constant_tables(seq_len, tb, nb):
    c = CHUNK
    lane = np.arange(256)
    head_std = lane // 64
    head_qk = (lane % 128) // 32
    head_g = np.arange(128) // 32
    i = np.arange(c)[:, None]
    j = np.arange(256)[None, :] % 64
    t = {}
    heads = np.arange(4)[:, None, None]
    t["hm"] = np.broadcast_to(head_std[None, None, :] == heads, (4, tb, 256)).astype(np.float32)
    t["hmqk"] = np.broadcast_to(head_qk[None, None, :] == heads, (4, tb, 256)).astype(np.float32)
    t["hmg"] = np.broadcast_to(head_g[None, None, :] == heads, (4, c, 128)).astype(np.float32)
    t["hs"] = (head_std[:, None] == head_std[None, :]).astype(np.float32)
    t["bd"] = t["hs"]
    t["bdqk"] = (head_std[:, None] == head_qk[None, :]).astype(np.float32)
    t["bdg"] = (head_std[:, None] == head_g[None, :]).astype(np.float32)
    t["tril"] = (i >= j).astype(np.float32)
    t["strict"] = (i > j).astype(np.float32)
    t["eye"] = (i == j).astype(np.float32)
    r = np.arange(tb)
    same_chunk = (r[:, None] // c) == (r[None, :] // c)
    t["btri"] = (same_chunk & (r[:, None] >= r[None, :])).astype(np.float32)
    t["selc"] = ((r[None, :] // c) == np.arange(16)[:, None]).astype(np.float32)
    n_grp, n_stp = _s5_groups(tb, nb)
    jj, mm, gg, bb = np.meshgrid(np.arange(S5_LAGS), np.arange(n_stp), np.arange(n_grp), np.arange(nb),
                                 indexing="ij")
    old = (bb * tb + (gg * n_stp + mm) * S5_LAGS + jj).reshape(-1)
    sperm = np.zeros((nb * tb, nb * tb), np.float32)
    sperm[np.arange(nb * tb), old] = 1.0
    t["sperm"] = sperm
    t["eyer"] = np.tile(t["eye"], (tb // c, 1))
    rc = min(RET_CHUNK, tb)
    lg = np.log(1.0 - 2.0 ** (-5.0 - np.arange(4, dtype=np.float64)))
    ri = np.arange(rc)[:, None]
    rj = np.arange(4 * rc)[None, :] % rc
    lg_cols = lg[np.arange(4 * rc) // rc][None, :]
    t["rdall"] = np.where(ri >= rj, np.exp(lg_cols * np.where(ri >= rj, ri - rj, 0)), 0.0).astype(np.float32)
    lg_qk = lg[head_qk][None, :]
    t["rqdec"] = np.exp(lg_qk * (ri + 1.0)).astype(np.float32)
    t["rkdec"] = np.exp(lg_qk * (rc - 1.0 - ri)).astype(np.float32)
    t["rcd"] = np.exp(lg_qk * rc).astype(np.float32)
    inv = ROPE_BASE ** (-np.arange(0, HEAD_DIM, 2, dtype=np.float64) / HEAD_DIM)
    ang = np.arange(seq_len, dtype=np.float64)[:, None] * inv[None, :]
    t["cos"] = np.tile(np.cos(ang), (1, 4)).astype(np.float32)
    t["sin"] = np.tile(np.sin(ang), (1, 4)).astype(np.float32)
    t["tile16"] = np.tile(np.eye(S5_GROUP, dtype=np.float32), (1, S5_GROUPS))
    t["tile64"] = np.tile(np.eye(S5_STATE, dtype=np.float32), (1, S5_GROUPS))
    return t


def _s5_tables(lam_re, lam_im, b_re, b_im, c_re, c_im, d, log_dt, tabs, n_stp):
    hp = lax.Precision.HIGHEST
    g, p, hc = S5_GROUPS, S5_STATE, S5_GROUP
    nl = lam_re.shape[0]
    lam_re, lam_im = lam_re.astype(F32), lam_im.astype(F32)
    dt = jnp.exp(log_dt.astype(F32))[..., None]
    mag = jnp.exp(lam_re * dt)
    ang = lam_im * dt
    a_re, a_im = mag * jnp.cos(ang), mag * jnp.sin(ang)
    den = lam_re * lam_re + lam_im * lam_im
    nr, ni = a_re - 1.0, a_im
    coef_re = (nr * lam_re + ni * lam_im) / den
    coef_im = (ni * lam_re - nr * lam_im) / den
    b_re, b_im = b_re.astype(F32), b_im.astype(F32)
    bb_re = coef_re[..., None] * b_re - coef_im[..., None] * b_im
    bb_im = coef_re[..., None] * b_im + coef_im[..., None] * b_re

    def apow(n):
        n = jnp.asarray(n, F32)[None, :, None, None]
        m = jnp.exp((lam_re * dt)[:, None] * n)
        return m * jnp.cos(ang[:, None] * n), m * jnp.sin(ang[:, None] * n)

    lags = np.arange(S5_LAGS)
    p_re, p_im = apow(lags)
    c_re, c_im = c_re.astype(F32), c_im.astype(F32)
    ab_re = p_re[..., None] * bb_re[:, None] - p_im[..., None] * bb_im[:, None]
    ab_im = p_re[..., None] * bb_im[:, None] + p_im[..., None] * bb_re[:, None]
    kk = (jnp.einsum('ntgpi,ngop->ntgio', ab_re, c_re, precision=hp)
          - jnp.einsum('ntgpi,ngop->ntgio', ab_im, c_im, precision=hp))
    kk = kk.at[:, 0].add(d.astype(F32)[..., None] * jnp.eye(hc, dtype=F32))
    grp256 = np.arange(g * hc) // hc
    grp1024 = np.arange(g * p) // p
    tile16 = jnp.asarray(tabs["tile16"])
    tile64 = jnp.asarray(tabs["tile64"])
    kcat = jnp.einsum('nro,oc->nrc', kk.reshape(nl, S5_LAGS * g * hc, hc), tile16, precision=hp)
    kcat = jnp.where(jnp.asarray(np.tile(grp256, S5_LAGS)[:, None] == grp256[None, :]), kcat, 0.0)

    def rows_to_state(bb):
        m = jnp.einsum('nrp,pc->nrc', jnp.transpose(bb, (0, 1, 3, 2)).reshape(nl, g * hc, p), tile64,
                       precision=hp)
        return jnp.where(jnp.asarray(grp256[:, None] == grp1024[None, :]), m, 0.0)

    def state_to_rows(cc):
        m = jnp.einsum('nro,oc->nrc', jnp.transpose(cc, (0, 1, 3, 2)).reshape(nl, g * p, hc), tile16,
                       precision=hp)
        return jnp.where(jnp.asarray(grp1024[:, None] == grp256[None, :]), m, 0.0)

    bbar = jnp.concatenate([rows_to_state(bb_re), rows_to_state(bb_im)], axis=2)
    cbd = jnp.concatenate([state_to_rows(c_re), -state_to_rows(c_im)], axis=1)
    t_re, t_im = apow(lags + 1)
    a1_re, a1_im = apow(np.array([1]))
    al_re, al_im = apow(np.array([S5_LAGS]))
    pm_re, pm_im = apow(S5_LAGS * np.arange(n_stp))
    ag_re, ag_im = apow(np.array([S5_LAGS * n_stp]))
    flat = lambda z: z.reshape(nl, z.shape[1], g * p)
    return dict(skcat=_bf(kcat), sbbar=_bf(bbar), scbd=_bf(cbd),
                stre=flat(t_re), stim=flat(t_im),
                sa=jnp.concatenate([flat(a1_re), flat(a1_im), flat(al_re), flat(al_im)], axis=1),
                scar=jnp.concatenate([flat(pm_re), flat(pm_im), flat(ag_re), flat(ag_im)], axis=1))


def _const_spec(arr):
    nd = arr.ndim
    return pl.BlockSpec(arr.shape, lambda b, t, _nd=nd: (0,) * _nd)


def _layer_spec(arr, layer):
    nd = arr.ndim - 1
    return pl.BlockSpec((None,) + arr.shape[1:], lambda b, t, _nd=nd, _l=layer: (_l,) + (0,) * _nd)


def _layer_call(x, layer, stacked, const_inputs, cos, sin, tb, nb):
    bsz, seq, _ = x.shape
    grid = (bsz // nb, seq // tb)
    rr = nb * tb
    in_specs = [pl.BlockSpec((nb, tb, D_MODEL), lambda b, t: (b, t, 0)),
                pl.BlockSpec((tb, LANE), lambda b, t: (t, 0)),
                pl.BlockSpec((tb, LANE), lambda b, t: (t, 0))]
    operands = [x, cos, sin]
    order = ["gpre", "gpost", "wrot", "wmain", "wtail", "wgg", "wout",
             "hm", "hmqk", "hmg", "hs", "bd", "bdqk", "bdg", "tril", "strict", "eye", "eyer", "btri", "selc", "sperm",
             "rdall", "rqdec", "rkdec", "rcd", "rnorm",
             "dconv", "darow", "dbias", "debeta", "deg", "dnorm",
             "skcat", "sbbar", "scbd", "stre", "stim", "sa", "scar", "swglu", "sbglu",
             "gwgk", "gb", "gnorm"]
    for name in order:
        if name in stacked:
            operands.append(stacked[name])
            in_specs.append(_layer_spec(stacked[name], layer))
        else:
            operands.append(const_inputs[name])
            in_specs.append(_const_spec(const_inputs[name]))
    scratch = [
        pltpu.VMEM((rr, D_MODEL), BF16),
        pltpu.VMEM((rr, G_RET), F32),
        pltpu.VMEM((rr, G_DN), F32),
        pltpu.VMEM((rr, G_S5), F32),
        pltpu.VMEM((rr, G_GLA), F32),
        pltpu.VMEM((rr, D_MODEL), BF16),
        pltpu.VMEM((nb, SUBLANES, 768), F32),
        pltpu.VMEM((rr, 768), F32),
        pltpu.VMEM((rr, S5_LAGS * BRANCH_W), BF16),
        pltpu.VMEM((nb, BRANCH_W, BRANCH_W), F32),
        pltpu.VMEM((nb, BRANCH_W, BRANCH_W), F32),
        pltpu.VMEM((nb, BRANCH_W, GLA_QK), F32),
        pltpu.VMEM((2, nb, S5_HALF), F32),
    ]
    return pl.pallas_call(
        functools.partial(_layer_kernel, tb=tb, nb=nb),
        grid=grid,
        in_specs=in_specs,
        out_specs=pl.BlockSpec((nb, tb, D_MODEL), lambda b, t: (b, t, 0)),
        out_shape=jax.ShapeDtypeStruct(x.shape, x.dtype),
        scratch_shapes=scratch,
        compiler_params=pltpu.CompilerParams(
            dimension_semantics=("arbitrary", "arbitrary"),
            vmem_limit_bytes=VMEM_LIMIT_BYTES),
        name="hybrid_layer",
    )(*operands)


def _pick_tiles(bsz, seq):
    nb = 2 if bsz % 2 == 0 else 1
    for tb in (256, 128, 64):
        if seq % tb == 0:
            return tb, nb
    raise ValueError(f"sequence length {seq} must be a multiple of {CHUNK}")


def kernel(x, norm_pre, norm_post, w_in, w_out, ret_norm, dn_conv, dn_a_log, dn_dt_bias, dn_norm,
           s5_lam_re, s5_lam_im, s5_b_re, s5_b_im, s5_c_re, s5_c_im, s5_d, s5_log_dt, s5_w_glu, s5_b_glu,
           gla_w_gk, gla_b_gk, gla_norm):
    bsz, seq, dm = x.shape
    depth = w_in.shape[0]
    assert dm == D_MODEL and x.dtype == F32
    tb, nb = _pick_tiles(bsz, seq)
    tabs = _constant_tables(seq, tb, nb)
    bf_names = ("hm", "hmqk", "hmg", "hs", "bd", "btri", "selc", "sperm")
    skip = ("cos", "sin", "tile16", "tile64")
    const_inputs = {k: jnp.asarray(v, BF16 if k in bf_names else F32)
                    for k, v in tabs.items() if k not in skip}
    cos = jnp.asarray(tabs["cos"])
    sin = jnp.asarray(tabs["sin"])

    w_in_ops = _in_proj_operands(w_in)
    w_out_b = _bf(w_out)

    e_beta = np.zeros((LANE, 256), np.float32)
    e_g = np.zeros((LANE, 256), np.float32)
    for hh in range(N_HEADS):
        e_beta[SM_BETA + hh, 64 * hh:64 * hh + 64] = 1.0
        e_g[SM_A + hh, 64 * hh:64 * hh + 64] = 1.0
    e_beta = jnp.asarray(e_beta, BF16)
    e_g = jnp.asarray(e_g, BF16)

    a_rows = jnp.zeros((depth, 1, LANE), F32).at[:, 0, SM_A:SM_A + 4].set(jnp.exp(dn_a_log.astype(F32)))
    b_rows = jnp.zeros((depth, 1, LANE), F32).at[:, 0, SM_A:SM_A + 4].set(dn_dt_bias.astype(F32))
    wgk = jnp.zeros((depth, LANE, GLA_QK), BF16).at[:, 0:GLA_GATE_RANK, :].set(_bf(gla_w_gk))
    s5 = _s5_tables(s5_lam_re, s5_lam_im, s5_b_re, s5_b_im, s5_c_re, s5_c_im, s5_d, s5_log_dt, tabs,
                    _s5_groups(tb, nb)[1])
    tile4 = lambda g: jnp.tile(g.astype(F32), (1, N_HEADS)).reshape(depth, 1, 256)
    rnorm, dnorm, gnorm = tile4(ret_norm), tile4(dn_norm), tile4(gla_norm)
    swglu = _bf(s5_w_glu)

    stacked = dict(
        gpre=norm_pre.reshape(depth, 1, dm).astype(F32),
        gpost=norm_post.reshape(depth, 1, dm).astype(F32),
        wout=w_out_b,
        rnorm=rnorm,
        dconv=dn_conv.astype(F32),
        darow=a_rows, dbias=b_rows,
        dnorm=dnorm,
        swglu=swglu, sbglu=s5_b_glu.reshape(depth, 1, 256).astype(F32),
        gwgk=wgk, gb=gla_b_gk.reshape(depth, 1, GLA_QK).astype(F32),
        gnorm=gnorm,
    )
    stacked.update(s5)
    stacked.update(w_in_ops)
    const_inputs.update(debeta=e_beta, deg=e_g)
    for i in range(depth):
        x = _layer_call(x, i, stacked, const_inputs, cos, sin, tb, nb)
    return x
```

```python
import functools
import math

import jax
import jax.numpy as jnp
import numpy as np
from jax import lax
from jax.experimental import pallas as pl
from jax.experimental.pallas import tpu as pltpu

F32 = jnp.float32
BF16 = jnp.bfloat16

D_MODEL = 1024
BRANCH_W = 256
N_HEADS = 4
HEAD_DIM = 64
EPS = 1e-6
ROPE_BASE = 10000.0
DN_CONV = 4
S5_GROUP = 16
S5_GROUPS = 16
S5_STATE = 64
S5_HALF = S5_GROUPS * S5_STATE
GLA_DK = 32
GLA_QK = 128
GLA_GATE_RANK = 16
GLA_GATE_TAU = 16.0
IN_SPLITS = [256, 256, 256, 256, 768, 4, 4, 256, 256, 256, 128, 128, 256, 16, 256]

CHUNK = 64
RET_CHUNK = 128
S5_LAGS = 8
LANE = 128
SUBLANES = 8
VMEM_LIMIT_BYTES = 56 * 1024 * 1024

_OFF = np.concatenate([[0], np.cumsum(IN_SPLITS)])
(O_RQ, O_RK, O_RV, O_RG, O_DQKV, O_DBETA, O_DA, O_DG, O_SU, O_SG,
 O_GQ, O_GK, O_GV, O_GCODE, O_GG) = (int(o) for o in _OFF[:-1])
assert O_DBETA % LANE == 0 and O_DA == O_DBETA + 4 and O_DG == O_DA + 4
_T = lambda o: o - O_DG
PROJ_GROUPS = {
    "retention": [("rot", 0, 256), ("rot", 256, 256), ("main", O_RV, 256), ("main", O_RG, 256)],
    "deltanet": [("main", O_DQKV, 256), ("main", O_DQKV + 256, 256), ("main", O_DQKV + 512, 256),
                 ("main", O_DBETA, LANE), ("tail", _T(O_DG), 256)],
    "s5": [("tail", _T(O_SU), 256), ("tail", _T(O_SG), 256)],
    "gla": [("tail", _T(O_GQ), 256), ("tail", _T(O_GV), 256), ("tail", _T(O_GCODE), LANE), ("gg", 0, 256)],
}
G_RET, G_DN, G_S5, G_GLA = (sum(w for _, _, w in PROJ_GROUPS[k]) for k in ("retention", "deltanet", "s5", "gla"))
TAIL_COLS = _T(O_GCODE) + LANE
R_Q, R_K, R_V, R_G = 0, 256, 512, 768
D_QKV, D_SMALL, D_G = 0, 768, 896
S_U, S_G = 0, 256
L_Q, L_K, L_V, L_CODE, L_G = 0, 128, 256, 512, 640
SM_BETA, SM_A = 0, 4


def _bf(x):
    return x.astype(BF16)


def _dot(a, b):
    return jnp.dot(a, b, preferred_element_type=F32)


def _dot_nt(a, b):
    return lax.dot_general(a, b, (((1,), (1,)), ((), ())), preferred_element_type=F32)


def _dot_tn(a, b):
    return lax.dot_general(a, b, (((0,), (0,)), ((), ())), preferred_element_type=F32)


def _split2(x):
    x1 = _bf(x)
    x2 = _bf(x - x1.astype(F32))
    return x1, x2


def _dot_x2(x, m):
    x1, x2 = _split2(x)
    return _dot(x1, m) + _dot(x2, m)


def _dot_m2(m, x):
    x1, x2 = _split2(x)
    return _dot(m, x1) + _dot(m, x2)


def _sigmoid(x):
    return 1.0 / (1.0 + jnp.exp(-x))


def _silu(x):
    return x * _sigmoid(x)


def _softplus(x):
    return jnp.maximum(x, 0.0) + jnp.log(1.0 + jnp.exp(-jnp.abs(x)))


def _stack_heads(x, hm_ref):
    xb = _bf(x)
    rows = x.shape[0]
    return jnp.concatenate([xb * hm_ref[h, 0:rows, :] for h in range(N_HEADS)], axis=0)


def _head_rmsnorm_gate(o, g_row, gate, hs):
    ms = _dot(_bf(o * o), hs) * (1.0 / HEAD_DIM)
    return o * lax.rsqrt(ms + EPS) * g_row * _silu(gate)


def _s5_groups(tb, nb):
    n_grp = max(1, SUBLANES // nb)
    n_s5 = tb // S5_LAGS
    assert n_s5 % n_grp == 0
    return n_grp, n_s5 // n_grp


def _interleave(feeder, consumers):
    live = []
    feeding = True
    while feeding or live:
        if feeding:
            try:
                tag = next(feeder)
                while tag is not None:
                    live.append(consumers.pop(tag))
                    tag = next(feeder)
            except StopIteration:
                feeding = False
        for g in list(live):
            try:
                next(g)
            except StopIteration:
                live.remove(g)
    assert not consumers


def _layer_kernel(
        x_ref, cos_ref, sin_ref,
        gpre_ref, gpost_ref, wrot_ref, wmain_ref, wtail_ref, wgg_ref, wout_ref,
        hm_ref, hmqk_ref, hmg_ref, hs_ref, bd_ref, bdqk_ref, bdg_ref,
        tril_ref, strict_ref, eye_ref, eyer_ref, btri_ref, selc_ref, sperm_ref,
        rdall_ref, rqdec_ref, rkdec_ref, rcd_ref, rnorm_ref,
        dconv_ref, darow_ref, dbias_ref, debeta_ref, deg_ref, dnorm_ref,
        skcat_ref, sbbar_ref, scbd_ref, stre_ref, stim_ref, sa_ref, scar_ref,
        swglu_ref, sbglu_ref,
        gwgk_ref, gb_ref, gnorm_ref,
        out_ref,
        h_ref, pr_ref, pd_ref, ps_ref, pg_ref, y_ref, xc_ref, qkv_ref, ucat_ref,
        sret_ref, sdn_ref, sgla_ref, ss5_ref,
        *, tb, nb):
    t_idx = pl.program_id(1)
    n_chunks = tb // CHUNK
    n_s5 = tb // S5_LAGS
    n_grp, n_stp = _s5_groups(tb, nb)
    batches = range(nb)
    rr = nb * tb

    def rb(b):
        return slice(b * tb, (b + 1) * tb)

    units = [(b, c, slice(b * tb + c * CHUNK, b * tb + (c + 1) * CHUNK))
             for c in range(n_chunks) for b in batches]
    n_units = len(units)

    @pl.when(t_idx == 0)
    def _reset():
        sret_ref[...] = jnp.zeros_like(sret_ref)
        sdn_ref[...] = jnp.zeros_like(sdn_ref)
        sgla_ref[...] = jnp.zeros_like(sgla_ref)
        ss5_ref[...] = jnp.zeros_like(ss5_ref)
        for b in batches:
            xc_ref[b] = jnp.zeros((SUBLANES, 768), F32)

    for b in batches:
        x = x_ref[b]
        h_ref[rb(b), :] = _bf(x * lax.rsqrt(jnp.mean(x * x, axis=-1, keepdims=True) + EPS) * gpre_ref[...])

    def projections():
        weights = {"rot": wrot_ref, "main": wmain_ref, "tail": wtail_ref, "gg": wgg_ref}
        for name, dst_ref in (("deltanet", pd_ref), ("s5", ps_ref), ("gla", pg_ref), ("retention", pr_ref)):
            c0 = 0
            for operand, first, width in PROJ_GROUPS[name]:
                dst_ref[:, c0:c0 + width] = _dot(h_ref[...], weights[operand][:, first:first + width])
                c0 += width
                yield None
            yield name

    o_partial = []

    def out_project(branch):
        rows = slice(branch * BRANCH_W, (branch + 1) * BRANCH_W)
        o_partial.append(_dot(y_ref[:, rows], wout_ref[rows, :]))

    hs = hs_ref[...]
    bd = bd_ref[...]
    bd32 = bd.astype(F32)

    def retention():
        rc = min(RET_CHUNK, tb)
        n_rc = tb // rc
        runits = [(b, c, slice(b * tb + c * rc, b * tb + (c + 1) * rc)) for c in range(n_rc) for b in batches]
        ret_q, ret_k, ret_v = {}, {}, {}
        for b, c, sl in runits:
            cs = cos_ref[c * rc:(c + 1) * rc, :]
            sn = sin_ref[c * rc:(c + 1) * rc, :]
            qa = pr_ref[sl, R_Q:R_Q + LANE]
            qb = pr_ref[sl, R_Q + LANE:R_Q + 2 * LANE]
            ka = pr_ref[sl, R_K:R_K + LANE]
            kb = pr_ref[sl, R_K + LANE:R_K + 2 * LANE]
            ret_q[b, c] = jnp.concatenate([qa * cs - qb * sn, qa * sn + qb * cs], axis=1)
            ret_k[b, c] = jnp.concatenate([ka * cs - kb * sn, ka * sn + kb * cs], axis=1)
            ret_v[b, c] = pr_ref[sl, R_V:R_V + BRANCH_W]
            if b == nb - 1:
                yield
        o_intra, kv = {}, {}
        for b, c, sl in runits:
            s_all = _dot_nt(_bf(ret_q[b, c]), _stack_heads(ret_k[b, c], hmqk_ref))
            o_intra[b, c] = _dot(_bf(s_all * rdall_ref[...]), _stack_heads(ret_v[b, c], hm_ref))
            kv[b, c] = bdqk_ref[...] * _dot_tn(_bf(ret_v[b, c]), _bf(ret_k[b, c] * rkdec_ref[...]))
            if b == nb - 1:
                yield
        st = [sret_ref[b] for b in batches]
        o_ret = {}
        for b, c, sl in runits:
            o_ret[b, c] = o_intra[b, c] + _dot_nt(_bf(ret_q[b, c] * rqdec_ref[...]), _bf(st[b]))
            st[b] = st[b] * rcd_ref[...] + kv[b, c]
            if b == nb - 1:
                yield
        for b in batches:
            sret_ref[b] = st[b]
        o = jnp.concatenate([o_ret[b, c] for b in batches for c in range(n_rc)], axis=0)
        y_ref[:, 0:BRANCH_W] = _bf(_head_rmsnorm_gate(o, rnorm_ref[...], pr_ref[:, R_G:R_G + BRANCH_W], hs))
        yield
        out_project(0)

    def deltanet():
        sub = lax.broadcasted_iota(jnp.int32, (tb, 768), 0) % SUBLANES
        for b in batches:
            xb = pd_ref[rb(b), D_QKV:D_QKV + 768]
            xfull = jnp.concatenate([xc_ref[b], xb], axis=0)
            conv = dconv_ref[DN_CONV - 1:DN_CONV, :] * xb
            for s in range(1, DN_CONV):
                rolled = pltpu.roll(xfull.reshape(tb // SUBLANES + 1, SUBLANES, 768), s, axis=1)
                rolled = rolled.reshape(tb + SUBLANES, 768)
                shifted = jnp.where(sub >= s, rolled[SUBLANES:], rolled[0:tb])
                conv = conv + dconv_ref[DN_CONV - 1 - s:DN_CONV - s, :] * shifted
            qkv_ref[rb(b), :] = _silu(conv)
            xc_ref[b] = xb[tb - SUBLANES:tb]
            yield

        q = qkv_ref[:, 0:256]
        k = qkv_ref[:, 256:512]
        v = qkv_ref[:, 512:768]
        ss = _dot(_bf(jnp.concatenate([q * q, k * k], axis=0)), hs)
        qn = q * lax.rsqrt(ss[0:rr] + EPS) * (HEAD_DIM ** -0.5)
        kn = k * lax.rsqrt(ss[rr:2 * rr] + EPS)
        yield
        small = pd_ref[:, D_SMALL:D_SMALL + LANE]
        beta = _dot(_bf(_sigmoid(small)), debeta_ref[...])
        g_s = -darow_ref[...] * _softplus(small + dbias_ref[...])
        gcum_s = jnp.concatenate([_dot_m2(btri_ref[...], g_s[rb(b)]) for b in batches], axis=0)
        yield
        gcum = _dot_x2(gcum_s, deg_ref[...])
        grows = [_dot_m2(selc_ref[...], gcum[rb(b)] * eyer_ref[...]) for b in batches]
        yield
        dec, glast, kgt = [], [], []
        for b, c, sl in units:
            gc = gcum[sl]
            dec.append(jnp.exp(jnp.minimum(gc - grows[b][c:c + 1, :], 0.0)))
            glast.append(gc[CHUNK - 1:CHUNK, :])
            kgt.append(_bf((kn[sl] * jnp.exp(glast[-1] - gc)).T))
            if b == nb - 1:
                yield
        aa = []
        for b, _, sl in units:
            aa.append(_dot_nt(_bf(jnp.concatenate([kn[sl] * beta[sl], qn[sl]], axis=0)),
                              _stack_heads(kn[sl], hm_ref)))
            if b == nb - 1:
                yield
        attn = [aa[u][CHUNK:2 * CHUNK] * dec[u] * tril_ref[...] for u in range(n_units)]

        def blockdiag(m):
            mb = _bf(m)
            return jnp.concatenate([mb, mb, mb, mb], axis=0) * bd

        pw = [-(aa[u][0:CHUNK] * dec[u] * strict_ref[...]) for u in range(n_units)]
        t_all = [eye_ref[...] + pw[u] for u in range(n_units)]
        yield
        pw = [_dot(_bf(pw[u]), blockdiag(pw[u])) for u in range(n_units)]
        yield
        for level in range(1, 5):
            both = [_dot(_bf(jnp.concatenate([pw[u], t_all[u]], axis=0)), blockdiag(pw[u]))
                    for u in range(n_units)]
            pw = [both[u][0:CHUNK] for u in range(n_units)]
            t_all = [t_all[u] + both[u][CHUNK:2 * CHUNK] for u in range(n_units)]
            yield
        t_all = [t_all[u] + _dot(_bf(t_all[u]), blockdiag(pw[u])) for u in range(n_units)]
        yield
        uw = []
        for u, (b, _, sl) in enumerate(units):
            bsl = beta[sl]
            uw.append(_dot(_bf(t_all[u]), jnp.concatenate(
                [_stack_heads(v[sl] * bsl, hm_ref),
                 _stack_heads(kn[sl] * bsl * jnp.exp(gcum[sl]), hm_ref)], axis=1)))
            if b == nb - 1:
                yield

        st = [sdn_ref[b] for b in batches]
        o_parts = {}
        for u, (b, c, sl) in enumerate(units):
            wq = _dot(_bf(jnp.concatenate([uw[u][:, 256:512], qn[sl] * jnp.exp(gcum[sl])], axis=0)), _bf(st[b]))
            v_new = uw[u][:, 0:256] - wq[0:CHUNK]
            o_parts[(b, c)] = wq[CHUNK:2 * CHUNK] + _dot(_bf(attn[u]), _stack_heads(v_new, hm_ref))
            st[b] = st[b] * jnp.exp(glast[u]) + bd32 * _dot(kgt[u], _bf(v_new))
            if b == nb - 1:
                yield
        for b in batches:
            sdn_ref[b] = st[b]
        o = jnp.concatenate([o_parts[(b, c)] for b in batches for c in range(n_chunks)], axis=0)
        y_ref[:, BRANCH_W:2 * BRANCH_W] = _bf(
            _head_rmsnorm_gate(o, dnorm_ref[...], pd_ref[:, D_G:D_G + BRANCH_W], hs))
        yield
        out_project(1)

    def s5():
        u_in = ps_ref[:, S_U:S_U + BRANCH_W]
        row_in_chunk = lax.broadcasted_iota(jnp.int32, (rr, BRANCH_W), 0) % S5_LAGS
        ucat_ref[:, 0:BRANCH_W] = _bf(u_in)
        for tau in range(1, S5_LAGS):
            shifted = pltpu.roll(u_in.reshape(rr // S5_LAGS, S5_LAGS, BRANCH_W), tau, axis=1).reshape(rr, BRANCH_W)
            ucat_ref[:, tau * BRANCH_W:(tau + 1) * BRANCH_W] = _bf(jnp.where(row_in_chunk >= tau, shifted, 0.0))
            if tau % 4 == 0:
                yield
        yield
        u_perm = _bf(_dot(sperm_ref[...], _bf(u_in)))
        xx = _dot(u_perm, sbbar_ref[...])
        yield
        a1_re = sa_ref[0:1, :]
        a1_im = sa_ref[1:2, :]
        n_cb = nb * n_s5
        z_re = xx[0:n_cb, 0:S5_HALF]
        z_im = xx[0:n_cb, S5_HALF:2 * S5_HALF]
        for j in range(1, S5_LAGS):
            xj_re = xx[j * n_cb:(j + 1) * n_cb, 0:S5_HALF]
            xj_im = xx[j * n_cb:(j + 1) * n_cb, S5_HALF:2 * S5_HALF]
            z_re, z_im = a1_re * z_re - a1_im * z_im + xj_re, a1_re * z_im + a1_im * z_re + xj_im
            if j % 2 == 1:
                yield
        yield
        y_s5 = jnp.concatenate([_dot(ucat_ref[rb(b), :], skcat_ref[...]) for b in batches], axis=0)
        yield
        grp_rows = n_grp * nb
        al_re = sa_ref[2:3, :]
        al_im = sa_ref[3:4, :]
        s_re = jnp.zeros((grp_rows, S5_HALF), F32)
        s_im = jnp.zeros((grp_rows, S5_HALF), F32)
        loc_re, loc_im = [], []
        for m in range(n_stp):
            loc_re.append(s_re)
            loc_im.append(s_im)
            zm_re = z_re[m * grp_rows:(m + 1) * grp_rows]
            zm_im = z_im[m * grp_rows:(m + 1) * grp_rows]
            s_re, s_im = al_re * s_re - al_im * s_im + zm_re, al_re * s_im + al_im * s_re + zm_im
            if m % 2 == 1:
                yield
        ag_re = scar_ref[2 * n_stp:2 * n_stp + 1, :]
        ag_im = scar_ref[2 * n_stp + 1:2 * n_stp + 2, :]
        c_re = ss5_ref[0]
        c_im = ss5_ref[1]
        car_re, car_im = [], []
        for g in range(n_grp):
            car_re.append(c_re)
            car_im.append(c_im)
            e_re = s_re[g * nb:(g + 1) * nb]
            e_im = s_im[g * nb:(g + 1) * nb]
            c_re, c_im = ag_re * c_re - ag_im * c_im + e_re, ag_re * c_im + ag_im * c_re + e_im
        ss5_ref[0] = c_re
        ss5_ref[1] = c_im
        yield
        car_re = jnp.concatenate(car_re, axis=0)
        car_im = jnp.concatenate(car_im, axis=0)
        sp_re, sp_im = [], []
        for m in range(n_stp):
            p_re = scar_ref[m:m + 1, :]
            p_im = scar_ref[n_stp + m:n_stp + m + 1, :]
            sp_re.append(loc_re[m] + p_re * car_re - p_im * car_im)
            sp_im.append(loc_im[m] + p_re * car_im + p_im * car_re)
        yield
        sp_re = jnp.concatenate(sp_re, axis=0)[:, None, :]
        sp_im = jnp.concatenate(sp_im, axis=0)[:, None, :]
        t_re = stre_ref[...][None]
        t_im = stim_ref[...][None]

        def batch_major(v):
            tiles = []
            for b in batches:
                for n in range(n_s5):
                    g, m = divmod(n, n_stp)
                    src = (m * n_grp + g) * nb + b
                    tiles.append(v[src * S5_LAGS:(src + 1) * S5_LAGS])
            return jnp.concatenate(tiles, axis=0)

        v_re = batch_major((t_re * sp_re - t_im * sp_im).reshape(rr, S5_HALF))
        yield
        v_im = batch_major((t_re * sp_im + t_im * sp_re).reshape(rr, S5_HALF))
        yield
        vv = _bf(jnp.concatenate([v_re, v_im], axis=1))
        y_s5 = y_s5 + jnp.concatenate([_dot(vv[rb(b)], scbd_ref[...]) for b in batches], axis=0)
        yield
        c0 = math.sqrt(2.0 / math.pi)
        y_s5 = 0.5 * y_s5 * (1.0 + jnp.tanh(c0 * (y_s5 + 0.044715 * (y_s5 * y_s5 * y_s5))))
        y_s5 = y_s5 * _sigmoid(_dot(_bf(y_s5), swglu_ref[...]) + sbglu_ref[...])
        y_ref[:, 2 * BRANCH_W:3 * BRANCH_W] = _bf(y_s5 * _silu(ps_ref[:, S_G:S_G + BRANCH_W]))
        yield
        out_project(2)

    def gla():
        z = _dot(_bf(pg_ref[:, L_CODE:L_CODE + LANE]), gwgk_ref[...]) + gb_ref[...]
        gk = -_softplus(-z) * (1.0 / GLA_GATE_TAU)
        yield
        cum = jnp.concatenate([_dot_m2(btri_ref[...], gk[rb(b)]) for b in batches], axis=0)
        yield
        q = pg_ref[:, L_Q:L_Q + GLA_QK]
        k = pg_ref[:, L_K:L_K + GLA_QK]
        v = pg_ref[:, L_V:L_V + BRANCH_W]
        qt = _bf(q * jnp.exp(cum) * (GLA_DK ** -0.5))
        kt = k * jnp.exp(-cum)
        clast = [cum[sl][CHUNK - 1:CHUNK, :] for _, _, sl in units]
        yield
        s_all = []
        for b, _, sl in units:
            s_all.append(_dot_nt(qt[sl], _stack_heads(kt[sl], hmg_ref)))
            if b == nb - 1:
                yield
        o_intra = []
        for u, (b, _, sl) in enumerate(units):
            o_intra.append(_dot(_bf(s_all[u] * tril_ref[...]), _stack_heads(v[sl], hm_ref)))
            if b == nb - 1:
                yield
        kv = []
        for u, (b, _, sl) in enumerate(units):
            kv.append(bdg_ref[...] * _dot_tn(_bf(v[sl]), _bf(k[sl] * jnp.exp(clast[u] - cum[sl]))))
            if b == nb - 1:
                yield
        st = [sgla_ref[b] for b in batches]
        o_parts = {}
        for u, (b, c, sl) in enumerate(units):
            o_parts[(b, c)] = o_intra[u] + _dot_nt(qt[sl], _bf(st[b]))
            st[b] = st[b] * jnp.exp(clast[u]) + kv[u]
            if b == nb - 1:
                yield
        for b in batches:
            sgla_ref[b] = st[b]
        o = jnp.concatenate([o_parts[(b, c)] for b in batches for c in range(n_chunks)], axis=0)
        y_ref[:, 3 * BRANCH_W:4 * BRANCH_W] = _bf(
            _head_rmsnorm_gate(o, gnorm_ref[...], pg_ref[:, L_G:L_G + BRANCH_W], hs))
        yield
        out_project(3)

    _interleave(projections(), {"s5": s5(), "deltanet": deltanet(), "retention": retention(), "gla": gla()})

    o = o_partial[0] + o_partial[1] + o_partial[2] + o_partial[3]
    o = o * lax.rsqrt(jnp.mean(o * o, axis=-1, keepdims=True) + EPS) * gpost_ref[...]
    for b in batches:
        out_ref[b] = x_ref[b] + o[rb(b)]


def _in_proj_operands(w_in):
    dp, dm, _ = w_in.shape
    main = _bf(w_in)

    def deinterleave(o):
        t = w_in[:, :, o:o + 256].reshape(dp, dm, N_HEADS, HEAD_DIM // 2, 2)
        return jnp.transpose(t, (0, 1, 4, 2, 3)).reshape(dp, dm, 256)

    rot = _bf(jnp.concatenate([deinterleave(O_RQ) * (HEAD_DIM ** -0.5), deinterleave(O_RK)], axis=2))
    return dict(wrot=rot, wmain=main, wtail=main[:, :, O_DG:O_DG + TAIL_COLS], wgg=main[:, :, O_GG:O_GG + 256])


@functools.lru_cache(maxsize=None)
def _constant_tables(seq_len, tb, nb):
    c = CHUNK
    lane = np.arange(256)
    head_std = lane // 64
    head_qk = (lane % 128) // 32
    head_g = np.arange(128) // 32
    i = np.arange(c)[:, None]
    j = np.arange(256)[None, :] % 64
    t = {}
    heads = np.arange(4)[:, None, None]
    t["hm"] = np.broadcast_to(head_std[None, None, :] == heads, (4, tb, 256)).astype(np.float32)
    t["hmqk"] = np.broadcast_to(head_qk[None, None, :] == heads, (4, tb, 256)).astype(np.float32)
    t["hmg"] = np.broadcast_to(head_g[None, None, :] == heads, (4, c, 128)).astype(np.float32)
    t["hs"] = (head_std[:, None] == head_std[None, :]).astype(np.float32)
    t["bd"] = t["hs"]
    t["bdqk"] = (head_std[:, None] == head_qk[None, :]).astype(np.float32)
    t["bdg"] = (head_std[:, None] == head_g[None, :]).astype(np.float32)
    t["tril"] = (i >= j).astype(np.float32)
    t["strict"] = (i > j).astype(np.float32)
    t["eye"] = (i == j).astype(np.float32)
    r = np.arange(tb)
    same_chunk = (r[:, None] // c) == (r[None, :] // c)
    t["btri"] = (same_chunk & (r[:, None] >= r[None, :])).astype(np.float32)
    t["selc"] = ((r[None, :] // c) == np.arange(16)[:, None]).astype(np.float32)
    n_grp, n_stp = _s5_groups(tb, nb)
    jj, mm, gg, bb = np.meshgrid(np.arange(S5_LAGS), np.arange(n_stp), np.arange(n_grp), np.arange(nb),
                                 indexing="ij")
    old = (bb * tb + (gg * n_stp + mm) * S5_LAGS + jj).reshape(-1)
    sperm = np.zeros((nb * tb, nb * tb), np.float32)
    sperm[np.arange(nb * tb), old] = 1.0
    t["sperm"] = sperm
    t["eyer"] = np.tile(t["eye"], (tb // c, 1))
    rc = min(RET_CHUNK, tb)
    lg = np.log(1.0 - 2.0 ** (-5.0 - np.arange(4, dtype=np.float64)))
    ri = np.arange(rc)[:, None]
    rj = np.arange(4 * rc)[None, :] % rc
    lg_cols = lg[np.arange(4 * rc) // rc][None, :]
    t["rdall"] = np.where(ri >= rj, np.exp(lg_cols * np.where(ri >= rj, ri - rj, 0)), 0.0).astype(np.float32)
    lg_qk = lg[head_qk][None, :]
    t["rqdec"] = np.exp(lg_qk * (ri + 1.0)).astype(np.float32)
    t["rkdec"] = np.exp(lg_qk * (rc - 1.0 - ri)).astype(np.float32)
    t["rcd"] = np.exp(lg_qk * rc).astype(np.float32)
    inv = ROPE_BASE ** (-np.arange(0, HEAD_DIM, 2, dtype=np.float64) / HEAD_DIM)
    ang = np.arange(seq_len, dtype=np.float64)[:, None] * inv[None, :]
    t["cos"] = np.tile(np.cos(ang), (1, 4)).astype(np.float32)
    t["sin"] = np.tile(np.sin(ang), (1, 4)).astype(np.float32)
    t["tile16"] = np.tile(np.eye(S5_GROUP, dtype=np.float32), (1, S5_GROUPS))
    t["tile64"] = np.tile(np.eye(S5_STATE, dtype=np.float32), (1, S5_GROUPS))
    return t


def _s5_tables(lam_re, lam_im, b_re, b_im, c_re, c_im, d, log_dt, tabs, n_stp):
    hp = lax.Precision.HIGHEST
    g, p, hc = S5_GROUPS, S5_STATE, S5_GROUP
    nl = lam_re.shape[0]
    lam_re, lam_im = lam_re.astype(F32), lam_im.astype(F32)
    dt = jnp.exp(log_dt.astype(F32))[..., None]
    mag = jnp.exp(lam_re * dt)
    ang = lam_im * dt
    a_re, a_im = mag * jnp.cos(ang), mag * jnp.sin(ang)
    den = lam_re * lam_re + lam_im * lam_im
    nr, ni = a_re - 1.0, a_im
    coef_re = (nr * lam_re + ni * lam_im) / den
    coef_im = (ni * lam_re - nr * lam_im) / den
    b_re, b_im = b_re.astype(F32), b_im.astype(F32)
    bb_re = coef_re[..., None] * b_re - coef_im[..., None] * b_im
    bb_im = coef_re[..., None] * b_im + coef_im[..., None] * b_re

    def apow(n):
        n = jnp.asarray(n, F32)[None, :, None, None]
        m = jnp.exp((lam_re * dt)[:, None] * n)
        return m * jnp.cos(ang[:, None] * n), m * jnp.sin(ang[:, None] * n)

    lags = np.arange(S5_LAGS)
    p_re, p_im = apow(lags)
    c_re, c_im = c_re.astype(F32), c_im.astype(F32)
    ab_re = p_re[..., None] * bb_re[:, None] - p_im[..., None] * bb_im[:, None]
    ab_im = p_re[..., None] * bb_im[:, None] + p_im[..., None] * bb_re[:, None]
    kk = (jnp.einsum('ntgpi,ngop->ntgio', ab_re, c_re, precision=hp)
          - jnp.einsum('ntgpi,ngop->ntgio', ab_im, c_im, precision=hp))
    kk = kk.at[:, 0].add(d.astype(F32)[..., None] * jnp.eye(hc, dtype=F32))
    grp256 = np.arange(g * hc) // hc
    grp1024 = np.arange(g * p) // p
    tile16 = jnp.asarray(tabs["tile16"])
    tile64 = jnp.asarray(tabs["tile64"])
    kcat = jnp.einsum('nro,oc->nrc', kk.reshape(nl, S5_LAGS * g * hc, hc), tile16, precision=hp)
    kcat = jnp.where(jnp.asarray(np.tile(grp256, S5_LAGS)[:, None] == grp256[None, :]), kcat, 0.0)

    def rows_to_state(bb):
        m = jnp.einsum('nrp,pc->nrc', jnp.transpose(bb, (0, 1, 3, 2)).reshape(nl, g * hc, p), tile64,
                       precision=hp)
        return jnp.where(jnp.asarray(grp256[:, None] == grp1024[None, :]), m, 0.0)

    def state_to_rows(cc):
        m = jnp.einsum('nro,oc->nrc', jnp.transpose(cc, (0, 1, 3, 2)).reshape(nl, g * p, hc), tile16,
                       precision=hp)
        return jnp.where(jnp.asarray(grp1024[:, None] == grp256[None, :]), m, 0.0)

    bbar = jnp.concatenate([rows_to_state(bb_re), rows_to_state(bb_im)], axis=2)
    cbd = jnp.concatenate([state_to_rows(c_re), -state_to_rows(c_im)], axis=1)
    t_re, t_im = apow(lags + 1)
    a1_re, a1_im = apow(np.array([1]))
    al_re, al_im = apow(np.array([S5_LAGS]))
    pm_re, pm_im = apow(S5_LAGS * np.arange(n_stp))
    ag_re, ag_im = apow(np.array([S5_LAGS * n_stp]))
    flat = lambda z: z.reshape(nl, z.shape[1], g * p)
    return dict(skcat=_bf(kcat), sbbar=_bf(bbar), scbd=_bf(cbd),
                stre=flat(t_re), stim=flat(t_im),
                sa=jnp.concatenate([flat(a1_re), flat(a1_im), flat(al_re), flat(al_im)], axis=1),
                scar=jnp.concatenate([flat(pm_re), flat(pm_im), flat(ag_re), flat(ag_im)], axis=1))


def _const_spec(arr):
    nd = arr.ndim
    return pl.BlockSpec(arr.shape, lambda b, t, _nd=nd: (0,) * _nd)


def _layer_spec(arr, layer):
    nd = arr.ndim - 1
    return pl.BlockSpec((None,) + arr.shape[1:], lambda b, t, _nd=nd, _l=layer: (_l,) + (0,) * _nd)


def _layer_call(x, layer, stacked, const_inputs, cos, sin, tb, nb):
    bsz, seq, _ = x.shape
    grid = (bsz // nb, seq // tb)
    rr = nb * tb
    in_specs = [pl.BlockSpec((nb, tb, D_MODEL), lambda b, t: (b, t, 0)),
                pl.BlockSpec((tb, LANE), lambda b, t: (t, 0)),
                pl.BlockSpec((tb, LANE), lambda b, t: (t, 0))]
    operands = [x, cos, sin]
    order = ["gpre", "gpost", "wrot", "wmain", "wtail", "wgg", "wout",
             "hm", "hmqk", "hmg", "hs", "bd", "bdqk", "bdg", "tril", "strict", "eye", "eyer", "btri", "selc", "sperm",
             "rdall", "rqdec", "rkdec", "rcd", "rnorm",
             "dconv", "darow", "dbias", "debeta", "deg", "dnorm",
             "skcat", "sbbar", "scbd", "stre", "stim", "sa", "scar", "swglu", "sbglu",
             "gwgk", "gb", "gnorm"]
    for name in order:
        if name in stacked:
            operands.append(stacked[name])
            in_specs.append(_layer_spec(stacked[name], layer))
        else:
            operands.append(const_inputs[name])
            in_specs.append(_const_spec(const_inputs[name]))
    scratch = [
        pltpu.VMEM((rr, D_MODEL), BF16),
        pltpu.VMEM((rr, G_RET), F32),
        pltpu.VMEM((rr, G_DN), F32),
        pltpu.VMEM((rr, G_S5), F32),
        pltpu.VMEM((rr, G_GLA), F32),
        pltpu.VMEM((rr, D_MODEL), BF16),
        pltpu.VMEM((nb, SUBLANES, 768), F32),
        pltpu.VMEM((rr, 768), F32),
        pltpu.VMEM((rr, S5_LAGS * BRANCH_W), BF16),
        pltpu.VMEM((nb, BRANCH_W, BRANCH_W), F32),
        pltpu.VMEM((nb, BRANCH_W, BRANCH_W), F32),
        pltpu.VMEM((nb, BRANCH_W, GLA_QK), F32),
        pltpu.VMEM((2, nb, S5_HALF), F32),
    ]
    return pl.pallas_call(
        functools.partial(_layer_kernel, tb=tb, nb=nb),
        grid=grid,
        in_specs=in_specs,
        out_specs=pl.BlockSpec((nb, tb, D_MODEL), lambda b, t: (b, t, 0)),
        out_shape=jax.ShapeDtypeStruct(x.shape, x.dtype),
        scratch_shapes=scratch,
        compiler_params=pltpu.CompilerParams(
            dimension_semantics=("arbitrary", "arbitrary"),
            vmem_limit_bytes=VMEM_LIMIT_BYTES),
        name="hybrid_layer",
    )(*operands)


def _pick_tiles(bsz, seq):
    nb = 2 if bsz % 2 == 0 else 1
    for tb in (256, 128, 64):
        if seq % tb == 0:
            return tb, nb
    raise ValueError(f"sequence length {seq} must be a multiple of {CHUNK}")


def kernel(x, norm_pre, norm_post, w_in, w_out, ret_norm, dn_conv, dn_a_log, dn_dt_bias, dn_norm,
           s5_lam_re, s5_lam_im, s5_b_re, s5_b_im, s5_c_re, s5_c_im, s5_d, s5_log_dt, s5_w_glu, s5_b_glu,
           gla_w_gk, gla_b_gk, gla_norm):
    bsz, seq, dm = x.shape
    depth = w_in.shape[0]
    assert dm == D_MODEL and x.dtype == F32
    tb, nb = _pick_tiles(bsz, seq)
    tabs = _constant_tables(seq, tb, nb)
    bf_names = ("hm", "hmqk", "hmg", "hs", "bd", "btri", "selc", "sperm")
    skip = ("cos", "sin", "tile16", "tile64")
    const_inputs = {k: jnp.asarray(v, BF16 if k in bf_names else F32)
                    for k, v in tabs.items() if k not in skip}
    cos = jnp.asarray(tabs["cos"])
    sin = jnp.asarray(tabs["sin"])

    w_in_ops = _in_proj_operands(w_in)
    w_out_b = _bf(w_out)

    e_beta = np.zeros((LANE, 256), np.float32)
    e_g = np.zeros((LANE, 256), np.float32)
    for hh in range(N_HEADS):
        e_beta[SM_BETA + hh, 64 * hh:64 * hh + 64] = 1.0
        e_g[SM_A + hh, 64 * hh:64 * hh + 64] = 1.0
    e_beta = jnp.asarray(e_beta, BF16)
    e_g = jnp.asarray(e_g, BF16)

    a_rows = jnp.zeros((depth, 1, LANE), F32).at[:, 0, SM_A:SM_A + 4].set(jnp.exp(dn_a_log.astype(F32)))
    b_rows = jnp.zeros((depth, 1, LANE), F32).at[:, 0, SM_A:SM_A + 4].set(dn_dt_bias.astype(F32))
    wgk = jnp.zeros((depth, LANE, GLA_QK), BF16).at[:, 0:GLA_GATE_RANK, :].set(_bf(gla_w_gk))
    s5 = _s5_tables(s5_lam_re, s5_lam_im, s5_b_re, s5_b_im, s5_c_re, s5_c_im, s5_d, s5_log_dt, tabs,
                    _s5_groups(tb, nb)[1])
    tile4 = lambda g: jnp.tile(g.astype(F32), (1, N_HEADS)).reshape(depth, 1, 256)
    rnorm, dnorm, gnorm = tile4(ret_norm), tile4(dn_norm), tile4(gla_norm)
    swglu = _bf(s5_w_glu)

    stacked = dict(
        gpre=norm_pre.reshape(depth, 1, dm).astype(F32),
        gpost=norm_post.reshape(depth, 1, dm).astype(F32),
        wout=w_out_b,
        rnorm=rnorm,
        dconv=dn_conv.astype(F32),
        darow=a_rows, dbias=b_rows,
        dnorm=dnorm,
        swglu=swglu, sbglu=s5_b_glu.reshape(depth, 1, 256).astype(F32),
        gwgk=wgk, gb=gla_b_gk.reshape(depth, 1, GLA_QK).astype(F32),
        gnorm=gnorm,
    )
    stacked.update(s5)
    stacked.update(w_in_ops)
    const_inputs.update(debeta=e_beta, deg=e_g)
    for i in range(depth):
        x = _layer_call(x, i, stacked, const_inputs, cos, sin, tb, nb)
    return x
```

```python
import functools
import math

import jax
import jax.numpy as jnp
import numpy as np
from jax import lax
from jax.experimental import pallas as pl
from jax.experimental.pallas import tpu as pltpu

F32 = jnp.float32
BF16 = jnp.bfloat16

D_MODEL = 1024
BRANCH_W = 256
N_HEADS = 4
HEAD_DIM = 64
EPS = 1e-6
ROPE_BASE = 10000.0
DN_CONV = 4
S5_GROUP = 16
S5_GROUPS = 16
S5_STATE = 64
S5_HALF = S5_GROUPS * S5_STATE
GLA_DK = 32
GLA_QK = 128
GLA_GATE_RANK = 16
GLA_GATE_TAU = 16.0
IN_SPLITS = [256, 256, 256, 256, 768, 4, 4, 256, 256, 256, 128, 128, 256, 16, 256]

CHUNK = 64
RET_CHUNK = 128
S5_LAGS = 8
LANE = 128
SUBLANES = 8
VMEM_LIMIT_BYTES = 56 * 1024 * 1024

_OFF = np.concatenate([[0], np.cumsum(IN_SPLITS)])
(O_RQ, O_RK, O_RV, O_RG, O_DQKV, O_DBETA, O_DA, O_DG, O_SU, O_SG,
 O_GQ, O_GK, O_GV, O_GCODE, O_GG) = (int(o) for o in _OFF[:-1])
assert O_DBETA % LANE == 0 and O_DA == O_DBETA + 4 and O_DG == O_DA + 4
_T = lambda o: o - O_DG
PROJ_GROUPS = {
    "retention": [("rot", 0, 256), ("rot", 256, 256), ("main", O_RV, 256), ("main", O_RG, 256)],
    "deltanet": [("main", O_DQKV, 256), ("main", O_DQKV + 256, 256), ("main", O_DQKV + 512, 256),
                 ("main", O_DBETA, LANE), ("tail", _T(O_DG), 256)],
    "s5": [("tail", _T(O_SU), 256), ("tail", _T(O_SG), 256)],
    "gla": [("tail", _T(O_GQ), 256), ("tail", _T(O_GV), 256), ("tail", _T(O_GCODE), LANE), ("gg", 0, 256)],
}
G_RET, G_DN, G_S5, G_GLA = (sum(w for _, _, w in PROJ_GROUPS[k]) for k in ("retention", "deltanet", "s5", "gla"))
TAIL_COLS = _T(O_GCODE) + LANE
R_Q, R_K, R_V, R_G = 0, 256, 512, 768
D_QKV, D_SMALL, D_G = 0, 768, 896
S_U, S_G = 0, 256
L_Q, L_K, L_V, L_CODE, L_G = 0, 128, 256, 512, 640
SM_BETA, SM_A = 0, 4


def _bf(x):
    return x.astype(BF16)


def _dot(a, b):
    return jnp.dot(a, b, preferred_element_type=F32)


def _dot_nt(a, b):
    return lax.dot_general(a, b, (((1,), (1,)), ((), ())), preferred_element_type=F32)


def _dot_tn(a, b):
    return lax.dot_general(a, b, (((0,), (0,)), ((), ())), preferred_element_type=F32)


def _split2(x):
    x1 = _bf(x)
    x2 = _bf(x - x1.astype(F32))
    return x1, x2


def _dot_x2(x, m):
    x1, x2 = _split2(x)
    return _dot(x1, m) + _dot(x2, m)


def _dot_m2(m, x):
    x1, x2 = _split2(x)
    return _dot(m, x1) + _dot(m, x2)


def _sigmoid(x):
    return 1.0 / (1.0 + jnp.exp(-x))


def _silu(x):
    return x * _sigmoid(x)


def _softplus(x):
    return jnp.maximum(x, 0.0) + jnp.log(1.0 + jnp.exp(-jnp.abs(x)))


def _stack_heads(x, hm_ref):
    xb = _bf(x)
    rows = x.shape[0]
    return jnp.concatenate([xb * hm_ref[h, 0:rows, :] for h in range(N_HEADS)], axis=0)


def _head_rmsnorm_gate(o, g_row, gate, hs):
    ms = _dot(_bf(o * o), hs) * (1.0 / HEAD_DIM)
    return o * lax.rsqrt(ms + EPS) * g_row * _silu(gate)


def _s5_groups(tb, nb):
    n_grp = max(1, SUBLANES // nb)
    n_s5 = tb // S5_LAGS
    assert n_s5 % n_grp == 0
    return n_grp, n_s5 // n_grp


def _interleave(feeder, consumers):
    live = []
    feeding = True
    while feeding or live:
        if feeding:
            try:
                tag = next(feeder)
                while tag is not None:
                    live.append(consumers.pop(tag))
                    tag = next(feeder)
            except StopIteration:
                feeding = False
        for g in list(live):
            try:
                next(g)
            except StopIteration:
                live.remove(g)
    assert not consumers


def _layer_kernel(
        x_ref, cos_ref, sin_ref,
        gpre_ref, gpost_ref, wrot_ref, wmain_ref, wtail_ref, wgg_ref, wout_ref,
        hm_ref, hmqk_ref, hmg_ref, hs_ref, bd_ref, bdqk_ref, bdg_ref,
        tril_ref, strict_ref, eye_ref, eyer_ref, btri_ref, selc_ref, sperm_ref,
        rdall_ref, rqdec_ref, rkdec_ref, rcd_ref, rnorm_ref,
        dconv_ref, darow_ref, dbias_ref, debeta_ref, deg_ref, dnorm_ref,
        skcat_ref, sbbar_ref, scbd_ref, stre_ref, stim_ref, sa_ref, scar_ref,
        swglu_ref, sbglu_ref,
        gwgk_ref, gb_ref, gnorm_ref,
        out_ref,
        h_ref, pr_ref, pd_ref, ps_ref, pg_ref, y_ref, xc_ref, qkv_ref, ucat_ref,
        sret_ref, sdn_ref, sgla_ref, ss5_ref,
        *, tb, nb):
    t_idx = pl.program_id(1)
    n_chunks = tb // CHUNK
    n_s5 = tb // S5_LAGS
    n_grp, n_stp = _s5_groups(tb, nb)
    batches = range(nb)
    rr = nb * tb

    def rb(b):
        return slice(b * tb, (b + 1) * tb)

    units = [(b, c, slice(b * tb + c * CHUNK, b * tb + (c + 1) * CHUNK))
             for c in range(n_chunks) for b in batches]
    n_units = len(units)

    @pl.when(t_idx == 0)
    def _reset():
        sret_ref[...] = jnp.zeros_like(sret_ref)
        sdn_ref[...] = jnp.zeros_like(sdn_ref)
        sgla_ref[...] = jnp.zeros_like(sgla_ref)
        ss5_ref[...] = jnp.zeros_like(ss5_ref)
        for b in batches:
            xc_ref[b] = jnp.zeros((SUBLANES, 768), F32)

    for b in batches:
        x = x_ref[b]
        h_ref[rb(b), :] = _bf(x * lax.rsqrt(jnp.mean(x * x, axis=-1, keepdims=True) + EPS) * gpre_ref[...])

    def projections():
        weights = {"rot": wrot_ref, "main": wmain_ref, "tail": wtail_ref, "gg": wgg_ref}
        for name, dst_ref in (("deltanet", pd_ref), ("s5", ps_ref), ("gla", pg_ref), ("retention", pr_ref)):
            c0 = 0
            for operand, first, width in PROJ_GROUPS[name]:
                dst_ref[:, c0:c0 + width] = _dot(h_ref[...], weights[operand][:, first:first + width])
                c0 += width
                yield None
            yield name

    o_partial = []

    def out_project(branch):
        rows = slice(branch * BRANCH_W, (branch + 1) * BRANCH_W)
        o_partial.append(_dot(y_ref[:, rows], wout_ref[rows, :]))

    hs = hs_ref[...]
    bd = bd_ref[...]
    bd32 = bd.astype(F32)

    def retention():
        rc = min(RET_CHUNK, tb)
        n_rc = tb // rc
        runits = [(b, c, slice(b * tb + c * rc, b * tb + (c + 1) * rc)) for c in range(n_rc) for b in batches]
        ret_q, ret_k, ret_v = {}, {}, {}
        for b, c, sl in runits:
            cs = cos_ref[c * rc:(c + 1) * rc, :]
            sn = sin_ref[c * rc:(c + 1) * rc, :]
            qa = pr_ref[sl, R_Q:R_Q + LANE]
            qb = pr_ref[sl, R_Q + LANE:R_Q + 2 * LANE]
            ka = pr_ref[sl, R_K:R_K + LANE]
            kb = pr_ref[sl, R_K + LANE:R_K + 2 * LANE]
            ret_q[b, c] = jnp.concatenate([qa * cs - qb * sn, qa * sn + qb * cs], axis=1)
            ret_k[b, c] = jnp.concatenate([ka * cs - kb * sn, ka * sn + kb * cs], axis=1)
            ret_v[b, c] = pr_ref[sl, R_V:R_V + BRANCH_W]
            if b == nb - 1:
                yield
        o_intra, kv = {}, {}
        for b, c, sl in runits:
            s_all = _dot_nt(_bf(ret_q[b, c]), _stack_heads(ret_k[b, c], hmqk_ref))
            o_intra[b, c] = _dot(_bf(s_all * rdall_ref[...]), _stack_heads(ret_v[b, c], hm_ref))
            kv[b, c] = bdqk_ref[...] * _dot_tn(_bf(ret_v[b, c]), _bf(ret_k[b, c] * rkdec_ref[...]))
            if b == nb - 1:
                yield
        st = [sret_ref[b] for b in batches]
        o_ret = {}
        for b, c, sl in runits:
            o_ret[b, c] = o_intra[b, c] + _dot_nt(_bf(ret_q[b, c] * rqdec_ref[...]), _bf(st[b]))
            st[b] = st[b] * rcd_ref[...] + kv[b, c]
            if b == nb - 1:
                yield
        for b in batches:
            sret_ref[b] = st[b]
        o = jnp.concatenate([o_ret[b, c] for b in batches for c in range(n_rc)], axis=0)
        y_ref[:, 0:BRANCH_W] = _bf(_head_rmsnorm_gate(o, rnorm_ref[...], pr_ref[:, R_G:R_G + BRANCH_W], hs))
        yield
        out_project(0)

    def deltanet():
        sub = lax.broadcasted_iota(jnp.int32, (tb, 768), 0) % SUBLANES
        for b in batches:
            xb = pd_ref[rb(b), D_QKV:D_QKV + 768]
            xfull = jnp.concatenate([xc_ref[b], xb], axis=0)
            conv = dconv_ref[DN_CONV - 1:DN_CONV, :] * xb
            for s in range(1, DN_CONV):
                rolled = pltpu.roll(xfull.reshape(tb // SUBLANES + 1, SUBLANES, 768), s, axis=1)
                rolled = rolled.reshape(tb + SUBLANES, 768)
                shifted = jnp.where(sub >= s, rolled[SUBLANES:], rolled[0:tb])
                conv = conv + dconv_ref[DN_CONV - 1 - s:DN_CONV - s, :] * shifted
            qkv_ref[rb(b), :] = _silu(conv)
            xc_ref[b] = xb[tb - SUBLANES:tb]
            yield

        q = qkv_ref[:, 0:256]
        k = qkv_ref[:, 256:512]
        v = qkv_ref[:, 512:768]
        ss = _dot(_bf(jnp.concatenate([q * q, k * k], axis=0)), hs)
        qn = q * lax.rsqrt(ss[0:rr] + EPS) * (HEAD_DIM ** -0.5)
        kn = k * lax.rsqrt(ss[rr:2 * rr] + EPS)
        yield
        small = pd_ref[:, D_SMALL:D_SMALL + LANE]
        beta = _dot(_bf(_sigmoid(small)), debeta_ref[...])
        g_s = -darow_ref[...] * _softplus(small + dbias_ref[...])
        gcum_s = jnp.concatenate([_dot_m2(btri_ref[...], g_s[rb(b)]) for b in batches], axis=0)
        yield
        gcum = _dot_x2(gcum_s, deg_ref[...])
        grows = [_dot_m2(selc_ref[...], gcum[rb(b)] * eyer_ref[...]) for b in batches]
        yield
        dec, glast, kgt = [], [], []
        for b, c, sl in units:
            gc = gcum[sl]
            dec.append(jnp.exp(jnp.minimum(gc - grows[b][c:c + 1, :], 0.0)))
            glast.append(gc[CHUNK - 1:CHUNK, :])
            kgt.append(_bf((kn[sl] * jnp.exp(glast[-1] - gc)).T))
            if b == nb - 1:
                yield
        aa = []
        for b, _, sl in units:
            aa.append(_dot_nt(_bf(jnp.concatenate([kn[sl] * beta[sl], qn[sl]], axis=0)),
                              _stack_heads(kn[sl], hm_ref)))
            if b == nb - 1:
                yield
        attn = [aa[u][CHUNK:2 * CHUNK] * dec[u] * tril_ref[...] for u in range(n_units)]

        def blockdiag(m):
            mb = _bf(m)
            return jnp.concatenate([mb, mb, mb, mb], axis=0) * bd

        pw = [-(aa[u][0:CHUNK] * dec[u] * strict_ref[...]) for u in range(n_units)]
        t_all = [eye_ref[...] + pw[u] for u in range(n_units)]
        yield
        pw = [_dot(_bf(pw[u]), blockdiag(pw[u])) for u in range(n_units)]
        yield
        for level in range(1, 5):
            both = [_dot(_bf(jnp.concatenate([pw[u], t_all[u]], axis=0)), blockdiag(pw[u]))
                    for u in range(n_units)]
            pw = [both[u][0:CHUNK] for u in range(n_units)]
            t_all = [t_all[u] + both[u][CHUNK:2 * CHUNK] for u in range(n_units)]
            yield
        t_all = [t_all[u] + _dot(_bf(t_all[u]), blockdiag(pw[u])) for u in range(n_units)]
        yield
        uw = []
        for u, (b, _, sl) in enumerate(units):
            bsl = beta[sl]
            uw.append(_dot(_bf(t_all[u]), jnp.concatenate(
                [_stack_heads(v[sl] * bsl, hm_ref),
                 _stack_heads(kn[sl] * bsl * jnp.exp(gcum[sl]), hm_ref)], axis=1)))
            if b == nb - 1:
                yield

        st = [sdn_ref[b] for b in batches]
        o_parts = {}
        for u, (b, c, sl) in enumerate(units):
            wq = _dot(_bf(jnp.concatenate([uw[u][:, 256:512], qn[sl] * jnp.exp(gcum[sl])], axis=0)), _bf(st[b]))
            v_new = uw[u][:, 0:256] - wq[0:CHUNK]
            o_parts[(b, c)] = wq[CHUNK:2 * CHUNK] + _dot(_bf(attn[u]), _stack_heads(v_new, hm_ref))
            st[b] = st[b] * jnp.exp(glast[u]) + bd32 * _dot(kgt[u], _bf(v_new))
            if b == nb - 1:
                yield
        for b in batches:
            sdn_ref[b] = st[b]
        o = jnp.concatenate([o_parts[(b, c)] for b in batches for c in range(n_chunks)], axis=0)
        y_ref[:, BRANCH_W:2 * BRANCH_W] = _bf(
            _head_rmsnorm_gate(o, dnorm_ref[...], pd_ref[:, D_G:D_G + BRANCH_W], hs))
        yield
        out_project(1)

    def s5():
        u_in = ps_ref[:, S_U:S_U + BRANCH_W]
        row_in_chunk = lax.broadcasted_iota(jnp.int32, (rr, BRANCH_W), 0) % S5_LAGS
        ucat_ref[:, 0:BRANCH_W] = _bf(u_in)
        for tau in range(1, S5_LAGS):
            shifted = pltpu.roll(u_in.reshape(rr // S5_LAGS, S5_LAGS, BRANCH_W), tau, axis=1).reshape(rr, BRANCH_W)
            ucat_ref[:, tau * BRANCH_W:(tau + 1) * BRANCH_W] = _bf(jnp.where(row_in_chunk >= tau, shifted, 0.0))
            if tau % 4 == 0:
                yield
        yield
        u_perm = _bf(_dot(sperm_ref[...], _bf(u_in)))
        n_cb = nb * n_s5
        hw = S5_HALF // 2
        z_parts_re, z_parts_im = [], []
        for half in range(2):
            lo = half * hw
            x_re = _dot(u_perm, sbbar_ref[:, lo:lo + hw])
            x_im = _dot(u_perm, sbbar_ref[:, S5_HALF + lo:S5_HALF + lo + hw])
            yield
            a1_re = sa_ref[0:1, lo:lo + hw]
            a1_im = sa_ref[1:2, lo:lo + hw]
            zh_re = x_re[0:n_cb]
            zh_im = x_im[0:n_cb]
            for j in range(1, S5_LAGS):
                xj_re = x_re[j * n_cb:(j + 1) * n_cb]
                xj_im = x_im[j * n_cb:(j + 1) * n_cb]
                zh_re, zh_im = a1_re * zh_re - a1_im * zh_im + xj_re, a1_re * zh_im + a1_im * zh_re + xj_im
                if j % 4 == 3:
                    yield
            z_parts_re.append(zh_re)
            z_parts_im.append(zh_im)
            yield
        z_re = jnp.concatenate(z_parts_re, axis=1)
        z_im = jnp.concatenate(z_parts_im, axis=1)
        y_s5 = jnp.concatenate([_dot(ucat_ref[rb(b), :], skcat_ref[...]) for b in batches], axis=0)
        yield
        grp_rows = n_grp * nb
        al_re = sa_ref[2:3, :]
        al_im = sa_ref[3:4, :]
        s_re = jnp.zeros((grp_rows, S5_HALF), F32)
        s_im = jnp.zeros((grp_rows, S5_HALF), F32)
        loc_re, loc_im = [], []
        for m in range(n_stp):
            loc_re.append(s_re)
            loc_im.append(s_im)
            zm_re = z_re[m * grp_rows:(m + 1) * grp_rows]
            zm_im = z_im[m * grp_rows:(m + 1) * grp_rows]
            s_re, s_im = al_re * s_re - al_im * s_im + zm_re, al_re * s_im + al_im * s_re + zm_im
            if m % 2 == 1:
                yield
        ag_re = scar_ref[2 * n_stp:2 * n_stp + 1, :]
        ag_im = scar_ref[2 * n_stp + 1:2 * n_stp + 2, :]
        c_re = ss5_ref[0]
        c_im = ss5_ref[1]
        car_re, car_im = [], []
        for g in range(n_grp):
            car_re.append(c_re)
            car_im.append(c_im)
            e_re = s_re[g * nb:(g + 1) * nb]
            e_im = s_im[g * nb:(g + 1) * nb]
            c_re, c_im = ag_re * c_re - ag_im * c_im + e_re, ag_re * c_im + ag_im * c_re + e_im
        ss5_ref[0] = c_re
        ss5_ref[1] = c_im
        yield
        car_re = jnp.concatenate(car_re, axis=0)
        car_im = jnp.concatenate(car_im, axis=0)
        sp_re, sp_im = [], []
        for m in range(n_stp):
            p_re = scar_ref[m:m + 1, :]
            p_im = scar_ref[n_stp + m:n_stp + m + 1, :]
            sp_re.append(loc_re[m] + p_re * car_re - p_im * car_im)
            sp_im.append(loc_im[m] + p_re * car_im + p_im * car_re)
        yield
        sp_re = jnp.concatenate(sp_re, axis=0)[:, None, :]
        sp_im = jnp.concatenate(sp_im, axis=0)[:, None, :]
        t_re = stre_ref[...][None]
        t_im = stim_ref[...][None]

        def batch_major(v):
            tiles = []
            for b in batches:
                for n in range(n_s5):
                    g, m = divmod(n, n_stp)
                    src = (m * n_grp + g) * nb + b
                    tiles.append(v[src * S5_LAGS:(src + 1) * S5_LAGS])
            return jnp.concatenate(tiles, axis=0)

        v_re = batch_major((t_re * sp_re - t_im * sp_im).reshape(rr, S5_HALF))
        yield
        v_im = batch_major((t_re * sp_im + t_im * sp_re).reshape(rr, S5_HALF))
        yield
        vv = _bf(jnp.concatenate([v_re, v_im], axis=1))
        y_s5 = y_s5 + jnp.concatenate([_dot(vv[rb(b)], scbd_ref[...]) for b in batches], axis=0)
        yield
        c0 = math.sqrt(2.0 / math.pi)
        y_s5 = 0.5 * y_s5 * (1.0 + jnp.tanh(c0 * (y_s5 + 0.044715 * (y_s5 * y_s5 * y_s5))))
        y_s5 = y_s5 * _sigmoid(_dot(_bf(y_s5), swglu_ref[...]) + sbglu_ref[...])
        y_ref[:, 2 * BRANCH_W:3 * BRANCH_W] = _bf(y_s5 * _silu(ps_ref[:, S_G:S_G + BRANCH_W]))
        yield
        out_project(2)

    def gla():
        z = _dot(_bf(pg_ref[:, L_CODE:L_CODE + LANE]), gwgk_ref[...]) + gb_ref[...]
        gk = -_softplus(-z) * (1.0 / GLA_GATE_TAU)
        yield
        cum = jnp.concatenate([_dot_m2(btri_ref[...], gk[rb(b)]) for b in batches], axis=0)
        yield
        q = pg_ref[:, L_Q:L_Q + GLA_QK]
        k = pg_ref[:, L_K:L_K + GLA_QK]
        v = pg_ref[:, L_V:L_V + BRANCH_W]
        qt = _bf(q * jnp.exp(cum) * (GLA_DK ** -0.5))
        kt = k * jnp.exp(-cum)
        clast = [cum[sl][CHUNK - 1:CHUNK, :] for _, _, sl in units]
        yield
        s_all = []
        for b, _, sl in units:
            s_all.append(_dot_nt(qt[sl], _stack_heads(kt[sl], hmg_ref)))
            if b == nb - 1:
                yield
        o_intra = []
        for u, (b, _, sl) in enumerate(units):
            o_intra.append(_dot(_bf(s_all[u] * tril_ref[...]), _stack_heads(v[sl], hm_ref)))
            if b == nb - 1:
                yield
        kv = []
        for u, (b, _, sl) in enumerate(units):
            kv.append(bdg_ref[...] * _dot_tn(_bf(v[sl]), _bf(k[sl] * jnp.exp(clast[u] - cum[sl]))))
            if b == nb - 1:
                yield
        st = [sgla_ref[b] for b in batches]
        o_parts = {}
        for u, (b, c, sl) in enumerate(units):
            o_parts[(b, c)] = o_intra[u] + _dot_nt(qt[sl], _bf(st[b]))
            st[b] = st[b] * jnp.exp(clast[u]) + kv[u]
            if b == nb - 1:
                yield
        for b in batches:
            sgla_ref[b] = st[b]
        o = jnp.concatenate([o_parts[(b, c)] for b in batches for c in range(n_chunks)], axis=0)
        y_ref[:, 3 * BRANCH_W:4 * BRANCH_W] = _bf(
            _head_rmsnorm_gate(o, gnorm_ref[...], pg_ref[:, L_G:L_G + BRANCH_W], hs))
        yield
        out_project(3)

    _interleave(projections(), {"s5": s5(), "deltanet": deltanet(), "retention": retention(), "gla": gla()})

    o = o_partial[0] + o_partial[1] + o_partial[2] + o_partial[3]
    o = o * lax.rsqrt(jnp.mean(o * o, axis=-1, keepdims=True) + EPS) * gpost_ref[...]
    for b in batches:
        out_ref[b] = x_ref[b] + o[rb(b)]


def _in_proj_operands(w_in):
    dp, dm, _ = w_in.shape
    main = _bf(w_in)

    def deinterleave(o):
        t = w_in[:, :, o:o + 256].reshape(dp, dm, N_HEADS, HEAD_DIM // 2, 2)
        return jnp.transpose(t, (0, 1, 4, 2, 3)).reshape(dp, dm, 256)

    rot = _bf(jnp.concatenate([deinterleave(O_RQ) * (HEAD_DIM ** -0.5), deinterleave(O_RK)], axis=2))
    return dict(wrot=rot, wmain=main, wtail=main[:, :, O_DG:O_DG + TAIL_COLS], wgg=main[:, :, O_GG:O_GG + 256])


@functools.lru_cache(maxsize=None)
def _constant_tables(seq_len, tb, nb):
    c = CHUNK
    lane = np.arange(256)
    head_std = lane // 64
    head_qk = (lane % 128) // 32
    head_g = np.arange(128) // 32
    i = np.arange(c)[:, None]
    j = np.arange(256)[None, :] % 64
    t = {}
    heads = np.arange(4)[:, None, None]
    t["hm"] = np.broadcast_to(head_std[None, None, :] == heads, (4, tb, 256)).astype(np.float32)
    t["hmqk"] = np.broadcast_to(head_qk[None, None, :] == heads, (4, tb, 256)).astype(np.float32)
    t["hmg"] = np.broadcast_to(head_g[None, None, :] == heads, (4, c, 128)).astype(np.float32)
    t["hs"] = (head_std[:, None] == head_std[None, :]).astype(np.float32)
    t["bd"] = t["hs"]
    t["bdqk"] = (head_std[:, None] == head_qk[None, :]).astype(np.float32)
    t["bdg"] = (head_std[:, None] == head_g[None, :]).astype(np.float32)
    t["tril"] = (i >= j).astype(np.float32)
    t["strict"] = (i > j).astype(np.float32)
    t["eye"] = (i == j).astype(np.float32)
    r = np.arange(tb)
    same_chunk = (r[:, None] // c) == (r[None, :] // c)
    t["btri"] = (same_chunk & (r[:, None] >= r[None, :])).astype(np.float32)
    t["selc"] = ((r[None, :] // c) == np.arange(16)[:, None]).astype(np.float32)
    n_grp, n_stp = _s5_groups(tb, nb)
    jj, mm, gg, bb = np.meshgrid(np.arange(S5_LAGS), np.arange(n_stp), np.arange(n_grp), np.arange(nb),
                                 indexing="ij")
    old = (bb * tb + (gg * n_stp + mm) * S5_LAGS + jj).reshape(-1)
    sperm = np.zeros((nb * tb, nb * tb), np.float32)
    sperm[np.arange(nb * tb), old] = 1.0
    t["sperm"] = sperm
    t["eyer"] = np.tile(t["eye"], (tb // c, 1))
    rc = min(RET_CHUNK, tb)
    lg = np.log(1.0 - 2.0 ** (-5.0 - np.arange(4, dtype=np.float64)))
    ri = np.arange(rc)[:, None]
    rj = np.arange(4 * rc)[None, :] % rc
    lg_cols = lg[np.arange(4 * rc) // rc][None, :]
    t["rdall"] = np.where(ri >= rj, np.exp(lg_cols * np.where(ri >= rj, ri - rj, 0)), 0.0).astype(np.float32)
    lg_qk = lg[head_qk][None, :]
    t["rqdec"] = np.exp(lg_qk * (ri + 1.0)).astype(np.float32)
    t["rkdec"] = np.exp(lg_qk * (rc - 1.0 - ri)).astype(np.float32)
    t["rcd"] = np.exp(lg_qk * rc).astype(np.float32)
    inv = ROPE_BASE ** (-np.arange(0, HEAD_DIM, 2, dtype=np.float64) / HEAD_DIM)
    ang = np.arange(seq_len, dtype=np.float64)[:, None] * inv[None, :]
    t["cos"] = np.tile(np.cos(ang), (1, 4)).astype(np.float32)
    t["sin"] = np.tile(np.sin(ang), (1, 4)).astype(np.float32)
    t["tile16"] = np.tile(np.eye(S5_GROUP, dtype=np.float32), (1, S5_GROUPS))
    t["tile64"] = np.tile(np.eye(S5_STATE, dtype=np.float32), (1, S5_GROUPS))
    return t


def _s5_tables(lam_re, lam_im, b_re, b_im, c_re, c_im, d, log_dt, tabs, n_stp):
    hp = lax.Precision.HIGHEST
    g, p, hc = S5_GROUPS, S5_STATE, S5_GROUP
    nl = lam_re.shape[0]
    lam_re, lam_im = lam_re.astype(F32), lam_im.astype(F32)
    dt = jnp.exp(log_dt.astype(F32))[..., None]
    mag = jnp.exp(lam_re * dt)
    ang = lam_im * dt
    a_re, a_im = mag * jnp.cos(ang), mag * jnp.sin(ang)
    den = lam_re * lam_re + lam_im * lam_im
    nr, ni = a_re - 1.0, a_im
    coef_re = (nr * lam_re + ni * lam_im) / den
    coef_im = (ni * lam_re - nr * lam_im) / den
    b_re, b_im = b_re.astype(F32), b_im.astype(F32)
    bb_re = coef_re[..., None] * b_re - coef_im[..., None] * b_im
    bb_im = coef_re[..., None] * b_im + coef_im[..., None] * b_re

    def apow(n):
        n = jnp.asarray(n, F32)[None, :, None, None]
        m = jnp.exp((lam_re * dt)[:, None] * n)
        return m * jnp.cos(ang[:, None] * n), m * jnp.sin(ang[:, None] * n)

    lags = np.arange(S5_LAGS)
    p_re, p_im = apow(lags)
    c_re, c_im = c_re.astype(F32), c_im.astype(F32)
    ab_re = p_re[..., None] * bb_re[:, None] - p_im[..., None] * bb_im[:, None]
    ab_im = p_re[..., None] * bb_im[:, None] + p_im[..., None] * bb_re[:, None]
    kk = (jnp.einsum('ntgpi,ngop->ntgio', ab_re, c_re, precision=hp)
          - jnp.einsum('ntgpi,ngop->ntgio', ab_im, c_im, precision=hp))
    kk = kk.at[:, 0].add(d.astype(F32)[..., None] * jnp.eye(hc, dtype=F32))
    grp256 = np.arange(g * hc) // hc
    grp1024 = np.arange(g * p) // p
    tile16 = jnp.asarray(tabs["tile16"])
    tile64 = jnp.asarray(tabs["tile64"])
    kcat = jnp.einsum('nro,oc->nrc', kk.reshape(nl, S5_LAGS * g * hc, hc), tile16, precision=hp)
    kcat = jnp.where(jnp.asarray(np.tile(grp256, S5_LAGS)[:, None] == grp256[None, :]), kcat, 0.0)

    def rows_to_state(bb):
        m = jnp.einsum('nrp,pc->nrc', jnp.transpose(bb, (0, 1, 3, 2)).reshape(nl, g * hc, p), tile64,
                       precision=hp)
        return jnp.where(jnp.asarray(grp256[:, None] == grp1024[None, :]), m, 0.0)

    def state_to_rows(cc):
        m = jnp.einsum('nro,oc->nrc', jnp.transpose(cc, (0, 1, 3, 2)).reshape(nl, g * p, hc), tile16,
                       precision=hp)
        return jnp.where(jnp.asarray(grp1024[:, None] == grp256[None, :]), m, 0.0)

    bbar = jnp.concatenate([rows_to_state(bb_re), rows_to_state(bb_im)], axis=2)
    cbd = jnp.concatenate([state_to_rows(c_re), -state_to_rows(c_im)], axis=1)
    t_re, t_im = apow(lags + 1)
    a1_re, a1_im = apow(np.array([1]))
    al_re, al_im = apow(np.array([S5_LAGS]))
    pm_re, pm_im = apow(S5_LAGS * np.arange(n_stp))
    ag_re, ag_im = apow(np.array([S5_LAGS * n_stp]))
    flat = lambda z: z.reshape(nl, z.shape[1], g * p)
    return dict(skcat=_bf(kcat), sbbar=_bf(bbar), scbd=_bf(cbd),
                stre=flat(t_re), stim=flat(t_im),
                sa=jnp.concatenate([flat(a1_re), flat(a1_im), flat(al_re), flat(al_im)], axis=1),
                scar=jnp.concatenate([flat(pm_re), flat(pm_im), flat(ag_re), flat(ag_im)], axis=1))


def _const_spec(arr):
    nd = arr.ndim
    return pl.BlockSpec(arr.shape, lambda b, t, _nd=nd: (0,) * _nd)


def _layer_spec(arr, layer):
    nd = arr.ndim - 1
    return pl.BlockSpec((None,) + arr.shape[1:], lambda b, t, _nd=nd, _l=layer: (_l,) + (0,) * _nd)


def _layer_call(x, layer, stacked, const_inputs, cos, sin, tb, nb):
    bsz, seq, _ = x.shape
    grid = (bsz // nb, seq // tb)
    rr = nb * tb
    in_specs = [pl.BlockSpec((nb, tb, D_MODEL), lambda b, t: (b, t, 0)),
                pl.BlockSpec((tb, LANE), lambda b, t: (t, 0)),
                pl.BlockSpec((tb, LANE), lambda b, t: (t, 0))]
    operands = [x, cos, sin]
    order = ["gpre", "gpost", "wrot", "wmain", "wtail", "wgg", "wout",
             "hm", "hmqk", "hmg", "hs", "bd", "bdqk", "bdg", "tril", "strict", "eye", "eyer", "btri", "selc", "sperm",
             "rdall", "rqdec", "rkdec", "rcd", "rnorm",
             "dconv", "darow", "dbias", "debeta", "deg", "dnorm",
             "skcat", "sbbar", "scbd", "stre", "stim", "sa", "scar", "swglu", "sbglu",
             "gwgk", "gb", "gnorm"]
    for name in order:
        if name in stacked:
            operands.append(stacked[name])
            in_specs.append(_layer_spec(stacked[name], layer))
        else:
            operands.append(const_inputs[name])
            in_specs.append(_const_spec(const_inputs[name]))
    scratch = [
        pltpu.VMEM((rr, D_MODEL), BF16),
        pltpu.VMEM((rr, G_RET), F32),
        pltpu.VMEM((rr, G_DN), F32),
        pltpu.VMEM((rr, G_S5), F32),
        pltpu.VMEM((rr, G_GLA), F32),
        pltpu.VMEM((rr, D_MODEL), BF16),
        pltpu.VMEM((nb, SUBLANES, 768), F32),
        pltpu.VMEM((rr, 768), F32),
        pltpu.VMEM((rr, S5_LAGS * BRANCH_W), BF16),
        pltpu.VMEM((nb, BRANCH_W, BRANCH_W), F32),
        pltpu.VMEM((nb, BRANCH_W, BRANCH_W), F32),
        pltpu.VMEM((nb, BRANCH_W, GLA_QK), F32),
        pltpu.VMEM((2, nb, S5_HALF), F32),
    ]
    return pl.pallas_call(
        functools.partial(_layer_kernel, tb=tb, nb=nb),
        grid=grid,
        in_specs=in_specs,
        out_specs=pl.BlockSpec((nb, tb, D_MODEL), lambda b, t: (b, t, 0)),
        out_shape=jax.ShapeDtypeStruct(x.shape, x.dtype),
        scratch_shapes=scratch,
        compiler_params=pltpu.CompilerParams(
            dimension_semantics=("arbitrary", "arbitrary"),
            vmem_limit_bytes=VMEM_LIMIT_BYTES),
        name="hybrid_layer",
    )(*operands)


def _pick_tiles(bsz, seq):
    nb = 2 if bsz % 2 == 0 else 1
    for tb in (256, 128, 64):
        if seq % tb == 0:
            return tb, nb
    raise ValueError(f"sequence length {seq} must be a multiple of {CHUNK}")


def kernel(x, norm_pre, norm_post, w_in, w_out, ret_norm, dn_conv, dn_a_log, dn_dt_bias, dn_norm,
           s5_lam_re, s5_lam_im, s5_b_re, s5_b_im, s5_c_re, s5_c_im, s5_d, s5_log_dt, s5_w_glu, s5_b_glu,
           gla_w_gk, gla_b_gk, gla_norm):
    bsz, seq, dm = x.shape
    depth = w_in.shape[0]
    assert dm == D_MODEL and x.dtype == F32
    tb, nb = _pick_tiles(bsz, seq)
    tabs = _constant_tables(seq, tb, nb)
    bf_names = ("hm", "hmqk", "hmg", "hs", "bd", "btri", "selc", "sperm")
    skip = ("cos", "sin", "tile16", "tile64")
    const_inputs = {k: jnp.asarray(v, BF16 if k in bf_names else F32)
                    for k, v in tabs.items() if k not in skip}
    cos = jnp.asarray(tabs["cos"])
    sin = jnp.asarray(tabs["sin"])

    w_in_ops = _in_proj_operands(w_in)
    w_out_b = _bf(w_out)

    e_beta = np.zeros((LANE, 256), np.float32)
    e_g = np.zeros((LANE, 256), np.float32)
    for hh in range(N_HEADS):
        e_beta[SM_BETA + hh, 64 * hh:64 * hh + 64] = 1.0
        e_g[SM_A + hh, 64 * hh:64 * hh + 64] = 1.0
    e_beta = jnp.asarray(e_beta, BF16)
    e_g = jnp.asarray(e_g, BF16)

    a_rows = jnp.zeros((depth, 1, LANE), F32).at[:, 0, SM_A:SM_A + 4].set(jnp.exp(dn_a_log.astype(F32)))
    b_rows = jnp.zeros((depth, 1, LANE), F32).at[:, 0, SM_A:SM_A + 4].set(dn_dt_bias.astype(F32))
    wgk = jnp.zeros((depth, LANE, GLA_QK), BF16).at[:, 0:GLA_GATE_RANK, :].set(_bf(gla_w_gk))
    s5 = _s5_tables(s5_lam_re, s5_lam_im, s5_b_re, s5_b_im, s5_c_re, s5_c_im, s5_d, s5_log_dt, tabs,
                    _s5_groups(tb, nb)[1])
    tile4 = lambda g: jnp.tile(g.astype(F32), (1, N_HEADS)).reshape(depth, 1, 256)
    rnorm, dnorm, gnorm = tile4(ret_norm), tile4(dn_norm), tile4(gla_norm)
    swglu = _bf(s5_w_glu)

    stacked = dict(
        gpre=norm_pre.reshape(depth, 1, dm).astype(F32),
        gpost=norm_post.reshape(depth, 1, dm).astype(F32),
        wout=w_out_b,
        rnorm=rnorm,
        dconv=dn_conv.astype(F32),
        darow=a_rows, dbias=b_rows,
        dnorm=dnorm,
        swglu=swglu, sbglu=s5_b_glu.reshape(depth, 1, 256).astype(F32),
        gwgk=wgk, gb=gla_b_gk.reshape(depth, 1, GLA_QK).astype(F32),
        gnorm=gnorm,
    )
    stacked.update(s5)
    stacked.update(w_in_ops)
    const_inputs.update(debeta=e_beta, deg=e_g)
    for i in range(depth):
        x = _layer_call(x, i, stacked, const_inputs, cos, sin, tb, nb)
    return x
```
